```python
import math
import jax, jax.numpy as jnp
from jax import lax
import numpy as np

D_MODEL = 1024
BATCH = 8
SEQ = 2048
DEPTH = 2
DEC_BATCH = 128
DEC_SEQ = 1
PAST_LEN = 2048
PAGE_SIZE = 128

MIX_WIDTH = D_MODEL
ATT_HEADS = 4
ATT_V_DIM = MIX_WIDTH // 2 // ATT_HEADS
ATT_QK_DIM = ATT_V_DIM // 2
SSM_INNER = MIX_WIDTH // 2
SSM_HEAD_DIM = 64
SSM_HEADS = SSM_INNER // SSM_HEAD_DIM
SSM_GROUPS = 2
SSM_STATE = 128
CONV_WIDTH = 4
CONV_DIM = SSM_INNER + 2 * SSM_GROUPS * SSM_STATE
SSD_CHUNK = 128
Q_COLS = ATT_HEADS * 2 * ATT_QK_DIM
K_COLS = ATT_HEADS * 2 * ATT_QK_DIM
V_COLS = ATT_HEADS * ATT_V_DIM
Z_COLS = SSM_INNER
IN_COLS = Q_COLS + K_COLS + V_COLS + Z_COLS + CONV_DIM + SSM_HEADS
IN_SPLITS = (Q_COLS, Q_COLS + K_COLS, Q_COLS + K_COLS + V_COLS,
             Q_COLS + K_COLS + V_COLS + Z_COLS, Q_COLS + K_COLS + V_COLS + Z_COLS + CONV_DIM)
N_MEM = 256
MEM_HEADS = 4
MEM_HEAD_DIM = D_MODEL // MEM_HEADS
D_FF = 2816
N_EXPERTS = 8
TOP_K = 2
D_FF_EXPERT = 2816
N_DENSE = (DEPTH + 1) // 2
N_MOE = DEPTH // 2
ROPE_THETA = 10000.0
Q_BLOCK = 128
LN_EPS = 1e-5
NORM_EPS = 1e-5
DEEPNORM_ALPHA = (2 * DEPTH) ** 0.25
DEEPNORM_BETA = (8 * DEPTH) ** -0.25

kernel_name = "hymba_diffattn_ssd_deepnorm_moe_step"


def _layer_norm(x, g, b):
    xf = x.astype(jnp.float32)
    mu = jnp.mean(xf, -1, keepdims=True)
    xc = xf - mu
    var = jnp.mean(xc * xc, -1, keepdims=True)
    return (xc * lax.rsqrt(var + LN_EPS)).astype(x.dtype) * g + b


def _rms_norm(x, w):
    xf = x.astype(jnp.float32)
    return (xf * lax.rsqrt(jnp.mean(xf * xf, -1, keepdims=True) + NORM_EPS)).astype(x.dtype) * w


def _rope(x, pos):
    d = x.shape[-1]
    inv = ROPE_THETA ** (-jnp.arange(0, d, 2, dtype=jnp.float32) / d)
    ang = pos.astype(jnp.float32)[:, None] * inv[None, :]
    cos, sin = jnp.cos(ang)[:, None, :], jnp.sin(ang)[:, None, :]
    xf = x.astype(jnp.float32)
    x1, x2 = xf[..., : d // 2], xf[..., d // 2:]
    return jnp.concatenate([x1 * cos - x2 * sin, x2 * cos + x1 * sin], -1).astype(x.dtype)


def _diff_attention(q, k, v, q_pos, k_pos, lam):
    b, sq = q.shape[:2]
    bq = min(Q_BLOCK, sq)
    n_blk = -(-sq // bq)
    pad = n_blk * bq - sq
    q = jnp.pad(q, ((0, 0), (0, pad), (0, 0), (0, 0), (0, 0)))
    q_pos = jnp.pad(q_pos, (0, pad), mode='edge')
    qb = q.reshape(b, n_blk, bq, *q.shape[2:]).swapaxes(0, 1)
    pb = q_pos.reshape(n_blk, bq)
    scale = ATT_QK_DIM ** -0.5

    def block(args):
        qi, pi = args
        s = jnp.einsum('bqhmd,bkhmd->bhmqk', qi, k, preferred_element_type=jnp.float32) * scale
        s = jnp.where((k_pos[None, :] <= pi[:, None])[None, None, None], s, -jnp.inf)
        p = jax.nn.softmax(s, axis=-1)
        a = p[:, :, 0] - lam * p[:, :, 1]
        return jnp.einsum('bhqk,bkhd->bqhd', a.astype(v.dtype), v)

    out = lax.map(block, (qb, pb))
    return out.swapaxes(0, 1).reshape(b, n_blk * bq, *out.shape[3:])[:, :sq]


def _ssd(xh, dt, a_head, bmat, cmat, h0):
    b, L = xh.shape[:2]
    G, R, P, N = SSM_GROUPS, SSM_HEADS // SSM_GROUPS, SSM_HEAD_DIM, SSM_STATE
    q = min(SSD_CHUNK, L)
    nc = -(-L // q)
    pad = nc * q - L
    f32 = jnp.float32
    padl = lambda t: jnp.pad(t, ((0, 0), (0, pad)) + ((0, 0),) * (t.ndim - 2))
    x = padl(xh.astype(f32)).reshape(b, nc, q, G, R, P)
    dtc = padl(dt).reshape(b, nc, q, G, R)
    B = padl(bmat.astype(f32)).reshape(b, nc, q, G, N)
    C = padl(cmat.astype(f32)).reshape(b, nc, q, G, N)
    acs = jnp.cumsum(dtc * a_head.reshape(G, R), axis=2)
    xdt = x * dtc[..., None]
    seg = acs[:, :, :, None] - acs[:, :, None, :]
    causal = jnp.tril(jnp.ones((q, q), bool))[:, :, None, None]
    decay = jnp.exp(jnp.where(causal, seg, -jnp.inf))
    cb = jnp.einsum('bclgn,bcsgn->bclsg', C, B)
    y_diag = jnp.einsum('bclsgr,bcsgrp->bclgrp', cb[..., None] * decay, xdt)
    xw = xdt * jnp.exp(acs[:, :, -1:] - acs)[..., None]
    states = jnp.einsum('bclgn,bclgrp->bcgrpn', B, xw)
    chunk_decay = jnp.exp(acs[:, :, -1])

    def step(h, inp):
        st, dec = inp
        return h * dec[..., None, None] + st, h

    h_last, h_in = lax.scan(step, h0.astype(f32).reshape(b, G, R, P, N),
                            (states.swapaxes(0, 1), chunk_decay.swapaxes(0, 1)))
    h_in = h_in.swapaxes(0, 1)
    y_off = jnp.einsum('bclgn,bcgrpn->bclgrp', C, h_in) * jnp.exp(acs)[..., None]
    y = (y_diag + y_off).reshape(b, nc * q, SSM_HEADS, P)[:, :L]
    return y, h_last.reshape(b, SSM_HEADS, P, N)


def _mixer(x, pos, k_pos, past_k, past_v, conv_prev, h0, layer,
           w_in, conv_w, conv_b, dt_bias, a_log, d_skip, ssm_norm_w,
           lambda_params, attn_norm_w, w_out):
    b, L, _ = x.shape
    q, k, v, z, xbc, dt = jnp.split(x @ w_in, IN_SPLITS, axis=-1)
    q = _rope(q.reshape(b, L, 2 * ATT_HEADS, ATT_QK_DIM), pos)
    k = _rope(k.reshape(b, L, 2 * ATT_HEADS, ATT_QK_DIM), pos)
    v = v.reshape(b, L, ATT_HEADS, ATT_V_DIM)
    k_all = k if past_k is None else jnp.concatenate([past_k.astype(k.dtype), k], axis=1)
    v_all = v if past_v is None else jnp.concatenate([past_v.astype(v.dtype), v], axis=1)
    lam_init = 0.8 - 0.6 * math.exp(-0.3 * layer)
    lp = lambda_params.astype(jnp.float32)
    lam = jnp.exp(jnp.dot(lp[0], lp[1])) - jnp.exp(jnp.dot(lp[2], lp[3])) + lam_init
    att = _diff_attention(q.reshape(b, L, ATT_HEADS, 2, ATT_QK_DIM),
                          k_all.reshape(b, -1, ATT_HEADS, 2, ATT_QK_DIM), v_all, pos, k_pos, lam)
    att = (_rms_norm(att, attn_norm_w) * (1.0 - lam_init)).reshape(b, L, V_COLS)
    xbc_pad = jnp.concatenate([conv_prev.astype(xbc.dtype), xbc], axis=1)
    conv = conv_b + sum(xbc_pad[:, i:i + L] * conv_w[i] for i in range(CONV_WIDTH))
    new_conv = xbc_pad[:, -(CONV_WIDTH - 1):]
    conv = jax.nn.silu(conv)
    xs, bm, cm = jnp.split(conv, (SSM_INNER, SSM_INNER + SSM_GROUPS * SSM_STATE), axis=-1)
    dt = jax.nn.softplus(dt.astype(jnp.float32) + dt_bias.astype(jnp.float32))
    a_head = -jnp.exp(a_log.astype(jnp.float32))
    xh = xs.reshape(b, L, SSM_HEADS, SSM_HEAD_DIM)
    y, h_last = _ssd(xh, dt, a_head, bm.reshape(b, L, SSM_GROUPS, SSM_STATE),
                     cm.reshape(b, L, SSM_GROUPS, SSM_STATE), h0)
    y = y + d_skip.astype(jnp.float32)[:, None] * xh.astype(jnp.float32)
    y = y.astype(x.dtype).reshape(b, L, SSM_INNER) * jax.nn.silu(z)
    y = _rms_norm(y.reshape(b, L, SSM_GROUPS, SSM_INNER // SSM_GROUPS),
                  ssm_norm_w.reshape(SSM_GROUPS, -1)).reshape(b, L, SSM_INNER)
    out = jnp.concatenate([att, y], axis=-1) @ w_out
    return out, k, v, new_conv, h_last


def _mem_kv(mem, w_ckv):
    b = mem.shape[0]
    mk, mv = jnp.split(mem @ w_ckv, 2, axis=-1)
    return (mk.reshape(b, N_MEM, MEM_HEADS, MEM_HEAD_DIM), mv.reshape(b, N_MEM, MEM_HEADS, MEM_HEAD_DIM))


def _cross_attn(x, mem_k, mem_v, w_cq, w_co):
    b, L, _ = x.shape
    q = (x @ w_cq).reshape(b, L, MEM_HEADS, MEM_HEAD_DIM)
    s = jnp.einsum('bqhd,bkhd->bhqk', q, mem_k.astype(q.dtype),
                   preferred_element_type=jnp.float32) * (MEM_HEAD_DIM ** -0.5)
    p = jax.nn.softmax(s, axis=-1)
    o = jnp.einsum('bhqk,bkhd->bqhd', p.astype(x.dtype), mem_v.astype(x.dtype))
    return o.reshape(b, L, D_MODEL) @ w_co


def _swiglu(x, w_gu, w_down):
    g, u = jnp.split(x @ w_gu, 2, axis=-1)
    return (jax.nn.silu(g) * u) @ w_down


def _moe(x, w_router, w_exp_gu, w_exp_down):
    logits = (x @ w_router).astype(jnp.float32)
    top_v, top_i = lax.top_k(logits, TOP_K)
    gates = jax.nn.softmax(top_v, axis=-1)
    gate = jnp.sum(jax.nn.one_hot(top_i, N_EXPERTS, dtype=jnp.float32) * gates[..., None], axis=-2)
    out = jnp.zeros_like(x)
    for e in range(N_EXPERTS):
        out = out + _swiglu(x, w_exp_gu[e], w_exp_down[e]) * gate[..., e:e + 1].astype(x.dtype)
    return out


def _channel(x, layer, w_ffn_gu, w_ffn_down, w_router, w_exp_gu, w_exp_down):
    if layer % 2 == 0:
        return _swiglu(x, w_ffn_gu[layer // 2], w_ffn_down[layer // 2])
    return _moe(x, w_router[layer // 2], w_exp_gu[layer // 2], w_exp_down[layer // 2])


def setup_inputs(seed: int = 0) -> dict:
    key = jax.random.key(seed)
    ks = iter(jax.random.split(key, 48))
    f32 = jnp.float32
    n_pages = PAST_LEN // PAGE_SIZE
    n_used = DEC_BATCH * n_pages
    n_phys = n_used + n_used // 4

    def nrm(shape, scale=1.0):
        return jax.random.normal(next(ks), shape, f32) * scale

    def gain(shape):
        return 1.0 + nrm(shape, 0.02)

    page_table = jax.random.permutation(next(ks), n_phys)[:n_used].reshape(DEC_BATCH, n_pages).astype(jnp.int32)
    dt0 = jnp.exp(jax.random.uniform(next(ks), (DEPTH, SSM_HEADS), f32, math.log(1e-3), math.log(1e-1)))
    dt_bias = dt0 + jnp.log(-jnp.expm1(-dt0))
    a_log = jnp.log(jax.random.uniform(next(ks), (DEPTH, SSM_HEADS), f32, 1.0, 16.0))
    beta = DEEPNORM_BETA
    return {
        "x_prompt": nrm((BATCH, SEQ, D_MODEL)),
        "x_sample": nrm((DEC_BATCH, DEC_SEQ, D_MODEL)),
        "cache_attn_k": nrm((DEPTH, n_phys, PAGE_SIZE, 2 * ATT_HEADS, ATT_QK_DIM)),
        "cache_attn_v": nrm((DEPTH, n_phys, PAGE_SIZE, ATT_HEADS, ATT_V_DIM)),
        "cache_mem_k": nrm((DEPTH, DEC_BATCH, N_MEM, MEM_HEADS, MEM_HEAD_DIM)),
        "cache_mem_v": nrm((DEPTH, DEC_BATCH, N_MEM, MEM_HEADS, MEM_HEAD_DIM)),
        "state_conv": nrm((DEPTH, DEC_BATCH, CONV_WIDTH - 1, CONV_DIM)),
        "state_ssm": nrm((DEPTH, DEC_BATCH, SSM_HEADS, SSM_HEAD_DIM, SSM_STATE), 0.5),
        "page_table": page_table,
        "mem_prompt": nrm((BATCH, N_MEM, D_MODEL)),
        "ln_g": gain((DEPTH, 3, D_MODEL)),
        "ln_b": nrm((DEPTH, 3, D_MODEL), 0.02),
        "w_in": nrm((DEPTH, D_MODEL, IN_COLS), D_MODEL ** -0.5),
        "conv_w": nrm((DEPTH, CONV_WIDTH, CONV_DIM), CONV_WIDTH ** -0.5),
        "conv_b": nrm((DEPTH, CONV_DIM), 0.02),
        "dt_bias": dt_bias,
        "a_log": a_log,
        "d_skip": gain((DEPTH, SSM_HEADS)),
        "ssm_norm_w": gain((DEPTH, SSM_INNER)),
        "lambda_params": nrm((DEPTH, 4, ATT_QK_DIM), 0.1),
        "attn_norm_w": gain((DEPTH, ATT_V_DIM)),
        "w_out": nrm((DEPTH, MIX_WIDTH, D_MODEL), beta * MIX_WIDTH ** -0.5),
        "w_cq": nrm((DEPTH, D_MODEL, D_MODEL), D_MODEL ** -0.5),
        "w_ckv": nrm((DEPTH, D_MODEL, 2 * D_MODEL), D_MODEL ** -0.5),
        "w_co": nrm((DEPTH, D_MODEL, D_MODEL), beta * D_MODEL ** -0.5),
        "w_ffn_gu": nrm((N_DENSE, D_MODEL, 2 * D_FF), D_MODEL ** -0.5),
        "w_ffn_down": nrm((N_DENSE, D_FF, D_MODEL), beta * D_FF ** -0.5),
        "w_router": nrm((N_MOE, D_MODEL, N_EXPERTS), D_MODEL ** -0.5),
        "w_exp_gu": nrm((N_MOE, N_EXPERTS, D_MODEL, 2 * D_FF_EXPERT), D_MODEL ** -0.5),
        "w_exp_down": nrm((N_MOE, N_EXPERTS, D_FF_EXPERT, D_MODEL), beta * D_FF_EXPERT ** -0.5),
    }


def reference(x_prompt, x_sample, cache_attn_k, cache_attn_v, cache_mem_k, cache_mem_v,
              state_conv, state_ssm, page_table, mem_prompt, ln_g, ln_b, w_in, conv_w, conv_b,
              dt_bias, a_log, d_skip, ssm_norm_w, lambda_params, attn_norm_w, w_out,
              w_cq, w_ckv, w_co, w_ffn_gu, w_ffn_down, w_router, w_exp_gu, w_exp_down):
    bp, seq = x_prompt.shape[:2]
    bs, dec_seq = x_sample.shape[:2]
    past_len = page_table.shape[1] * cache_attn_k.shape[2]
    pos_p = jnp.arange(seq, dtype=jnp.int32)
    pos_s = past_len + jnp.arange(dec_seq, dtype=jnp.int32)
    kpos_s = jnp.arange(past_len + dec_seq, dtype=jnp.int32)
    hp, hs = x_prompt, x_sample
    kp_l, vp_l, mkp_l, mvp_l, cp_l, sp_l = [], [], [], [], [], []
    ks_l, vs_l, cs_l, ss_l = [], [], [], []
    for l in range(DEPTH):
        mix_w = (w_in[l], conv_w[l], conv_b[l], dt_bias[l], a_log[l], d_skip[l], ssm_norm_w[l],
                 lambda_params[l], attn_norm_w[l], w_out[l])
        conv0 = jnp.zeros((bp, CONV_WIDTH - 1, CONV_DIM), hp.dtype)
        h0 = jnp.zeros((bp, SSM_HEADS, SSM_HEAD_DIM, SSM_STATE), jnp.float32)
        m, k_new, v_new, c_new, s_new = _mixer(hp, pos_p, pos_p, None, None, conv0, h0, l, *mix_w)
        hp = _layer_norm(DEEPNORM_ALPHA * hp + m, ln_g[l, 0], ln_b[l, 0])
        mk, mv = _mem_kv(mem_prompt, w_ckv[l])
        hp = _layer_norm(DEEPNORM_ALPHA * hp + _cross_attn(hp, mk, mv, w_cq[l], w_co[l]), ln_g[l, 1], ln_b[l, 1])
        hp = _layer_norm(DEEPNORM_ALPHA * hp + _channel(hp, l, w_ffn_gu, w_ffn_down, w_router, w_exp_gu, w_exp_down),
                         ln_g[l, 2], ln_b[l, 2])
        kp_l.append(k_new); vp_l.append(v_new); mkp_l.append(mk); mvp_l.append(mv)
        cp_l.append(c_new); sp_l.append(s_new.astype(x_prompt.dtype))
        past_k = cache_attn_k[l, page_table].reshape(bs, past_len, 2 * ATT_HEADS, ATT_QK_DIM)
        past_v = cache_attn_v[l, page_table].reshape(bs, past_len, ATT_HEADS, ATT_V_DIM)
        m, k_new, v_new, c_new, s_new = _mixer(hs, pos_s, kpos_s, past_k, past_v,
                                               state_conv[l], state_ssm[l], l, *mix_w)
        hs = _layer_norm(DEEPNORM_ALPHA * hs + m, ln_g[l, 0], ln_b[l, 0])
        hs = _layer_norm(DEEPNORM_ALPHA * hs + _cross_attn(hs, cache_mem_k[l], cache_mem_v[l], w_cq[l], w_co[l]),
                         ln_g[l, 1], ln_b[l, 1])
        hs = _layer_norm(DEEPNORM_ALPHA * hs + _channel(hs, l, w_ffn_gu, w_ffn_down, w_router, w_exp_gu, w_exp_down),
                         ln_g[l, 2], ln_b[l, 2])
        ks_l.append(k_new); vs_l.append(v_new); cs_l.append(c_new); ss_l.append(s_new.astype(state_ssm.dtype))
    return (hp, hs,
            jnp.stack(kp_l), jnp.stack(vp_l), jnp.stack(mkp_l), jnp.stack(mvp_l),
            jnp.stack(cp_l), jnp.stack(sp_l),
            jnp.stack(ks_l), jnp.stack(vs_l), jnp.stack(cs_l), jnp.stack(ss_l))
```

```python
import functools
import math

import jax
import jax.numpy as jnp
from jax import lax
from jax.experimental import pallas as pl
from jax.experimental.pallas import tpu as pltpu

F32 = jnp.float32
BF16 = jnp.bfloat16

ATT_HEADS = 4
ATT_V_DIM = 128
ATT_QK_DIM = 64
ATT_COLS = 512
SSM_INNER = 512
SSM_HEAD_DIM = 64
SSM_HEADS = 8
SSM_GROUPS = 2
SSM_STATE = 128
CONV_WIDTH = 4
CONV_DIM = 1024
SSD_CHUNK = 128
MEM_HEADS = 4
N_EXPERTS = 8
ROPE_THETA = 10000.0
LN_EPS = 1e-5
NORM_EPS = 1e-5
QK_SCALE = ATT_QK_DIM ** -0.5

VMEM_LIMIT_BYTES = 56 * 1024 * 1024
NT_DIMS = (((1,), (1,)), ((), ()))


def _cparams(*sem):
    return pltpu.CompilerParams(dimension_semantics=sem, vmem_limit_bytes=VMEM_LIMIT_BYTES)


def _silu(x):
    return x / (1.0 + jnp.exp(-x))


def _softplus(x):
    return jnp.maximum(x, 0.0) + jnp.log1p(jnp.exp(-jnp.abs(x)))


def _layer_norm(y, g, b):
    mu = jnp.mean(y, axis=-1, keepdims=True)
    yc = y - mu
    var = jnp.mean(yc * yc, axis=-1, keepdims=True)
    return yc * lax.rsqrt(var + LN_EPS) * g + b


def _full(shape):
    return pl.BlockSpec(shape, lambda *_: (0,) * len(shape))


def _in_proj_kernel(x_ref, cos_ref, sin_ref, wq_ref, wk_ref, wv_ref, wz_ref, wxbc_ref, wdt_ref, wdtT_ref,
                    q_ref, kf_ref, kb_ref, vf_ref, vb_ref, z_ref, xbc_ref, dt_ref, dtT_ref):
    tm = x_ref.shape[0]
    xb = x_ref[...].astype(BF16)
    cos = jnp.concatenate([cos_ref[...]] * 4, axis=1)
    sin = jnp.concatenate([sin_ref[...]] * 4, axis=1)
    lane = lax.broadcasted_iota(jnp.int32, (tm, ATT_COLS), 1)
    first_half = (lane & (ATT_QK_DIM - 1)) < (ATT_QK_DIM // 2)

    def rope(t):
        partner = jnp.where(first_half, pltpu.roll(t, ATT_COLS - ATT_QK_DIM // 2, 1),
                            pltpu.roll(t, ATT_QK_DIM // 2, 1))
        return t * cos + partner * sin

    q = rope(jnp.dot(xb, wq_ref[...], preferred_element_type=F32))
    q_ref[...] = (q * QK_SCALE).astype(BF16)
    k = rope(jnp.dot(xb, wk_ref[...], preferred_element_type=F32))
    kf_ref[...] = k
    kb_ref[...] = k.astype(BF16)
    v = jnp.dot(xb, wv_ref[...], preferred_element_type=F32)
    vf_ref[...] = v
    vb_ref[...] = v.astype(BF16)
    z_ref[...] = jnp.dot(xb, wz_ref[...], preferred_element_type=F32)
    xbc_ref[...] = jnp.dot(xb, wxbc_ref[...], preferred_element_type=F32)
    dt_ref[...] = jnp.dot(xb, wdt_ref[...], preferred_element_type=F32)
    dtT_ref[...] = lax.dot_general(wdtT_ref[...], xb, NT_DIMS, preferred_element_type=F32)


def _in_proj(x, cos_t, sin_t, w, tm, n_pos_blocks):
    T, D = x.shape
    wq, wk, wv, wz, wxbc, wdt, wdtT = w
    row = lambda n: pl.BlockSpec((tm, n), lambda i: (i, 0))
    pos = pl.BlockSpec((tm, 128), lambda i: (i % n_pos_blocks, 0))
    out_shape = (
        jax.ShapeDtypeStruct((T, ATT_COLS), BF16),
        jax.ShapeDtypeStruct((T, ATT_COLS), F32),
        jax.ShapeDtypeStruct((T, ATT_COLS), BF16),
        jax.ShapeDtypeStruct((T, ATT_COLS), F32),
        jax.ShapeDtypeStruct((T, ATT_COLS), BF16),
        jax.ShapeDtypeStruct((T, SSM_INNER), F32),
        jax.ShapeDtypeStruct((T, CONV_DIM), F32),
        jax.ShapeDtypeStruct((T, SSM_HEADS), F32),
        jax.ShapeDtypeStruct((SSM_HEADS, T), F32),
    )
    out_specs = (row(ATT_COLS), row(ATT_COLS), row(ATT_COLS), row(ATT_COLS), row(ATT_COLS),
                 row(SSM_INNER), row(CONV_DIM), row(SSM_HEADS),
                 pl.BlockSpec((SSM_HEADS, tm), lambda i: (0, i)))
    return pl.pallas_call(
        _in_proj_kernel,
        grid=(T // tm,),
        in_specs=[row(D), pos, pos, _full(wq.shape), _full(wk.shape), _full(wv.shape), _full(wz.shape),
                  _full(wxbc.shape), _full(wdt.shape), _full(wdtT.shape)],
        out_specs=out_specs,
        out_shape=out_shape,
        compiler_params=_cparams("parallel"),
        name="in_proj",
    )(x, cos_t, sin_t, wq, wk, wv, wz, wxbc, wdt, wdtT)


def _gated_group_norm(y, z, norm_w):
    y = y * _silu(z)
    half = SSM_INNER // SSM_GROUPS
    y2 = y * y
    ms0 = jnp.mean(y2[:, :half], axis=-1, keepdims=True)
    ms1 = jnp.mean(y2[:, half:], axis=-1, keepdims=True)
    lane = lax.broadcasted_iota(jnp.int32, y.shape, 1)
    scale = jnp.where(lane < half, lax.rsqrt(ms0 + NORM_EPS), lax.rsqrt(ms1 + NORM_EPS))
    return y * scale * norm_w


def _ssd_kernel(xbc_ref, z_ref, dt_ref, dtT_ref, convw_ref, convb_ref, dtb_ref, dtbT_ref, alog_ref, alogT_ref,
                dskip_ref, normw_ref, y_ref, conv_out_ref, state_out_ref, ext_ref, h_ref):
    Q = SSD_CHUNK
    c = pl.program_id(1)
    last = pl.num_programs(1) - 1

    @pl.when(c == 0)
    def _():
        ext_ref[0:8, :] = jnp.zeros((8, CONV_DIM), F32)
        h_ref[...] = jnp.zeros(h_ref.shape, F32)

    ext_ref[8:8 + Q, :] = xbc_ref[...]
    conv = convb_ref[...]
    for i in range(CONV_WIDTH):
        conv = conv + ext_ref[5 + i:5 + i + Q, :] * convw_ref[i:i + 1, :]
    ext_ref[0:8, :] = xbc_ref[Q - 8:Q, :]
    conv = _silu(conv)
    xs = conv[:, :SSM_INNER]
    bmat = conv[:, SSM_INNER:SSM_INNER + SSM_GROUPS * SSM_STATE]
    cmat = conv[:, SSM_INNER + SSM_GROUPS * SSM_STATE:]

    dt_col = _softplus(dt_ref[...] + dtb_ref[...])
    a_col = dt_col * (-jnp.exp(alog_ref[...]))
    a_row = _softplus(dtT_ref[...] + dtbT_ref[...]) * (-jnp.exp(alogT_ref[...]))
    ri = lax.broadcasted_iota(jnp.int32, (Q, Q), 0)
    ci = lax.broadcasted_iota(jnp.int32, (Q, Q), 1)
    causal = ci <= ri
    tril = causal.astype(F32)
    triu = (ri <= ci).astype(F32)
    acs_col = jnp.dot(tril, a_col, preferred_element_type=F32, precision=lax.Precision.HIGHEST)
    acs_row = jnp.dot(a_row, triu, preferred_element_type=F32, precision=lax.Precision.HIGHEST)

    lo = ci < SSM_HEAD_DIM
    top = ri < SSM_HEAD_DIM
    pairs = []
    for k in range(SSM_HEADS // 2):
        h0, h1 = 2 * k, 2 * k + 1
        g = k // (SSM_HEADS // 2 // SSM_GROUPS)
        b_g = bmat[:, g * SSM_STATE:(g + 1) * SSM_STATE].astype(BF16)
        c_g = cmat[:, g * SSM_STATE:(g + 1) * SSM_STATE].astype(BF16)
        cb = lax.dot_general(c_g, b_g, NT_DIMS, preferred_element_type=F32)
        xs_pair = xs[:, 2 * SSM_HEAD_DIM * k:2 * SSM_HEAD_DIM * (k + 1)]
        dtc = jnp.where(lo, dt_col[:, h0:h0 + 1], dt_col[:, h1:h1 + 1])
        xdt = xs_pair * dtc
        acol = jnp.where(lo, acs_col[:, h0:h0 + 1], acs_col[:, h1:h1 + 1])
        alast0 = acs_row[h0:h0 + 1, Q - 1:Q]
        alast1 = acs_row[h1:h1 + 1, Q - 1:Q]
        alast = jnp.where(lo[0:1, :], alast0, alast1)
        xw = xdt * jnp.exp(alast - acol)
        y_pair = jnp.where(lo, dskip_ref[0:1, h0:h0 + 1], dskip_ref[0:1, h1:h1 + 1]) * xs_pair
        for h, sel in ((h0, lo), (h1, jnp.logical_not(lo))):
            seg = acs_col[:, h:h + 1] - acs_row[h:h + 1, :]
            decay = jnp.exp(jnp.where(causal, seg, -jnp.inf))
            m = (cb * decay).astype(BF16)
            y_pair = y_pair + jnp.dot(m, jnp.where(sel, xdt, 0.0).astype(BF16), preferred_element_type=F32)
        states = jnp.dot(xw.T.astype(BF16), b_g, preferred_element_type=F32)
        h_prev = h_ref[k]
        y_off = lax.dot_general(c_g, h_prev.astype(BF16), NT_DIMS, preferred_element_type=F32)
        y_pair = y_pair + y_off * jnp.exp(acol)
        dec = jnp.where(top[:, 0:1], jnp.exp(alast0), jnp.exp(alast1))
        h_ref[k] = h_prev * dec + states
        pairs.append(y_pair)
    y = jnp.concatenate(pairs, axis=1)
    y_ref[...] = _gated_group_norm(y, z_ref[...], normw_ref[...]).astype(BF16)

    @pl.when(c == last)
    def _():
        conv_out_ref[0] = xbc_ref[Q - (CONV_WIDTH - 1):Q, :]
        state_out_ref[0] = h_ref[...]


def _ssd(xbc, z, dt, dtT, sw, B, S):
    convw, convb, dtb, dtbT, alog, alogT, dskip, normw = sw
    nc = S // SSD_CHUNK
    Q = SSD_CHUNK
    rowb = lambda n: pl.BlockSpec((Q, n), lambda b, c: (b * nc + c, 0))
    npair = SSM_HEADS // 2
    return pl.pallas_call(
        _ssd_kernel,
        grid=(B, nc),
        in_specs=[rowb(CONV_DIM), rowb(SSM_INNER), rowb(SSM_HEADS),
                  pl.BlockSpec((SSM_HEADS, Q), lambda b, c: (0, b * nc + c)),
                  _full(convw.shape), _full(convb.shape), _full(dtb.shape), _full(dtbT.shape),
                  _full(alog.shape), _full(alogT.shape), _full(dskip.shape), _full(normw.shape)],
        out_specs=(rowb(SSM_INNER),
                   pl.BlockSpec((1, CONV_WIDTH - 1, CONV_DIM), lambda b, c: (b, 0, 0)),
                   pl.BlockSpec((1, npair, 2 * SSM_HEAD_DIM, SSM_STATE), lambda b, c: (b, 0, 0, 0))),
        out_shape=(jax.ShapeDtypeStruct((B * S, SSM_INNER), BF16),
                   jax.ShapeDtypeStruct((B, CONV_WIDTH - 1, CONV_DIM), F32),
                   jax.ShapeDtypeStruct((B, npair, 2 * SSM_HEAD_DIM, SSM_STATE), F32)),
        scratch_shapes=[pltpu.VMEM((Q + 8, CONV_DIM), F32),
                        pltpu.VMEM((npair, 2 * SSM_HEAD_DIM, SSM_STATE), F32)],
        compiler_params=_cparams("parallel", "arbitrary"),
        name="ssd_scan",
    )(xbc, z, dt, dtT, convw, convb, dtb, dtbT, alog, alogT, dskip, normw)


def _ssd_step_kernel(xbc_ref, z_ref, dt_ref, cstate_ref, hstate_ref, convw_ref, convb_ref, dtb_ref, alog_ref,
                     dskip_ref, normw_ref, y_ref, cout_ref, hout_ref):
    bt = xbc_ref.shape[0]
    P, N = SSM_HEAD_DIM, SSM_STATE
    new = xbc_ref[...]
    conv = convb_ref[...] + new * convw_ref[CONV_WIDTH - 1:CONV_WIDTH, :]
    for i in range(CONV_WIDTH - 1):
        conv = conv + cstate_ref[:, i * CONV_DIM:(i + 1) * CONV_DIM] * convw_ref[i:i + 1, :]
    cout_ref[:, 0:(CONV_WIDTH - 2) * CONV_DIM] = cstate_ref[:, CONV_DIM:(CONV_WIDTH - 1) * CONV_DIM]
    cout_ref[:, (CONV_WIDTH - 2) * CONV_DIM:] = new
    conv = _silu(conv)
    xs = conv[:, :SSM_INNER]
    bmat = conv[:, SSM_INNER:SSM_INNER + SSM_GROUPS * N]
    cmat = conv[:, SSM_INNER + SSM_GROUPS * N:]
    dt = _softplus(dt_ref[...] + dtb_ref[...])
    dec = jnp.exp(dt * (-jnp.exp(alog_ref[...])))
    eye = lax.broadcasted_iota(jnp.int32, (P, P), 0) == lax.broadcasted_iota(jnp.int32, (P, P), 1)
    rows = []
    for b in range(bt):
        pieces = []
        for h in range(SSM_HEADS):
            g = h // (SSM_HEADS // SSM_GROUPS)
            x_row = xs[b:b + 1, h * P:(h + 1) * P]
            x_col = jnp.sum(jnp.where(eye, x_row, 0.0), axis=1, keepdims=True)
            b_row = bmat[b:b + 1, g * N:(g + 1) * N]
            c_row = cmat[b:b + 1, g * N:(g + 1) * N]
            h_new = hstate_ref[b, h] * dec[b:b + 1, h:h + 1] + (x_col * dt[b:b + 1, h:h + 1]) * b_row
            hout_ref[b, h] = h_new
            y_col = jnp.sum(h_new * c_row, axis=1, keepdims=True) + dskip_ref[0:1, h:h + 1] * x_col
            pieces.append(jnp.sum(jnp.where(eye, y_col, 0.0), axis=0, keepdims=True))
        rows.append(jnp.concatenate(pieces, axis=1))
    y = jnp.concatenate(rows, axis=0)
    y_ref[...] = _gated_group_norm(y, z_ref[...], normw_ref[...]).astype(BF16)


def _ssd_step(xbc, z, dt, cstate, hstate, sw, bt=8):
    convw, convb, dtb, _, alog, _, dskip, normw = sw
    Bs = xbc.shape[0]
    cw = (CONV_WIDTH - 1) * CONV_DIM
    rowb = lambda n: pl.BlockSpec((bt, n), lambda i: (i, 0))
    hspec = pl.BlockSpec((bt, SSM_HEADS, SSM_HEAD_DIM, SSM_STATE), lambda i: (i, 0, 0, 0))
    return pl.pallas_call(
        _ssd_step_kernel,
        grid=(Bs // bt,),
        in_specs=[rowb(CONV_DIM), rowb(SSM_INNER), rowb(SSM_HEADS), rowb(cw), hspec,
                  _full(convw.shape), _full(convb.shape), _full(dtb.shape), _full(alog.shape),
                  _full(dskip.shape), _full(normw.shape)],
        out_specs=(rowb(SSM_INNER), rowb(cw), hspec),
        out_shape=(jax.ShapeDtypeStruct((Bs, SSM_INNER), BF16),
                   jax.ShapeDtypeStruct((Bs, cw), F32),
                   jax.ShapeDtypeStruct(hstate.shape, F32)),
        compiler_params=_cparams("parallel"),
        name="ssd_step",
    )(xbc, z, dt, cstate, hstate, convw, convb, dtb, alog, dskip, normw)


def _lambda_value(lp, lam_init):
    d1 = jnp.sum(lp[0:1, :] * lp[1:2, :], axis=1, keepdims=True)
    d2 = jnp.sum(lp[2:3, :] * lp[3:4, :], axis=1, keepdims=True)
    return jnp.exp(d1) - jnp.exp(d2) + lam_init


def _head_norm(o, nw, lam_init):
    return o * lax.rsqrt(jnp.mean(o * o, axis=-1, keepdims=True) + NORM_EPS) * nw * (1.0 - lam_init)


def _attn_kernel(lp_ref, nw_ref, q_ref, k_ref, v_ref, o_ref, m1_ref, l1_ref, acc1_ref, m2_ref, l2_ref, acc2_ref,
                 *, lam_init):
    tq, tk = q_ref.shape[0], k_ref.shape[0]
    qi = pl.program_id(2)
    kj = pl.program_id(3)

    @pl.when(kj == 0)
    def _():
        for m_ref, l_ref, acc_ref in ((m1_ref, l1_ref, acc1_ref), (m2_ref, l2_ref, acc2_ref)):
            m_ref[...] = jnp.full(m_ref.shape, -jnp.inf, F32)
            l_ref[...] = jnp.zeros(l_ref.shape, F32)
            acc_ref[...] = jnp.zeros(acc_ref.shape, F32)

    @pl.when(kj <= qi)
    def _():
        q = q_ref[...]
        k = k_ref[...]
        v = v_ref[...]
        lane = lax.broadcasted_iota(jnp.int32, q.shape, 1)
        rows = qi * tq + lax.broadcasted_iota(jnp.int32, (tq, tk), 0)
        cols = kj * tk + lax.broadcasted_iota(jnp.int32, (tq, tk), 1)
        visible = cols <= rows
        zero = jnp.zeros_like(q)
        for qm, m_ref, l_ref, acc_ref in ((jnp.where(lane < ATT_QK_DIM, q, zero), m1_ref, l1_ref, acc1_ref),
                                          (jnp.where(lane >= ATT_QK_DIM, q, zero), m2_ref, l2_ref, acc2_ref)):
            s = lax.dot_general(qm, k, NT_DIMS, preferred_element_type=F32)
            s = jnp.where(visible, s, -jnp.inf)
            m_old = m_ref[...]
            m_new = jnp.maximum(m_old, jnp.max(s, axis=1, keepdims=True))
            alpha = jnp.exp(m_old - m_new)
            p = jnp.exp(s - m_new)
            l_ref[...] = alpha * l_ref[...] + jnp.sum(p, axis=1, keepdims=True)
            acc_ref[...] = alpha * acc_ref[...] + jnp.dot(p.astype(BF16), v, preferred_element_type=F32)
            m_ref[...] = m_new

    @pl.when(kj == qi)
    def _():
        lam = _lambda_value(lp_ref[...], lam_init)
        o = acc1_ref[...] / l1_ref[...] - lam * (acc2_ref[...] / l2_ref[...])
        o_ref[...] = _head_norm(o, nw_ref[...], lam_init).astype(BF16)


def _attention(q, k, v, lp, nw, lam_init, B, S):
    tq = min(512, S)
    nq = S // tq
    qspec = pl.BlockSpec((tq, ATT_V_DIM), lambda b, h, i, j: (b * nq + i, h))
    kspec = pl.BlockSpec((tq, ATT_V_DIM), lambda b, h, i, j: (b * nq + jnp.minimum(i, j), h))
    return pl.pallas_call(
        functools.partial(_attn_kernel, lam_init=lam_init),
        grid=(B, ATT_HEADS, nq, nq),
        in_specs=[_full(lp.shape), _full(nw.shape), qspec, kspec, kspec],
        out_specs=qspec,
        out_shape=jax.ShapeDtypeStruct((B * S, ATT_COLS), BF16),
        scratch_shapes=[pltpu.VMEM((tq, 1), F32), pltpu.VMEM((tq, 1), F32), pltpu.VMEM((tq, ATT_V_DIM), F32),
                        pltpu.VMEM((tq, 1), F32), pltpu.VMEM((tq, 1), F32), pltpu.VMEM((tq, ATT_V_DIM), F32)],
        compiler_params=_cparams("parallel", "parallel", "parallel", "arbitrary"),
        name="diff_attention",
    )(lp, nw, q, k, v)


def _decode_attn_kernel(pt_ref, lp_ref, nw_ref, q_ref, kn_ref, vn_ref, *refs, n_pages, lam_init):
    k_refs = refs[:n_pages]
    v_refs = refs[n_pages:2 * n_pages]
    o_ref = refs[2 * n_pages]
    R = 2 * ATT_HEADS
    r = lax.broadcasted_iota(jnp.int32, (R, ATT_COLS), 0)
    grp = lax.broadcasted_iota(jnp.int32, (R, ATT_COLS), 1) // ATT_QK_DIM
    target = jnp.where(r < ATT_HEADS, 2 * r, 2 * (r - ATT_HEADS) + 1)
    qf = jnp.where(grp == target, q_ref[0].astype(F32), 0.0)
    qt = qf.astype(BF16)
    s = jnp.concatenate(
        [lax.dot_general(qt, k_ref[0].astype(BF16), NT_DIMS, preferred_element_type=F32) for k_ref in k_refs],
        axis=1)
    kn = kn_ref[0].astype(BF16).astype(F32)
    s_new = jnp.sum(qf * kn, axis=1, keepdims=True)
    m = jnp.maximum(jnp.max(s, axis=1, keepdims=True), s_new)
    p = jnp.exp(s - m)
    p_new = jnp.exp(s_new - m)
    inv_l = 1.0 / (jnp.sum(p, axis=1, keepdims=True) + p_new)
    lam = _lambda_value(lp_ref[...], lam_init)
    pn = p * inv_l
    pn_new = p_new * inv_l
    a = (pn - lam * pltpu.roll(pn, ATT_HEADS, 0)).astype(BF16)
    pn_new = jnp.broadcast_to(pn_new, (R, 128))
    a_new = (pn_new - lam * pltpu.roll(pn_new, ATT_HEADS, 0))[:, 0:1].astype(BF16).astype(F32)
    page = k_refs[0].shape[1]
    acc = a_new * vn_ref[0].astype(BF16).astype(F32)
    for j, v_ref in enumerate(v_refs):
        acc = acc + jnp.dot(a[:, j * page:(j + 1) * page], v_ref[0].astype(BF16), preferred_element_type=F32)
    outs = [_head_norm(acc[h:h + 1, h * ATT_V_DIM:(h + 1) * ATT_V_DIM], nw_ref[...], lam_init)
            for h in range(ATT_HEADS)]
    o_ref[0] = jnp.concatenate(outs, axis=1).astype(BF16)


def _decode_attention(pt, q, k_new, v_new, cache_k, cache_v, lp, nw, lam_init):
    Bs, n_pages = pt.shape
    page = cache_k.shape[1]
    tok = pl.BlockSpec((1, 1, ATT_COLS), lambda b, pt: (b, 0, 0))
    pspecs = [pl.BlockSpec((1, page, ATT_COLS), lambda b, pt, j=j: (pt[b, j], 0, 0)) for j in range(n_pages)]
    grid_spec = pltpu.PrefetchScalarGridSpec(
        num_scalar_prefetch=1,
        grid=(Bs,),
        in_specs=[pl.BlockSpec(lp.shape, lambda b, pt: (0, 0)), pl.BlockSpec(nw.shape, lambda b, pt: (0, 0)),
                  tok, tok, tok] + pspecs + pspecs,
        out_specs=tok,
    )
    return pl.pallas_call(
        functools.partial(_decode_attn_kernel, n_pages=n_pages, lam_init=lam_init),
        grid_spec=grid_spec,
        out_shape=jax.ShapeDtypeStruct((Bs, 1, ATT_COLS), BF16),
        compiler_params=_cparams("parallel"),
        name="decode_attention",
    )(pt, lp, nw, q, k_new, v_new, *([cache_k] * n_pages), *([cache_v] * n_pages))


def _proj_ln_kernel(*refs, n_in, alpha):
    a_refs = refs[:n_in]
    w_refs = refs[n_in:2 * n_in]
    x_ref, g_ref, b_ref, o_ref = refs[2 * n_in:]
    acc = alpha * x_ref[...]
    for a_ref, w_ref in zip(a_refs, w_refs):
        acc = acc + jnp.dot(a_ref[...].astype(BF16), w_ref[...], preferred_element_type=F32)
    o_ref[...] = _layer_norm(acc, g_ref[...], b_ref[...])


def _proj_ln(acts, ws, x, g, b, alpha, tm):
    T, D = x.shape
    row = lambda n: pl.BlockSpec((tm, n), lambda i: (i, 0))
    return pl.pallas_call(
        functools.partial(_proj_ln_kernel, n_in=len(acts), alpha=alpha),
        grid=(T // tm,),
        in_specs=[row(a.shape[1]) for a in acts] + [_full(w.shape) for w in ws] + [row(D), _full(g.shape), _full(b.shape)],
        out_specs=row(D),
        out_shape=jax.ShapeDtypeStruct((T, D), F32),
        compiler_params=_cparams("parallel"),
        name="proj_ln",
    )(*acts, *ws, x, g, b)


def _matmul_kernel(x_ref, w_ref, *o_refs):
    y = jnp.dot(x_ref[...].astype(BF16), w_ref[...], preferred_element_type=F32)
    for o_ref in o_refs:
        o_ref[...] = y.astype(o_ref.dtype)


def _matmul(x, w, out_dtypes, tm, tn):
    M, K = x.shape
    N = w.shape[1]
    ospec = pl.BlockSpec((tm, tn), lambda i, j: (i, j))
    return pl.pallas_call(
        _matmul_kernel,
        grid=(M // tm, N // tn),
        in_specs=[pl.BlockSpec((tm, K), lambda i, j: (i, 0)), pl.BlockSpec((K, tn), lambda i, j: (0, j))],
        out_specs=tuple(ospec for _ in out_dtypes),
        out_shape=tuple(jax.ShapeDtypeStruct((M, N), d) for d in out_dtypes),
        compiler_params=_cparams("parallel", "parallel"),
        name="matmul",
    )(x, w)


def _softmax_rows(s):
    m = jnp.max(s, axis=1, keepdims=True)
    p = jnp.exp(s - m)
    return p / jnp.sum(p, axis=1, keepdims=True)


def _cross_kernel(x_ref, wq_ref, mk_ref, mv_ref, wo_ref, g_ref, b_ref, o_ref, *, alpha):
    x = x_ref[...]
    D = x.shape[1]
    dh = D // MEM_HEADS
    q = jnp.dot(x.astype(BF16), wq_ref[...], preferred_element_type=F32).astype(BF16)
    outs = []
    for h in range(MEM_HEADS):
        sl = slice(h * dh, (h + 1) * dh)
        s = lax.dot_general(q[:, sl], mk_ref[:, sl], NT_DIMS, preferred_element_type=F32) * (dh ** -0.5)
        outs.append(jnp.dot(_softmax_rows(s).astype(BF16), mv_ref[:, sl], preferred_element_type=F32).astype(BF16))
    o = jnp.concatenate(outs, axis=1)
    y = alpha * x + jnp.dot(o, wo_ref[...], preferred_element_type=F32)
    o_ref[...] = _layer_norm(y, g_ref[...], b_ref[...])


def _cross_attention(x, wq, mk, mv, wo, g, b, alpha, B, S, n_mem, tm):
    T, D = x.shape
    nt = S // tm
    row = pl.BlockSpec((tm, D), lambda bi, i: (bi * nt + i, 0))
    mem = pl.BlockSpec((n_mem, D), lambda bi, i: (bi, 0))
    return pl.pallas_call(
        functools.partial(_cross_kernel, alpha=alpha),
        grid=(B, nt),
        in_specs=[row, _full(wq.shape), mem, mem, _full(wo.shape), _full(g.shape), _full(b.shape)],
        out_specs=row,
        out_shape=jax.ShapeDtypeStruct((T, D), F32),
        compiler_params=_cparams("parallel", "parallel"),
        name="cross_attention",
    )(x, wq, mk, mv, wo, g, b)


def _cross_decode_kernel(q_ref, mk_ref, mv_ref, o_ref):
    D = q_ref.shape[2]
    dh = D // MEM_HEADS
    q = jnp.broadcast_to(q_ref[0], (8, D))
    outs = []
    for h in range(MEM_HEADS):
        sl = slice(h * dh, (h + 1) * dh)
        s = lax.dot_general(q[:, sl], mk_ref[0, :, sl].astype(BF16), NT_DIMS, preferred_element_type=F32) * (dh ** -0.5)
        o = jnp.dot(_softmax_rows(s).astype(BF16), mv_ref[0, :, sl].astype(BF16), preferred_element_type=F32)
        outs.append(o[0:1, :])
    o_ref[0] = jnp.concatenate(outs, axis=1).astype(BF16)


def _cross_decode(q, mem_k, mem_v, first):
    Bs, _, D = q.shape
    n_mem = mem_k.shape[1]
    tok = pl.BlockSpec((1, 1, D), lambda b: (b, 0, 0))
    mem = pl.BlockSpec((1, n_mem, D), lambda b: (first + b, 0, 0))
    return pl.pallas_call(
        _cross_decode_kernel,
        grid=(Bs,),
        in_specs=[tok, mem, mem],
        out_specs=tok,
        out_shape=jax.ShapeDtypeStruct((Bs, 1, D), BF16),
        compiler_params=_cparams("parallel"),
        name="cross_decode",
    )(q, mem_k, mem_v)


def _ffn_kernel(x_ref, wg_ref, wu_ref, wd_ref, g_ref, b_ref, o_ref, acc_ref, *, alpha):
    f = pl.program_id(1)
    x = x_ref[...]
    xb = x.astype(BF16)

    @pl.when(f == 0)
    def _():
        acc_ref[...] = alpha * x

    gate = jnp.dot(xb, wg_ref[...], preferred_element_type=F32)
    up = jnp.dot(xb, wu_ref[...], preferred_element_type=F32)
    acc_ref[...] += jnp.dot((_silu(gate) * up).astype(BF16), wd_ref[...], preferred_element_type=F32)

    @pl.when(f == pl.num_programs(1) - 1)
    def _():
        o_ref[...] = _layer_norm(acc_ref[...], g_ref[...], b_ref[...])


def _ffn(x, w_gu, w_down, g, b, alpha, tm, tf):
    T, D = x.shape
    F = w_down.shape[0]
    nf = F // tf
    row = pl.BlockSpec((tm, D), lambda i, f: (i, 0))
    return pl.pallas_call(
        functools.partial(_ffn_kernel, alpha=alpha),
        grid=(T // tm, nf),
        in_specs=[row, pl.BlockSpec((D, tf), lambda i, f: (0, f)), pl.BlockSpec((D, tf), lambda i, f: (0, nf + f)),
                  pl.BlockSpec((tf, D), lambda i, f: (f, 0)), _full(g.shape), _full(b.shape)],
        out_specs=row,
        out_shape=jax.ShapeDtypeStruct((T, D), F32),
        scratch_shapes=[pltpu.VMEM((tm, D), F32)],
        compiler_params=_cparams("parallel", "arbitrary"),
        name="ffn",
    )(x, w_gu, w_gu, w_down, g, b)


def _router_kernel(x_ref, wr_ref, gate_ref):
    logits = jnp.dot(x_ref[...].astype(BF16), wr_ref[...], preferred_element_type=F32)
    lane = lax.broadcasted_iota(jnp.int32, logits.shape, 1).astype(F32)
    none = float(N_EXPERTS)
    m1 = jnp.max(logits, axis=1, keepdims=True)
    i1 = jnp.min(jnp.where(logits == m1, lane, none), axis=1, keepdims=True)
    rest = jnp.where(lane == i1, -jnp.inf, logits)
    m2 = jnp.max(rest, axis=1, keepdims=True)
    i2 = jnp.min(jnp.where(rest == m2, lane, none), axis=1, keepdims=True)
    e = jnp.exp(m2 - m1)
    den = 1.0 + e
    gate_ref[...] = jnp.where(lane == i1, 1.0 / den, 0.0) + jnp.where(lane == i2, e / den, 0.0)


def _router(x, wr, tm):
    T, D = x.shape
    return pl.pallas_call(
        _router_kernel,
        grid=(T // tm,),
        in_specs=[pl.BlockSpec((tm, D), lambda i: (i, 0)), _full(wr.shape)],
        out_specs=pl.BlockSpec((tm, N_EXPERTS), lambda i: (i, 0)),
        out_shape=jax.ShapeDtypeStruct((T, N_EXPERTS), F32),
        compiler_params=_cparams("parallel"),
        name="router",
    )(x, wr)


def _moe_kernel(x_ref, gate_ref, wg_ref, wu_ref, wd_ref, g_ref, b_ref, o_ref, acc_ref, *, alpha):
    e = pl.program_id(1)
    f = pl.program_id(2)
    x = x_ref[...]
    xb = x.astype(BF16)

    @pl.when(jnp.logical_and(e == 0, f == 0))
    def _():
        acc_ref[...] = alpha * x

    gates = gate_ref[...]
    lane = lax.broadcasted_iota(jnp.int32, gates.shape, 1)
    ge = jnp.sum(jnp.where(lane == e, gates, 0.0), axis=1, keepdims=True)
    gate = jnp.dot(xb, wg_ref[0], preferred_element_type=F32)
    up = jnp.dot(xb, wu_ref[0], preferred_element_type=F32)
    acc_ref[...] += jnp.dot((_silu(gate) * up).astype(BF16), wd_ref[0], preferred_element_type=F32) * ge

    @pl.when(jnp.logical_and(e == pl.num_programs(1) - 1, f == pl.num_programs(2) - 1))
    def _():
        o_ref[...] = _layer_norm(acc_ref[...], g_ref[...], b_ref[...])


def _moe(x, gates, w_gu, w_down, g, b, alpha, tm, tf):
    T, D = x.shape
    E, F, _ = w_down.shape
    nf = F // tf
    row = pl.BlockSpec((tm, D), lambda i, e, f: (i, 0))
    return pl.pallas_call(
        functools.partial(_moe_kernel, alpha=alpha),
        grid=(T // tm, E, nf),
        in_specs=[row, pl.BlockSpec((tm, E), lambda i, e, f: (i, 0)),
                  pl.BlockSpec((1, D, tf), lambda i, e, f: (e, 0, f)),
                  pl.BlockSpec((1, D, tf), lambda i, e, f: (e, 0, nf + f)),
                  pl.BlockSpec((1, tf, D), lambda i, e, f: (e, f, 0)), _full(g.shape), _full(b.shape)],
        out_specs=row,
        out_shape=jax.ShapeDtypeStruct((T, D), F32),
        scratch_shapes=[pltpu.VMEM((tm, D), F32)],
        compiler_params=_cparams("parallel", "arbitrary", "arbitrary"),
        name="moe",
    )(x, gates, w_gu, w_gu, w_down, g, b)


def _rope_tables(pos):
    half = ATT_QK_DIM // 2
    inv = ROPE_THETA ** (-jnp.arange(0, ATT_QK_DIM, 2, dtype=F32) / ATT_QK_DIM)
    ang = pos.astype(F32)[:, None] * inv[None, :]
    cos, sin = jnp.cos(ang), jnp.sin(ang)
    return jnp.concatenate([cos] * 4, axis=1), jnp.concatenate([-sin, sin, -sin, sin], axis=1)


def _row_tile(n, target):
    t = min(n, target)
    while n % t:
        t //= 2
    return t


def kernel(x_prompt, x_sample, cache_attn_k, cache_attn_v, cache_mem_k, cache_mem_v, state_conv, state_ssm,
           page_table, mem_prompt, ln_g, ln_b, w_in, conv_w, conv_b, dt_bias, a_log, d_skip, ssm_norm_w,
           lambda_params, attn_norm_w, w_out, w_cq, w_ckv, w_co, w_ffn_gu, w_ffn_down, w_router, w_exp_gu,
           w_exp_down):
    B, S, D = x_prompt.shape
    Bs = x_sample.shape[0]
    depth = w_in.shape[0]
    n_phys, page = cache_attn_k.shape[1], cache_attn_k.shape[2]
    n_pages = page_table.shape[1]
    past_len = n_pages * page
    n_mem = mem_prompt.shape[1]
    alpha = (2 * depth) ** 0.25
    assert S % SSD_CHUNK == 0 and x_sample.shape[1] == 1

    tm_p = _row_tile(S, 512)
    tm_s = _row_tile(Bs, 128)
    cos_p, sin_p = _rope_tables(jnp.arange(S, dtype=jnp.int32))
    cos_s, sin_s = _rope_tables(jnp.full((tm_s,), past_len, jnp.int32))

    hp = x_prompt.reshape(B * S, D)
    hs = x_sample.reshape(Bs, D)
    mem = mem_prompt.reshape(B * n_mem, D)
    cache_k = cache_attn_k.reshape(depth * n_phys, page, ATT_COLS)
    cache_v = cache_attn_v.reshape(depth * n_phys, page, ATT_COLS)
    cmem_k = cache_mem_k.reshape(depth * Bs, n_mem, D)
    cmem_v = cache_mem_v.reshape(depth * Bs, n_mem, D)
    cstate = state_conv.reshape(depth, Bs, (CONV_WIDTH - 1) * CONV_DIM)

    splits = (ATT_COLS, 2 * ATT_COLS, 3 * ATT_COLS, 3 * ATT_COLS + SSM_INNER, 3 * ATT_COLS + SSM_INNER + CONV_DIM)
    outs = {n: [] for n in ("kp", "vp", "mkp", "mvp", "cp", "sp", "ks", "vs", "cs", "ss")}
    for l in range(depth):
        lam_init = 0.8 - 0.6 * math.exp(-0.3 * l)
        wl = w_in[l].astype(BF16)
        wq, wk, wv, wz, wxbc, wdt = (wl[:, a:b] for a, b in zip((0,) + splits, splits + (wl.shape[1],)))
        w_proj = (wq, wk, wv, wz, wxbc, wdt, wdt.T)
        sw = (conv_w[l], conv_b[l][None, :], dt_bias[l][None, :], dt_bias[l][:, None], a_log[l][None, :],
              a_log[l][:, None], d_skip[l][None, :], ssm_norm_w[l][None, :])
        lp = lambda_params[l]
        nw = attn_norm_w[l][None, :]
        wo = w_out[l].astype(BF16)
        wo_att, wo_ssm = wo[:ATT_COLS], wo[ATT_COLS:]
        wcq = w_cq[l].astype(BF16)
        wckv = w_ckv[l].astype(BF16)
        wco = w_co[l].astype(BF16)
        g0, g1, g2 = (ln_g[l, i][None, :] for i in range(3))
        b0, b1, b2 = (ln_b[l, i][None, :] for i in range(3))

        q, kf, kb, vf, vb, z, xbc, dt, dtT = _in_proj(hp, cos_p, sin_p, w_proj, tm_p, S // tm_p)
        att = _attention(q, kb, vb, lp, nw, lam_init, B, S)
        y, conv_new, h_last = _ssd(xbc, z, dt, dtT, sw, B, S)
        hp = _proj_ln((att, y), (wo_att, wo_ssm), hp, g0, b0, alpha, tm_p)
        outs["kp"].append(kf.reshape(B, S, 2 * ATT_HEADS, ATT_QK_DIM))
        outs["vp"].append(vf.reshape(B, S, ATT_HEADS, ATT_V_DIM))
        outs["cp"].append(conv_new)
        outs["sp"].append(h_last.reshape(B, SSM_HEADS, SSM_HEAD_DIM, SSM_STATE))
        mkv_f, mkv_b = _matmul(mem, wckv, (F32, BF16), _row_tile(B * n_mem, 512), D)
        outs["mkp"].append(mkv_f[:, :D].reshape(B, n_mem, MEM_HEADS, D // MEM_HEADS))
        outs["mvp"].append(mkv_f[:, D:].reshape(B, n_mem, MEM_HEADS, D // MEM_HEADS))
        hp = _cross_attention(hp, wcq, mkv_b[:, :D], mkv_b[:, D:], wco, g1, b1, alpha, B, S, n_mem, tm_p)

        q, kf, _, vf, _, z, xbc, dt, _ = _in_proj(hs, cos_s, sin_s, w_proj, tm_s, 1)
        att = _decode_attention(page_table + l * n_phys, q.reshape(Bs, 1, ATT_COLS), kf.reshape(Bs, 1, ATT_COLS),
                                vf.reshape(Bs, 1, ATT_COLS), cache_k, cache_v, lp, nw, lam_init)
        y, conv_new, h_new = _ssd_step(xbc, z, dt, cstate[l], state_ssm[l], sw)
        hs = _proj_ln((att.reshape(Bs, ATT_COLS), y), (wo_att, wo_ssm), hs, g0, b0, alpha, tm_s)
        outs["ks"].append(kf.reshape(Bs, 1, 2 * ATT_HEADS, ATT_QK_DIM))
        outs["vs"].append(vf.reshape(Bs, 1, ATT_HEADS, ATT_V_DIM))
        outs["cs"].append(conv_new.reshape(Bs, CONV_WIDTH - 1, CONV_DIM))
        outs["ss"].append(h_new)
        (qc,) = _matmul(hs, wcq, (BF16,), tm_s, D)
        oc = _cross_decode(qc.reshape(Bs, 1, D), cmem_k, cmem_v, l * Bs)
        hs = _proj_ln((oc.reshape(Bs, D),), (wco,), hs, g1, b1, alpha, tm_s)

        if l % 2 == 0:
            wgu = w_ffn_gu[l // 2].astype(BF16)
            wd = w_ffn_down[l // 2].astype(BF16)
            tf = wd.shape[0] // 2
            hp = _ffn(hp, wgu, wd, g2, b2, alpha, tm_p, tf)
            hs = _ffn(hs, wgu, wd, g2, b2, alpha, tm_s, tf)
        else:
            wr = w_router[l // 2].astype(BF16)
            wgu = w_exp_gu[l // 2].astype(BF16)
            wd = w_exp_down[l // 2].astype(BF16)
            tf = wd.shape[1] // 2
            hp = _moe(hp, _router(hp, wr, tm_p), wgu, wd, g2, b2, alpha, tm_p, tf)
            hs = _moe(hs, _router(hs, wr, tm_s), wgu, wd, g2, b2, alpha, tm_s, tf)

    st = {n: jnp.stack(v) for n, v in outs.items()}
    return (hp.reshape(B, S, D), hs.reshape(Bs, 1, D), st["kp"], st["vp"], st["mkp"], st["mvp"], st["cp"],
            st["sp"], st["ks"], st["vs"], st["cs"], st["ss"])
```

```python
import functools
import math

import jax
import jax.numpy as jnp
from jax import lax
from jax.experimental import pallas as pl
from jax.experimental.pallas import tpu as pltpu

F32 = jnp.float32
BF16 = jnp.bfloat16

ATT_HEADS = 4
ATT_V_DIM = 128
ATT_QK_DIM = 64
ATT_COLS = 512
SSM_INNER = 512
SSM_HEAD_DIM = 64
SSM_HEADS = 8
SSM_GROUPS = 2
SSM_STATE = 128
CONV_WIDTH = 4
CONV_DIM = 1024
SSD_CHUNK = 128
MEM_HEADS = 4
N_EXPERTS = 8
ROPE_THETA = 10000.0
LN_EPS = 1e-5
NORM_EPS = 1e-5
QK_SCALE = ATT_QK_DIM ** -0.5

VMEM_LIMIT_BYTES = 56 * 1024 * 1024
NT_DIMS = (((1,), (1,)), ((), ()))


def _cparams(*sem):
    return pltpu.CompilerParams(dimension_semantics=sem, vmem_limit_bytes=VMEM_LIMIT_BYTES)


def _silu(x):
    return x / (1.0 + jnp.exp(-x))


def _softplus(x):
    return jnp.maximum(x, 0.0) + jnp.log1p(jnp.exp(-jnp.abs(x)))


def _layer_norm(y, g, b):
    mu = jnp.mean(y, axis=-1, keepdims=True)
    yc = y - mu
    var = jnp.mean(yc * yc, axis=-1, keepdims=True)
    return yc * lax.rsqrt(var + LN_EPS) * g + b


def _full(shape):
    return pl.BlockSpec(shape, lambda *_: (0,) * len(shape))


def _in_proj_kernel(x_ref, cos_ref, sin_ref, wq_ref, wk_ref, wv_ref, wz_ref, wxbc_ref, wdt_ref, wdtT_ref,
                    q_ref, kf_ref, kb_ref, vf_ref, vb_ref, z_ref, xbc_ref, dt_ref, dtT_ref):
    tm = x_ref.shape[0]
    xb = x_ref[...].astype(BF16)
    cos = jnp.concatenate([cos_ref[...]] * 4, axis=1)
    sin = jnp.concatenate([sin_ref[...]] * 4, axis=1)
    lane = lax.broadcasted_iota(jnp.int32, (tm, ATT_COLS), 1)
    first_half = (lane & (ATT_QK_DIM - 1)) < (ATT_QK_DIM // 2)

    def rope(t):
        partner = jnp.where(first_half, pltpu.roll(t, ATT_COLS - ATT_QK_DIM // 2, 1),
                            pltpu.roll(t, ATT_QK_DIM // 2, 1))
        return t * cos + partner * sin

    q = rope(jnp.dot(xb, wq_ref[...], preferred_element_type=F32))
    q_ref[...] = (q * QK_SCALE).astype(BF16)
    k = rope(jnp.dot(xb, wk_ref[...], preferred_element_type=F32))
    kf_ref[...] = k
    kb_ref[...] = k.astype(BF16)
    v = jnp.dot(xb, wv_ref[...], preferred_element_type=F32)
    vf_ref[...] = v
    vb_ref[...] = v.astype(BF16)
    z_ref[...] = jnp.dot(xb, wz_ref[...], preferred_element_type=F32)
    xbc_ref[...] = jnp.dot(xb, wxbc_ref[...], preferred_element_type=F32)
    dt_ref[...] = jnp.dot(xb, wdt_ref[...], preferred_element_type=F32)
    dtT_ref[...] = lax.dot_general(wdtT_ref[...], xb, NT_DIMS, preferred_element_type=F32)


def _in_proj(x, cos_t, sin_t, w, tm, n_pos_blocks):
    T, D = x.shape
    wq, wk, wv, wz, wxbc, wdt, wdtT = w
    row = lambda n: pl.BlockSpec((tm, n), lambda i: (i, 0))
    pos = pl.BlockSpec((tm, 128), lambda i: (i % n_pos_blocks, 0))
    out_shape = (
        jax.ShapeDtypeStruct((T, ATT_COLS), BF16),
        jax.ShapeDtypeStruct((T, ATT_COLS), F32),
        jax.ShapeDtypeStruct((T, ATT_COLS), BF16),
        jax.ShapeDtypeStruct((T, ATT_COLS), F32),
        jax.ShapeDtypeStruct((T, ATT_COLS), BF16),
        jax.ShapeDtypeStruct((T, SSM_INNER), F32),
        jax.ShapeDtypeStruct((T, CONV_DIM), F32),
        jax.ShapeDtypeStruct((T, SSM_HEADS), F32),
        jax.ShapeDtypeStruct((SSM_HEADS, T), F32),
    )
    out_specs = (row(ATT_COLS), row(ATT_COLS), row(ATT_COLS), row(ATT_COLS), row(ATT_COLS),
                 row(SSM_INNER), row(CONV_DIM), row(SSM_HEADS),
                 pl.BlockSpec((SSM_HEADS, tm), lambda i: (0, i)))
    return pl.pallas_call(
        _in_proj_kernel,
        grid=(T // tm,),
        in_specs=[row(D), pos, pos, _full(wq.shape), _full(wk.shape), _full(wv.shape), _full(wz.shape),
                  _full(wxbc.shape), _full(wdt.shape), _full(wdtT.shape)],
        out_specs=out_specs,
        out_shape=out_shape,
        compiler_params=_cparams("parallel"),
        name="in_proj",
    )(x, cos_t, sin_t, wq, wk, wv, wz, wxbc, wdt, wdtT)


def _gated_group_norm(y, z, norm_w):
    y = y * _silu(z)
    half = SSM_INNER // SSM_GROUPS
    y2 = y * y
    ms0 = jnp.mean(y2[:, :half], axis=-1, keepdims=True)
    ms1 = jnp.mean(y2[:, half:], axis=-1, keepdims=True)
    lane = lax.broadcasted_iota(jnp.int32, y.shape, 1)
    scale = jnp.where(lane < half, lax.rsqrt(ms0 + NORM_EPS), lax.rsqrt(ms1 + NORM_EPS))
    return y * scale * norm_w


def _ssd_kernel(xbc_ref, z_ref, dt_ref, dtT_ref, convw_ref, convb_ref, dtb_ref, dtbT_ref, alog_ref, alogT_ref,
                dskip_ref, normw_ref, y_ref, conv_out_ref, state_out_ref, ext_ref, h_ref):
    Q = SSD_CHUNK
    c = pl.program_id(1)
    last = pl.num_programs(1) - 1

    @pl.when(c == 0)
    def _():
        ext_ref[0:8, :] = jnp.zeros((8, CONV_DIM), F32)
        h_ref[...] = jnp.zeros(h_ref.shape, F32)

    ext_ref[8:8 + Q, :] = xbc_ref[...]
    conv = convb_ref[...]
    for i in range(CONV_WIDTH):
        conv = conv + ext_ref[5 + i:5 + i + Q, :] * convw_ref[i:i + 1, :]
    ext_ref[0:8, :] = xbc_ref[Q - 8:Q, :]
    conv = _silu(conv)
    xs = conv[:, :SSM_INNER]
    bmat = conv[:, SSM_INNER:SSM_INNER + SSM_GROUPS * SSM_STATE]
    cmat = conv[:, SSM_INNER + SSM_GROUPS * SSM_STATE:]

    dt_col = _softplus(dt_ref[...] + dtb_ref[...])
    a_col = dt_col * (-jnp.exp(alog_ref[...]))
    a_row = _softplus(dtT_ref[...] + dtbT_ref[...]) * (-jnp.exp(alogT_ref[...]))
    ri = lax.broadcasted_iota(jnp.int32, (Q, Q), 0)
    ci = lax.broadcasted_iota(jnp.int32, (Q, Q), 1)
    causal = ci <= ri
    tril = causal.astype(F32)
    triu = (ri <= ci).astype(F32)
    acs_col = jnp.dot(tril, a_col, preferred_element_type=F32, precision=lax.Precision.HIGHEST)
    acs_row = jnp.dot(a_row, triu, preferred_element_type=F32, precision=lax.Precision.HIGHEST)

    lo = ci < SSM_HEAD_DIM
    top = ri < SSM_HEAD_DIM
    pairs = []
    for k in range(SSM_HEADS // 2):
        h0, h1 = 2 * k, 2 * k + 1
        g = k // (SSM_HEADS // 2 // SSM_GROUPS)
        b_g = bmat[:, g * SSM_STATE:(g + 1) * SSM_STATE].astype(BF16)
        c_g = cmat[:, g * SSM_STATE:(g + 1) * SSM_STATE].astype(BF16)
        cb = lax.dot_general(c_g, b_g, NT_DIMS, preferred_element_type=F32)
        xs_pair = xs[:, 2 * SSM_HEAD_DIM * k:2 * SSM_HEAD_DIM * (k + 1)]
        dtc = jnp.where(lo, dt_col[:, h0:h0 + 1], dt_col[:, h1:h1 + 1])
        xdt = xs_pair * dtc
        acol = jnp.where(lo, acs_col[:, h0:h0 + 1], acs_col[:, h1:h1 + 1])
        alast0 = acs_row[h0:h0 + 1, Q - 1:Q]
        alast1 = acs_row[h1:h1 + 1, Q - 1:Q]
        alast = jnp.where(lo[0:1, :], alast0, alast1)
        xw = xdt * jnp.exp(alast - acol)
        y_pair = jnp.where(lo, dskip_ref[0:1, h0:h0 + 1], dskip_ref[0:1, h1:h1 + 1]) * xs_pair
        for h, sel in ((h0, lo), (h1, jnp.logical_not(lo))):
            seg = acs_col[:, h:h + 1] - acs_row[h:h + 1, :]
            decay = jnp.exp(jnp.where(causal, seg, -jnp.inf))
            m = (cb * decay).astype(BF16)
            y_pair = y_pair + jnp.dot(m, jnp.where(sel, xdt, 0.0).astype(BF16), preferred_element_type=F32)
        states = jnp.dot(xw.T.astype(BF16), b_g, preferred_element_type=F32)
        h_prev = h_ref[k]
        y_off = lax.dot_general(c_g, h_prev.astype(BF16), NT_DIMS, preferred_element_type=F32)
        y_pair = y_pair + y_off * jnp.exp(acol)
        dec = jnp.where(top[:, 0:1], jnp.exp(alast0), jnp.exp(alast1))
        h_ref[k] = h_prev * dec + states
        pairs.append(y_pair)
    y = jnp.concatenate(pairs, axis=1)
    y_ref[...] = _gated_group_norm(y, z_ref[...], normw_ref[...]).astype(BF16)

    @pl.when(c == last)
    def _():
        conv_out_ref[0] = xbc_ref[Q - (CONV_WIDTH - 1):Q, :]
        state_out_ref[0] = h_ref[...]


def _ssd(xbc, z, dt, dtT, sw, B, S):
    convw, convb, dtb, dtbT, alog, alogT, dskip, normw = sw
    nc = S // SSD_CHUNK
    Q = SSD_CHUNK
    rowb = lambda n: pl.BlockSpec((Q, n), lambda b, c: (b * nc + c, 0))
    npair = SSM_HEADS // 2
    return pl.pallas_call(
        _ssd_kernel,
        grid=(B, nc),
        in_specs=[rowb(CONV_DIM), rowb(SSM_INNER), rowb(SSM_HEADS),
                  pl.BlockSpec((SSM_HEADS, Q), lambda b, c: (0, b * nc + c)),
                  _full(convw.shape), _full(convb.shape), _full(dtb.shape), _full(dtbT.shape),
                  _full(alog.shape), _full(alogT.shape), _full(dskip.shape), _full(normw.shape)],
        out_specs=(rowb(SSM_INNER),
                   pl.BlockSpec((1, CONV_WIDTH - 1, CONV_DIM), lambda b, c: (b, 0, 0)),
                   pl.BlockSpec((1, npair, 2 * SSM_HEAD_DIM, SSM_STATE), lambda b, c: (b, 0, 0, 0))),
        out_shape=(jax.ShapeDtypeStruct((B * S, SSM_INNER), BF16),
                   jax.ShapeDtypeStruct((B, CONV_WIDTH - 1, CONV_DIM), F32),
                   jax.ShapeDtypeStruct((B, npair, 2 * SSM_HEAD_DIM, SSM_STATE), F32)),
        scratch_shapes=[pltpu.VMEM((Q + 8, CONV_DIM), F32),
                        pltpu.VMEM((npair, 2 * SSM_HEAD_DIM, SSM_STATE), F32)],
        compiler_params=_cparams("parallel", "arbitrary"),
        name="ssd_scan",
    )(xbc, z, dt, dtT, convw, convb, dtb, dtbT, alog, alogT, dskip, normw)


def _ssd_step_kernel(xbc_ref, z_ref, dt_ref, cstate_ref, hstate_ref, convw_ref, convb_ref, dtb_ref, alog_ref,
                     dskip_ref, normw_ref, y_ref, cout_ref, hout_ref):
    bt = xbc_ref.shape[0]
    P, N = SSM_HEAD_DIM, SSM_STATE
    new = xbc_ref[...]
    conv = convb_ref[...] + new * convw_ref[CONV_WIDTH - 1:CONV_WIDTH, :]
    for i in range(CONV_WIDTH - 1):
        conv = conv + cstate_ref[:, i * CONV_DIM:(i + 1) * CONV_DIM] * convw_ref[i:i + 1, :]
    cout_ref[:, 0:(CONV_WIDTH - 2) * CONV_DIM] = cstate_ref[:, CONV_DIM:(CONV_WIDTH - 1) * CONV_DIM]
    cout_ref[:, (CONV_WIDTH - 2) * CONV_DIM:] = new
    conv = _silu(conv)
    xs = conv[:, :SSM_INNER]
    bmat = conv[:, SSM_INNER:SSM_INNER + SSM_GROUPS * N]
    cmat = conv[:, SSM_INNER + SSM_GROUPS * N:]
    dt = _softplus(dt_ref[...] + dtb_ref[...])
    dec = jnp.exp(dt * (-jnp.exp(alog_ref[...])))
    H = SSM_HEADS
    HP = H * P
    exact = lax.Precision.HIGHEST
    eye_n = (lax.broadcasted_iota(jnp.int32, (N, N), 0) == lax.broadcasted_iota(jnp.int32, (N, N), 1)).astype(F32)
    x_t = jnp.concatenate(
        [lax.dot_general(eye_n, xs[:, c * N:(c + 1) * N], NT_DIMS, preferred_element_type=F32, precision=exact)
         for c in range(HP // N)], axis=0)
    rep = (lax.broadcasted_iota(jnp.int32, (HP, H), 0) // P == lax.broadcasted_iota(jnp.int32, (HP, H), 1)).astype(F32)
    per_head = jnp.concatenate([dt, dec, jnp.broadcast_to(dskip_ref[...], (8, H))], axis=0)
    cols = lax.dot_general(rep, per_head, NT_DIMS, preferred_element_type=F32, precision=exact)
    d_col = cols[:, 2 * bt:2 * bt + 1]
    first_group = lax.broadcasted_iota(jnp.int32, (HP, 1), 0) < HP // SSM_GROUPS
    lane_b = lax.broadcasted_iota(jnp.int32, (HP, bt), 1)
    y_t = jnp.zeros((HP, bt), F32)
    for b in range(bt):
        x_col = x_t[:, b:b + 1]
        b_sel = jnp.where(first_group, bmat[b:b + 1, 0:N], bmat[b:b + 1, N:2 * N])
        h_new = (hstate_ref[b].reshape(HP, N) * cols[:, bt + b:bt + b + 1]
                 + (x_col * cols[:, b:b + 1]) * b_sel)
        hout_ref[b] = h_new.reshape(H, P, N)
        c2 = jnp.concatenate([cmat[b:b + 1, 0:N], cmat[b:b + 1, N:2 * N], jnp.zeros((6, N), F32)], axis=0)
        yb = lax.dot_general(h_new.astype(BF16), c2.astype(BF16), NT_DIMS, preferred_element_type=F32)
        y_col = jnp.where(first_group, yb[:, 0:1], yb[:, 1:2]) + d_col * x_col
        y_t = jnp.where(lane_b == b, y_col, y_t)
    eye_b = (lax.broadcasted_iota(jnp.int32, (bt, bt), 0) == lax.broadcasted_iota(jnp.int32, (bt, bt), 1)).astype(F32)
    y = lax.dot_general(eye_b, y_t, NT_DIMS, preferred_element_type=F32, precision=exact)
    y_ref[...] = _gated_group_norm(y, z_ref[...], normw_ref[...]).astype(BF16)


def _ssd_step(xbc, z, dt, cstate, hstate, layer, sw, bt=8):
    convw, convb, dtb, _, alog, _, dskip, normw = sw
    Bs = xbc.shape[0]
    cw = (CONV_WIDTH - 1) * CONV_DIM
    rowb = lambda n: pl.BlockSpec((bt, n), lambda i: (i, 0))
    hshape = (bt, SSM_HEADS, SSM_HEAD_DIM, SSM_STATE)
    return pl.pallas_call(
        _ssd_step_kernel,
        grid=(Bs // bt,),
        in_specs=[rowb(CONV_DIM), rowb(SSM_INNER), rowb(SSM_HEADS),
                  pl.BlockSpec((None, bt, cw), lambda i: (layer, i, 0)),
                  pl.BlockSpec((None,) + hshape, lambda i: (layer, i, 0, 0, 0)),
                  _full(convw.shape), _full(convb.shape), _full(dtb.shape), _full(alog.shape),
                  _full(dskip.shape), _full(normw.shape)],
        out_specs=(rowb(SSM_INNER), rowb(cw), pl.BlockSpec(hshape, lambda i: (i, 0, 0, 0))),
        out_shape=(jax.ShapeDtypeStruct((Bs, SSM_INNER), BF16),
                   jax.ShapeDtypeStruct((Bs, cw), F32),
                   jax.ShapeDtypeStruct(hstate.shape[1:], F32)),
        compiler_params=_cparams("parallel"),
        name="ssd_step",
    )(xbc, z, dt, cstate, hstate, convw, convb, dtb, alog, dskip, normw)


def _lambda_value(lp, lam_init):
    d1 = jnp.sum(lp[0:1, :] * lp[1:2, :], axis=1, keepdims=True)
    d2 = jnp.sum(lp[2:3, :] * lp[3:4, :], axis=1, keepdims=True)
    return jnp.exp(d1) - jnp.exp(d2) + lam_init


def _head_norm(o, nw, lam_init):
    return o * lax.rsqrt(jnp.mean(o * o, axis=-1, keepdims=True) + NORM_EPS) * nw * (1.0 - lam_init)


def _attn_kernel(lp_ref, nw_ref, q_ref, k_ref, v_ref, o_ref, m_ref, l_ref, acc_ref, *, lam_init):
    tq = q_ref.shape[0]
    qi = pl.program_id(2)
    q = q_ref[...]
    lane = lax.broadcasted_iota(jnp.int32, q.shape, 1)
    zero = jnp.zeros_like(q)
    q_maps = (jnp.where(lane < ATT_QK_DIM, q, zero), jnp.where(lane >= ATT_QK_DIM, q, zero))
    m_ref[...] = jnp.full(m_ref.shape, -jnp.inf, F32)
    l_ref[...] = jnp.zeros(l_ref.shape, F32)
    acc_ref[...] = jnp.zeros(acc_ref.shape, F32)
    below_diag = (lax.broadcasted_iota(jnp.int32, (tq, tq), 1) <= lax.broadcasted_iota(jnp.int32, (tq, tq), 0))

    def block(j, diagonal):
        start = pl.multiple_of(j * tq, tq)
        k = k_ref[pl.ds(start, tq), :]
        v = v_ref[pl.ds(start, tq), :]
        for i in range(2):
            s = lax.dot_general(q_maps[i], k, NT_DIMS, preferred_element_type=F32)
            if diagonal:
                s = jnp.where(below_diag, s, -jnp.inf)
            m_old = m_ref[i]
            m_new = jnp.maximum(m_old, jnp.max(s, axis=1, keepdims=True))
            alpha = jnp.exp(m_old - m_new)
            p = jnp.exp(s - m_new)
            l_ref[i] = alpha * l_ref[i] + jnp.sum(p, axis=1, keepdims=True)
            acc_ref[i] = alpha * acc_ref[i] + jnp.dot(p.astype(BF16), v, preferred_element_type=F32)
            m_ref[i] = m_new

    def full_block(j, carry):
        block(j, False)
        return carry

    lax.fori_loop(0, qi, full_block, 0)
    block(qi, True)
    lam = _lambda_value(lp_ref[...], lam_init)
    o = acc_ref[0] / l_ref[0] - lam * (acc_ref[1] / l_ref[1])
    o_ref[...] = _head_norm(o, nw_ref[...], lam_init).astype(BF16)


def _attention(q, k, v, lp, nw, lam_init, B, S):
    tq = min(256, S)
    nq = S // tq
    qspec = pl.BlockSpec((tq, ATT_V_DIM), lambda b, h, i: (b * nq + i, h))
    kspec = pl.BlockSpec((S, ATT_V_DIM), lambda b, h, i: (b, h))
    return pl.pallas_call(
        functools.partial(_attn_kernel, lam_init=lam_init),
        grid=(B, ATT_HEADS, nq),
        in_specs=[_full(lp.shape), _full(nw.shape), qspec, kspec, kspec],
        out_specs=qspec,
        out_shape=jax.ShapeDtypeStruct((B * S, ATT_COLS), BF16),
        scratch_shapes=[pltpu.VMEM((2, tq, 1), F32), pltpu.VMEM((2, tq, 1), F32), pltpu.VMEM((2, tq, ATT_V_DIM), F32)],
        compiler_params=_cparams("parallel", "parallel", "arbitrary"),
        name="diff_attention",
    )(lp, nw, q, k, v)


def _decode_attn_kernel(pt_ref, lp_ref, nw_ref, q_ref, kn_ref, vn_ref, *refs, n_pages, lam_init):
    k_refs = refs[:n_pages]
    v_refs = refs[n_pages:2 * n_pages]
    o_ref = refs[2 * n_pages]
    R = 2 * ATT_HEADS
    page = k_refs[0].shape[2]
    r = lax.broadcasted_iota(jnp.int32, (R, ATT_COLS), 0)
    grp = lax.broadcasted_iota(jnp.int32, (R, ATT_COLS), 1) // ATT_QK_DIM
    target = jnp.where(r < ATT_HEADS, 2 * r, 2 * (r - ATT_HEADS) + 1)
    qf = jnp.where(grp == target, q_ref[0].astype(BF16).astype(F32), 0.0)
    qt = qf.astype(BF16)
    s = [jnp.dot(qt, k_ref[...].reshape(ATT_COLS, page).astype(BF16), preferred_element_type=F32)
         for k_ref in k_refs]
    s_new = jnp.sum(qf * kn_ref[0].astype(BF16).astype(F32), axis=1, keepdims=True)
    m = s_new
    for sj in s:
        m = jnp.maximum(m, jnp.max(sj, axis=1, keepdims=True))
    p = [jnp.exp(sj - m) for sj in s]
    p_new = jnp.exp(s_new - m)
    l = p_new
    for pj in p:
        l = l + jnp.sum(pj, axis=1, keepdims=True)
    inv_l = 1.0 / l
    lam = _lambda_value(lp_ref[...], lam_init)
    diff = lambda t: t - lam * pltpu.roll(t, ATT_HEADS, 0)
    a = [diff(pj * inv_l).astype(BF16) for pj in p]
    a_new = diff(jnp.broadcast_to(p_new * inv_l, (R, 128)))[:, 0:1].astype(BF16).astype(F32)
    row = lax.broadcasted_iota(jnp.int32, (R, ATT_V_DIM), 0)
    out = a_new * vn_ref[0].astype(BF16).astype(F32)
    for h in range(ATT_HEADS):
        acc = jnp.zeros((R, ATT_V_DIM), F32)
        for aj, v_ref in zip(a, v_refs):
            v_h = v_ref[pl.ds(h, page, stride=ATT_HEADS), :].astype(BF16)
            acc = acc + jnp.dot(aj, v_h, preferred_element_type=F32)
        out = out + jnp.where(row == h, acc, 0.0)
    o_ref[0] = _head_norm(out, nw_ref[...], lam_init)


def _decode_attention(pt, layer, q, k_new, v_new, cache_k, cache_v, lp, nw, lam_init):
    Bs, n_pages = pt.shape
    R = 2 * ATT_HEADS
    qspec = pl.BlockSpec((1, 1, ATT_COLS), lambda b, pt: (b, 0, 0))
    vspec = pl.BlockSpec((1, R, ATT_V_DIM), lambda b, pt: (b, 0, 0))
    kpages = [pl.BlockSpec((None, None) + cache_k.shape[2:], lambda b, pt, j=j: (layer, pt[b, j], 0, 0, 0))
              for j in range(n_pages)]
    vpages = [pl.BlockSpec((None, None) + cache_v.shape[2:], lambda b, pt, j=j: (layer, pt[b, j], 0, 0))
              for j in range(n_pages)]
    grid_spec = pltpu.PrefetchScalarGridSpec(
        num_scalar_prefetch=1,
        grid=(Bs,),
        in_specs=[pl.BlockSpec(lp.shape, lambda b, pt: (0, 0)), pl.BlockSpec(nw.shape, lambda b, pt: (0, 0)),
                  qspec, qspec, vspec] + kpages + vpages,
        out_specs=vspec,
    )
    return pl.pallas_call(
        functools.partial(_decode_attn_kernel, n_pages=n_pages, lam_init=lam_init),
        grid_spec=grid_spec,
        out_shape=jax.ShapeDtypeStruct((Bs, R, ATT_V_DIM), F32),
        compiler_params=_cparams("parallel"),
        name="decode_attention",
    )(pt, lp, nw, q, k_new, v_new, *([cache_k] * n_pages), *([cache_v] * n_pages))


def _proj_ln_kernel(*refs, n_in, alpha):
    a_refs = refs[:n_in]
    w_refs = refs[n_in:2 * n_in]
    x_ref, g_ref, b_ref, o_ref = refs[2 * n_in:]
    acc = alpha * x_ref[...]
    for a_ref, w_ref in zip(a_refs, w_refs):
        acc = acc + jnp.dot(a_ref[...].astype(BF16), w_ref[...], preferred_element_type=F32)
    o_ref[...] = _layer_norm(acc, g_ref[...], b_ref[...])


def _proj_ln(acts, ws, x, g, b, alpha, tm):
    T, D = x.shape
    row = lambda n: pl.BlockSpec((tm, n), lambda i: (i, 0))
    return pl.pallas_call(
        functools.partial(_proj_ln_kernel, n_in=len(acts), alpha=alpha),
        grid=(T // tm,),
        in_specs=[row(a.shape[1]) for a in acts] + [_full(w.shape) for w in ws] + [row(D), _full(g.shape), _full(b.shape)],
        out_specs=row(D),
        out_shape=jax.ShapeDtypeStruct((T, D), F32),
        compiler_params=_cparams("parallel"),
        name="proj_ln",
    )(*acts, *ws, x, g, b)


def _matmul_kernel(x_ref, w_ref, *o_refs):
    y = jnp.dot(x_ref[...].astype(BF16), w_ref[...], preferred_element_type=F32)
    for o_ref in o_refs:
        o_ref[...] = y.astype(o_ref.dtype)


def _matmul(x, w, out_dtypes, tm, tn):
    M, K = x.shape
    N = w.shape[1]
    ospec = pl.BlockSpec((tm, tn), lambda i, j: (i, j))
    return pl.pallas_call(
        _matmul_kernel,
        grid=(M // tm, N // tn),
        in_specs=[pl.BlockSpec((tm, K), lambda i, j: (i, 0)), pl.BlockSpec((K, tn), lambda i, j: (0, j))],
        out_specs=tuple(ospec for _ in out_dtypes),
        out_shape=tuple(jax.ShapeDtypeStruct((M, N), d) for d in out_dtypes),
        compiler_params=_cparams("parallel", "parallel"),
        name="matmul",
    )(x, w)


def _softmax_rows(s):
    m = jnp.max(s, axis=1, keepdims=True)
    p = jnp.exp(s - m)
    return p / jnp.sum(p, axis=1, keepdims=True)


def _cross_kernel(x_ref, wq_ref, mk_ref, mv_ref, wo_ref, g_ref, b_ref, o_ref, *, alpha):
    x = x_ref[...]
    D = x.shape[1]
    dh = D // MEM_HEADS
    q = jnp.dot(x.astype(BF16), wq_ref[...], preferred_element_type=F32).astype(BF16)
    outs = []
    for h in range(MEM_HEADS):
        sl = slice(h * dh, (h + 1) * dh)
        s = lax.dot_general(q[:, sl], mk_ref[:, sl], NT_DIMS, preferred_element_type=F32) * (dh ** -0.5)
        outs.append(jnp.dot(_softmax_rows(s).astype(BF16), mv_ref[:, sl], preferred_element_type=F32).astype(BF16))
    o = jnp.concatenate(outs, axis=1)
    y = alpha * x + jnp.dot(o, wo_ref[...], preferred_element_type=F32)
    o_ref[...] = _layer_norm(y, g_ref[...], b_ref[...])


def _cross_attention(x, wq, mk, mv, wo, g, b, alpha, B, S, n_mem, tm):
    T, D = x.shape
    nt = S // tm
    row = pl.BlockSpec((tm, D), lambda bi, i: (bi * nt + i, 0))
    mem = pl.BlockSpec((n_mem, D), lambda bi, i: (bi, 0))
    return pl.pallas_call(
        functools.partial(_cross_kernel, alpha=alpha),
        grid=(B, nt),
        in_specs=[row, _full(wq.shape), mem, mem, _full(wo.shape), _full(g.shape), _full(b.shape)],
        out_specs=row,
        out_shape=jax.ShapeDtypeStruct((T, D), F32),
        compiler_params=_cparams("parallel", "parallel"),
        name="cross_attention",
    )(x, wq, mk, mv, wo, g, b)


def _cross_decode_kernel(q_ref, mk_ref, mv_ref, o_ref):
    R, dh = q_ref.shape[1], q_ref.shape[2]
    q = q_ref[0].astype(BF16)
    row = lax.broadcasted_iota(jnp.int32, (R, dh), 0)
    out = jnp.zeros((R, dh), F32)
    for h in range(MEM_HEADS):
        s = lax.dot_general(q, mk_ref[:, h, :].astype(BF16), NT_DIMS, preferred_element_type=F32) * (dh ** -0.5)
        o = jnp.dot(_softmax_rows(s).astype(BF16), mv_ref[:, h, :].astype(BF16), preferred_element_type=F32)
        out = out + jnp.where(row == h, o, 0.0)
    o_ref[0] = out


def _cross_decode(q, mem_k, mem_v, layer):
    Bs, R, dh = q.shape
    tok = pl.BlockSpec((1, R, dh), lambda b: (b, 0, 0))
    mem = pl.BlockSpec((None, None) + mem_k.shape[2:], lambda b: (layer, b, 0, 0, 0))
    return pl.pallas_call(
        _cross_decode_kernel,
        grid=(Bs,),
        in_specs=[tok, mem, mem],
        out_specs=tok,
        out_shape=jax.ShapeDtypeStruct((Bs, R, dh), F32),
        compiler_params=_cparams("parallel"),
        name="cross_decode",
    )(q, mem_k, mem_v)


def _ffn_kernel(x_ref, wg_ref, wu_ref, wd_ref, g_ref, b_ref, o_ref, acc_ref, *, alpha):
    f = pl.program_id(1)
    x = x_ref[...]
    xb = x.astype(BF16)

    @pl.when(f == 0)
    def _():
        acc_ref[...] = alpha * x

    gate = jnp.dot(xb, wg_ref[...], preferred_element_type=F32)
    up = jnp.dot(xb, wu_ref[...], preferred_element_type=F32)
    acc_ref[...] += jnp.dot((_silu(gate) * up).astype(BF16), wd_ref[...], preferred_element_type=F32)

    @pl.when(f == pl.num_programs(1) - 1)
    def _():
        o_ref[...] = _layer_norm(acc_ref[...], g_ref[...], b_ref[...])


def _ffn(x, w_gu, w_down, g, b, alpha, tm, tf):
    T, D = x.shape
    F = w_down.shape[0]
    nf = F // tf
    row = pl.BlockSpec((tm, D), lambda i, f: (i, 0))
    return pl.pallas_call(
        functools.partial(_ffn_kernel, alpha=alpha),
        grid=(T // tm, nf),
        in_specs=[row, pl.BlockSpec((D, tf), lambda i, f: (0, f)), pl.BlockSpec((D, tf), lambda i, f: (0, nf + f)),
                  pl.BlockSpec((tf, D), lambda i, f: (f, 0)), _full(g.shape), _full(b.shape)],
        out_specs=row,
        out_shape=jax.ShapeDtypeStruct((T, D), F32),
        scratch_shapes=[pltpu.VMEM((tm, D), F32)],
        compiler_params=_cparams("parallel", "arbitrary"),
        name="ffn",
    )(x, w_gu, w_gu, w_down, g, b)


def _router_kernel(x_ref, wr_ref, gate_ref):
    logits = jnp.dot(x_ref[...].astype(BF16), wr_ref[...], preferred_element_type=F32)
    lane = lax.broadcasted_iota(jnp.int32, logits.shape, 1).astype(F32)
    none = float(N_EXPERTS)
    m1 = jnp.max(logits, axis=1, keepdims=True)
    i1 = jnp.min(jnp.where(logits == m1, lane, none), axis=1, keepdims=True)
    rest = jnp.where(lane == i1, -jnp.inf, logits)
    m2 = jnp.max(rest, axis=1, keepdims=True)
    i2 = jnp.min(jnp.where(rest == m2, lane, none), axis=1, keepdims=True)
    e = jnp.exp(m2 - m1)
    den = 1.0 + e
    gate_ref[...] = jnp.where(lane == i1, 1.0 / den, 0.0) + jnp.where(lane == i2, e / den, 0.0)


def _router(x, wr, tm):
    T, D = x.shape
    return pl.pallas_call(
        _router_kernel,
        grid=(T // tm,),
        in_specs=[pl.BlockSpec((tm, D), lambda i: (i, 0)), _full(wr.shape)],
        out_specs=pl.BlockSpec((tm, N_EXPERTS), lambda i: (i, 0)),
        out_shape=jax.ShapeDtypeStruct((T, N_EXPERTS), F32),
        compiler_params=_cparams("parallel"),
        name="router",
    )(x, wr)


def _moe_kernel(x_ref, gate_ref, wg_ref, wu_ref, wd_ref, g_ref, b_ref, o_ref, acc_ref, *, alpha):
    e = pl.program_id(1)
    f = pl.program_id(2)
    x = x_ref[...]
    xb = x.astype(BF16)

    @pl.when(jnp.logical_and(e == 0, f == 0))
    def _():
        acc_ref[...] = alpha * x

    gates = gate_ref[...]
    lane = lax.broadcasted_iota(jnp.int32, gates.shape, 1)
    ge = jnp.sum(jnp.where(lane == e, gates, 0.0), axis=1, keepdims=True)
    gate = jnp.dot(xb, wg_ref[0], preferred_element_type=F32)
    up = jnp.dot(xb, wu_ref[0], preferred_element_type=F32)
    acc_ref[...] += jnp.dot((_silu(gate) * up).astype(BF16), wd_ref[0], preferred_element_type=F32) * ge

    @pl.when(jnp.logical_and(e == pl.num_programs(1) - 1, f == pl.num_programs(2) - 1))
    def _():
        o_ref[...] = _layer_norm(acc_ref[...], g_ref[...], b_ref[...])


def _moe(x, gates, w_gu, w_down, g, b, alpha, tm, tf):
    T, D = x.shape
    E, F, _ = w_down.shape
    nf = F // tf
    row = pl.BlockSpec((tm, D), lambda i, e, f: (i, 0))
    return pl.pallas_call(
        functools.partial(_moe_kernel, alpha=alpha),
        grid=(T // tm, E, nf),
        in_specs=[row, pl.BlockSpec((tm, E), lambda i, e, f: (i, 0)),
                  pl.BlockSpec((1, D, tf), lambda i, e, f: (e, 0, f)),
                  pl.BlockSpec((1, D, tf), lambda i, e, f: (e, 0, nf + f)),
                  pl.BlockSpec((1, tf, D), lambda i, e, f: (e, f, 0)), _full(g.shape), _full(b.shape)],
        out_specs=row,
        out_shape=jax.ShapeDtypeStruct((T, D), F32),
        scratch_shapes=[pltpu.VMEM((tm, D), F32)],
        compiler_params=_cparams("parallel", "arbitrary", "arbitrary"),
        name="moe",
    )(x, gates, w_gu, w_gu, w_down, g, b)


def _rope_tables(pos):
    half = ATT_QK_DIM // 2
    inv = ROPE_THETA ** (-jnp.arange(0, ATT_QK_DIM, 2, dtype=F32) / ATT_QK_DIM)
    ang = pos.astype(F32)[:, None] * inv[None, :]
    cos, sin = jnp.cos(ang), jnp.sin(ang)
    return jnp.concatenate([cos] * 4, axis=1), jnp.concatenate([-sin, sin, -sin, sin], axis=1)


def _row_tile(n, target):
    t = min(n, target)
    while n % t:
        t //= 2
    return t


def kernel(x_prompt, x_sample, cache_attn_k, cache_attn_v, cache_mem_k, cache_mem_v, state_conv, state_ssm,
           page_table, mem_prompt, ln_g, ln_b, w_in, conv_w, conv_b, dt_bias, a_log, d_skip, ssm_norm_w,
           lambda_params, attn_norm_w, w_out, w_cq, w_ckv, w_co, w_ffn_gu, w_ffn_down, w_router, w_exp_gu,
           w_exp_down):
    B, S, D = x_prompt.shape
    Bs = x_sample.shape[0]
    depth = w_in.shape[0]
    n_phys, page = cache_attn_k.shape[1], cache_attn_k.shape[2]
    n_pages = page_table.shape[1]
    past_len = n_pages * page
    n_mem = mem_prompt.shape[1]
    alpha = (2 * depth) ** 0.25
    assert S % SSD_CHUNK == 0 and x_sample.shape[1] == 1

    tm_p = _row_tile(S, 512)
    tm_s = _row_tile(Bs, 128)
    cos_p, sin_p = _rope_tables(jnp.arange(S, dtype=jnp.int32))
    cos_s, sin_s = _rope_tables(jnp.full((tm_s,), past_len, jnp.int32))

    hp = x_prompt.reshape(B * S, D)
    hs = x_sample.reshape(Bs, D)
    mem = mem_prompt.reshape(B * n_mem, D)
    cache_k = cache_attn_k.transpose(0, 1, 3, 4, 2)
    cache_v = cache_attn_v.reshape(depth, n_phys, page * ATT_HEADS, ATT_V_DIM)
    cmem_k, cmem_v = cache_mem_k, cache_mem_v
    cstate = state_conv.reshape(depth, Bs, (CONV_WIDTH - 1) * CONV_DIM)
    pad_heads = lambda t: jnp.pad(t, [(0, 0)] * (t.ndim - 2) + [(0, 2 * ATT_HEADS - t.shape[-2]), (0, 0)])

    splits = (ATT_COLS, 2 * ATT_COLS, 3 * ATT_COLS, 3 * ATT_COLS + SSM_INNER, 3 * ATT_COLS + SSM_INNER + CONV_DIM)
    outs = {n: [] for n in ("kp", "vp", "mkp", "mvp", "cp", "sp", "ks", "vs", "cs", "ss")}
    for l in range(depth):
        lam_init = 0.8 - 0.6 * math.exp(-0.3 * l)
        wl = w_in[l].astype(BF16)
        wq, wk, wv, wz, wxbc, wdt = (wl[:, a:b] for a, b in zip((0,) + splits, splits + (wl.shape[1],)))
        w_proj = (wq, wk, wv, wz, wxbc, wdt, wdt.T)
        sw = (conv_w[l], conv_b[l][None, :], dt_bias[l][None, :], dt_bias[l][:, None], a_log[l][None, :],
              a_log[l][:, None], d_skip[l][None, :], ssm_norm_w[l][None, :])
        lp = lambda_params[l]
        nw = attn_norm_w[l][None, :]
        wo = w_out[l].astype(BF16)
        wo_att, wo_ssm = wo[:ATT_COLS], wo[ATT_COLS:]
        wcq = w_cq[l].astype(BF16)
        wckv = w_ckv[l].astype(BF16)
        wco = w_co[l].astype(BF16)
        g0, g1, g2 = (ln_g[l, i][None, :] for i in range(3))
        b0, b1, b2 = (ln_b[l, i][None, :] for i in range(3))

        q, kf, kb, vf, vb, z, xbc, dt, dtT = _in_proj(hp, cos_p, sin_p, w_proj, tm_p, S // tm_p)
        att = _attention(q, kb, vb, lp, nw, lam_init, B, S)
        y, conv_new, h_last = _ssd(xbc, z, dt, dtT, sw, B, S)
        hp = _proj_ln((att, y), (wo_att, wo_ssm), hp, g0, b0, alpha, tm_p)
        outs["kp"].append(kf.reshape(B, S, 2 * ATT_HEADS, ATT_QK_DIM))
        outs["vp"].append(vf.reshape(B, S, ATT_HEADS, ATT_V_DIM))
        outs["cp"].append(conv_new)
        outs["sp"].append(h_last.reshape(B, SSM_HEADS, SSM_HEAD_DIM, SSM_STATE))
        mkv_f, mkv_b = _matmul(mem, wckv, (F32, BF16), _row_tile(B * n_mem, 512), D)
        outs["mkp"].append(mkv_f[:, :D].reshape(B, n_mem, MEM_HEADS, D // MEM_HEADS))
        outs["mvp"].append(mkv_f[:, D:].reshape(B, n_mem, MEM_HEADS, D // MEM_HEADS))
        hp = _cross_attention(hp, wcq, mkv_b[:, :D], mkv_b[:, D:], wco, g1, b1, alpha, B, S, n_mem, tm_p)

        q, kf, _, vf, _, z, xbc, dt, _ = _in_proj(hs, cos_s, sin_s, w_proj, tm_s, 1)
        att = _decode_attention(page_table, l, q.astype(F32).reshape(Bs, 1, ATT_COLS), kf.reshape(Bs, 1, ATT_COLS),
                                pad_heads(vf.reshape(Bs, ATT_HEADS, ATT_V_DIM)), cache_k, cache_v, lp, nw, lam_init)
        att = att[:, :ATT_HEADS].reshape(Bs, ATT_COLS)
        y, conv_new, h_new = _ssd_step(xbc, z, dt, cstate, state_ssm, l, sw)
        hs = _proj_ln((att, y), (wo_att, wo_ssm), hs, g0, b0, alpha, tm_s)
        outs["ks"].append(kf.reshape(Bs, 1, 2 * ATT_HEADS, ATT_QK_DIM))
        outs["vs"].append(vf.reshape(Bs, 1, ATT_HEADS, ATT_V_DIM))
        outs["cs"].append(conv_new.reshape(Bs, CONV_WIDTH - 1, CONV_DIM))
        outs["ss"].append(h_new)
        (qc,) = _matmul(hs, wcq, (F32,), tm_s, D)
        oc = _cross_decode(pad_heads(qc.reshape(Bs, MEM_HEADS, D // MEM_HEADS)), cmem_k, cmem_v, l)
        hs = _proj_ln((oc[:, :MEM_HEADS].reshape(Bs, D),), (wco,), hs, g1, b1, alpha, tm_s)

        if l % 2 == 0:
            wgu = w_ffn_gu[l // 2].astype(BF16)
            wd = w_ffn_down[l // 2].astype(BF16)
            tf = wd.shape[0] // 2
            hp = _ffn(hp, wgu, wd, g2, b2, alpha, tm_p, tf)
            hs = _ffn(hs, wgu, wd, g2, b2, alpha, tm_s, tf)
        else:
            wr = w_router[l // 2].astype(BF16)
            wgu = w_exp_gu[l // 2].astype(BF16)
            wd = w_exp_down[l // 2].astype(BF16)
            tf = wd.shape[1] // 2
            hp = _moe(hp, _router(hp, wr, tm_p), wgu, wd, g2, b2, alpha, tm_p, tf)
            hs = _moe(hs, _router(hs, wr, tm_s), wgu, wd, g2, b2, alpha, tm_s, tf)

    st = {n: jnp.stack(v) for n, v in outs.items()}
    return (hp.reshape(B, S, D), hs.reshape(Bs, 1, D), st["kp"], st["vp"], st["mkp"], st["mvp"], st["cp"],
            st["sp"], st["ks"], st["vs"], st["cs"], st["ss"])
```

```python
import functools
import math

import jax
import jax.numpy as jnp
from jax import lax
from jax.experimental import pallas as pl
from jax.experimental.pallas import tpu as pltpu

F32 = jnp.float32
BF16 = jnp.bfloat16

ATT_HEADS = 4
ATT_V_DIM = 128
ATT_QK_DIM = 64
ATT_COLS = 512
SSM_INNER = 512
SSM_HEAD_DIM = 64
SSM_HEADS = 8
SSM_GROUPS = 2
SSM_STATE = 128
CONV_WIDTH = 4
CONV_DIM = 1024
SSD_CHUNK = 128
MEM_HEADS = 4
N_EXPERTS = 8
ROPE_THETA = 10000.0
LN_EPS = 1e-5
NORM_EPS = 1e-5
QK_SCALE = ATT_QK_DIM ** -0.5

LANES = 128
VMEM_LIMIT_BYTES = 56 * 1024 * 1024
NT_DIMS = (((1,), (1,)), ((), ()))


def _cparams(*sem):
    return pltpu.CompilerParams(dimension_semantics=sem, vmem_limit_bytes=VMEM_LIMIT_BYTES)


def _silu(x):
    return x / (1.0 + jnp.exp(-x))


def _softplus(x):
    return jnp.maximum(x, 0.0) + jnp.log1p(jnp.exp(-jnp.abs(x)))


def _layer_norm(y, g, b):
    mu = jnp.mean(y, axis=-1, keepdims=True)
    yc = y - mu
    var = jnp.mean(yc * yc, axis=-1, keepdims=True)
    return yc * lax.rsqrt(var + LN_EPS) * g + b


def _full(shape):
    return pl.BlockSpec(shape, lambda *_: (0,) * len(shape))


def _in_proj_kernel(x_ref, cos_ref, sin_ref, wq_ref, wk_ref, wv_ref, wz_ref, wxbc_ref, wdt_ref, wdtT_ref,
                    q_ref, kf_ref, kb_ref, vf_ref, vb_ref, z_ref, xbc_ref, dt_ref, dtT_ref):
    tm = x_ref.shape[0]
    xb = x_ref[...].astype(BF16)
    cos = jnp.concatenate([cos_ref[...]] * 4, axis=1)
    sin = jnp.concatenate([sin_ref[...]] * 4, axis=1)
    lane = lax.broadcasted_iota(jnp.int32, (tm, ATT_COLS), 1)
    first_half = (lane & (ATT_QK_DIM - 1)) < (ATT_QK_DIM // 2)

    def rope(t):
        partner = jnp.where(first_half, pltpu.roll(t, ATT_COLS - ATT_QK_DIM // 2, 1),
                            pltpu.roll(t, ATT_QK_DIM // 2, 1))
        return t * cos + partner * sin

    q = rope(jnp.dot(xb, wq_ref[...], preferred_element_type=F32))
    q_ref[...] = (q * QK_SCALE).astype(BF16)
    k = rope(jnp.dot(xb, wk_ref[...], preferred_element_type=F32))
    kf_ref[...] = k
    kb_ref[...] = k.astype(BF16)
    v = jnp.dot(xb, wv_ref[...], preferred_element_type=F32)
    vf_ref[...] = v
    vb_ref[...] = v.astype(BF16)
    z_ref[...] = jnp.dot(xb, wz_ref[...], preferred_element_type=F32)
    xbc_ref[...] = jnp.dot(xb, wxbc_ref[...], preferred_element_type=F32)
    dt_ref[...] = jnp.dot(xb, wdt_ref[...], preferred_element_type=F32)
    dtT_ref[...] = lax.dot_general(wdtT_ref[...], xb, NT_DIMS, preferred_element_type=F32)


def _in_proj(x, cos_t, sin_t, w, tm, n_pos_blocks):
    T, D = x.shape
    wq, wk, wv, wz, wxbc, wdt, wdtT = w
    row = lambda n: pl.BlockSpec((tm, n), lambda i: (i, 0))
    pos = pl.BlockSpec((tm, 128), lambda i: (i % n_pos_blocks, 0))
    out_shape = (
        jax.ShapeDtypeStruct((T, ATT_COLS), BF16),
        jax.ShapeDtypeStruct((T, ATT_COLS), F32),
        jax.ShapeDtypeStruct((T, ATT_COLS), BF16),
        jax.ShapeDtypeStruct((T, ATT_COLS), F32),
        jax.ShapeDtypeStruct((T, ATT_COLS), BF16),
        jax.ShapeDtypeStruct((T, SSM_INNER), F32),
        jax.ShapeDtypeStruct((T, CONV_DIM), F32),
        jax.ShapeDtypeStruct((T, SSM_HEADS), F32),
        jax.ShapeDtypeStruct((SSM_HEADS, T), F32),
    )
    out_specs = (row(ATT_COLS), row(ATT_COLS), row(ATT_COLS), row(ATT_COLS), row(ATT_COLS),
                 row(SSM_INNER), row(CONV_DIM), row(SSM_HEADS),
                 pl.BlockSpec((SSM_HEADS, tm), lambda i: (0, i)))
    return pl.pallas_call(
        _in_proj_kernel,
        grid=(T // tm,),
        in_specs=[row(D), pos, pos, _full(wq.shape), _full(wk.shape), _full(wv.shape), _full(wz.shape),
                  _full(wxbc.shape), _full(wdt.shape), _full(wdtT.shape)],
        out_specs=out_specs,
        out_shape=out_shape,
        compiler_params=_cparams("parallel"),
        name="in_proj",
    )(x, cos_t, sin_t, wq, wk, wv, wz, wxbc, wdt, wdtT)


def _gated_group_norm(y, z, norm_w):
    y = y * _silu(z)
    half = SSM_INNER // SSM_GROUPS
    y2 = y * y
    ms0 = jnp.mean(y2[:, :half], axis=-1, keepdims=True)
    ms1 = jnp.mean(y2[:, half:], axis=-1, keepdims=True)
    lane = lax.broadcasted_iota(jnp.int32, y.shape, 1)
    scale = jnp.where(lane < half, lax.rsqrt(ms0 + NORM_EPS), lax.rsqrt(ms1 + NORM_EPS))
    return y * scale * norm_w


def _ssd_kernel(xbc_ref, z_ref, dt_ref, dtT_ref, convw_ref, convb_ref, dtb_ref, dtbT_ref, alog_ref, alogT_ref,
                dskip_ref, normw_ref, y_ref, conv_out_ref, state_out_ref, ext_ref, h_ref):
    Q = SSD_CHUNK
    c = pl.program_id(1)
    last = pl.num_programs(1) - 1

    @pl.when(c == 0)
    def _():
        ext_ref[0:8, :] = jnp.zeros((8, CONV_DIM), F32)
        h_ref[...] = jnp.zeros(h_ref.shape, F32)

    ext_ref[8:8 + Q, :] = xbc_ref[...]
    conv = convb_ref[...]
    for i in range(CONV_WIDTH):
        conv = conv + ext_ref[5 + i:5 + i + Q, :] * convw_ref[i:i + 1, :]
    ext_ref[0:8, :] = xbc_ref[Q - 8:Q, :]
    conv = _silu(conv)
    xs = conv[:, :SSM_INNER]
    bmat = conv[:, SSM_INNER:SSM_INNER + SSM_GROUPS * SSM_STATE]
    cmat = conv[:, SSM_INNER + SSM_GROUPS * SSM_STATE:]

    dt_col = _softplus(dt_ref[...] + dtb_ref[...])
    a_col = dt_col * (-jnp.exp(alog_ref[...]))
    a_row = _softplus(dtT_ref[...] + dtbT_ref[...]) * (-jnp.exp(alogT_ref[...]))
    ri = lax.broadcasted_iota(jnp.int32, (Q, Q), 0)
    ci = lax.broadcasted_iota(jnp.int32, (Q, Q), 1)
    causal = ci <= ri
    tril = causal.astype(F32)
    triu = (ri <= ci).astype(F32)
    acs_col = jnp.dot(tril, a_col, preferred_element_type=F32, precision=lax.Precision.HIGHEST)
    acs_row = jnp.dot(a_row, triu, preferred_element_type=F32, precision=lax.Precision.HIGHEST)

    lo = ci < SSM_HEAD_DIM
    top = ri < SSM_HEAD_DIM
    pairs = []
    for k in range(SSM_HEADS // 2):
        h0, h1 = 2 * k, 2 * k + 1
        g = k // (SSM_HEADS // 2 // SSM_GROUPS)
        b_g = bmat[:, g * SSM_STATE:(g + 1) * SSM_STATE].astype(BF16)
        c_g = cmat[:, g * SSM_STATE:(g + 1) * SSM_STATE].astype(BF16)
        cb = lax.dot_general(c_g, b_g, NT_DIMS, preferred_element_type=F32)
        xs_pair = xs[:, 2 * SSM_HEAD_DIM * k:2 * SSM_HEAD_DIM * (k + 1)]
        dtc = jnp.where(lo, dt_col[:, h0:h0 + 1], dt_col[:, h1:h1 + 1])
        xdt = xs_pair * dtc
        acol = jnp.where(lo, acs_col[:, h0:h0 + 1], acs_col[:, h1:h1 + 1])
        alast0 = acs_row[h0:h0 + 1, Q - 1:Q]
        alast1 = acs_row[h1:h1 + 1, Q - 1:Q]
        alast = jnp.where(lo[0:1, :], alast0, alast1)
        xw = xdt * jnp.exp(alast - acol)
        y_pair = jnp.where(lo, dskip_ref[0:1, h0:h0 + 1], dskip_ref[0:1, h1:h1 + 1]) * xs_pair
        for h, sel in ((h0, lo), (h1, jnp.logical_not(lo))):
            seg = acs_col[:, h:h + 1] - acs_row[h:h + 1, :]
            decay = jnp.exp(jnp.where(causal, seg, -jnp.inf))
            m = (cb * decay).astype(BF16)
            y_pair = y_pair + jnp.dot(m, jnp.where(sel, xdt, 0.0).astype(BF16), preferred_element_type=F32)
        states = jnp.dot(xw.T.astype(BF16), b_g, preferred_element_type=F32)
        h_prev = h_ref[k]
        y_off = lax.dot_general(c_g, h_prev.astype(BF16), NT_DIMS, preferred_element_type=F32)
        y_pair = y_pair + y_off * jnp.exp(acol)
        dec = jnp.where(top[:, 0:1], jnp.exp(alast0), jnp.exp(alast1))
        h_ref[k] = h_prev * dec + states
        pairs.append(y_pair)
    y = jnp.concatenate(pairs, axis=1)
    y_ref[...] = _gated_group_norm(y, z_ref[...], normw_ref[...]).astype(BF16)

    @pl.when(c == last)
    def _():
        conv_out_ref[0] = xbc_ref[Q - (CONV_WIDTH - 1):Q, :]
        state_out_ref[0] = h_ref[...]


def _ssd(xbc, z, dt, dtT, sw, B, S):
    convw, convb, dtb, dtbT, alog, alogT, dskip, normw = sw
    nc = S // SSD_CHUNK
    Q = SSD_CHUNK
    rowb = lambda n: pl.BlockSpec((Q, n), lambda b, c: (b * nc + c, 0))
    npair = SSM_HEADS // 2
    return pl.pallas_call(
        _ssd_kernel,
        grid=(B, nc),
        in_specs=[rowb(CONV_DIM), rowb(SSM_INNER), rowb(SSM_HEADS),
                  pl.BlockSpec((SSM_HEADS, Q), lambda b, c: (0, b * nc + c)),
                  _full(convw.shape), _full(convb.shape), _full(dtb.shape), _full(dtbT.shape),
                  _full(alog.shape), _full(alogT.shape), _full(dskip.shape), _full(normw.shape)],
        out_specs=(rowb(SSM_INNER),
                   pl.BlockSpec((1, CONV_WIDTH - 1, CONV_DIM), lambda b, c: (b, 0, 0)),
                   pl.BlockSpec((1, npair, 2 * SSM_HEAD_DIM, SSM_STATE), lambda b, c: (b, 0, 0, 0))),
        out_shape=(jax.ShapeDtypeStruct((B * S, SSM_INNER), BF16),
                   jax.ShapeDtypeStruct((B, CONV_WIDTH - 1, CONV_DIM), F32),
                   jax.ShapeDtypeStruct((B, npair, 2 * SSM_HEAD_DIM, SSM_STATE), F32)),
        scratch_shapes=[pltpu.VMEM((Q + 8, CONV_DIM), F32),
                        pltpu.VMEM((npair, 2 * SSM_HEAD_DIM, SSM_STATE), F32)],
        compiler_params=_cparams("parallel", "arbitrary"),
        name="ssd_scan",
    )(xbc, z, dt, dtT, convw, convb, dtb, dtbT, alog, alogT, dskip, normw)


def _ssd_step_kernel(xbc_ref, z_ref, dt_ref, cstate_ref, hstate_ref, convw_ref, convb_ref, dtb_ref, alog_ref,
                     dskip_ref, normw_ref, y_ref, cout_ref, hout_ref):
    bt = xbc_ref.shape[0]
    P, N = SSM_HEAD_DIM, SSM_STATE
    new = xbc_ref[...]
    conv = convb_ref[...] + new * convw_ref[CONV_WIDTH - 1:CONV_WIDTH, :]
    for i in range(CONV_WIDTH - 1):
        conv = conv + cstate_ref[:, i * CONV_DIM:(i + 1) * CONV_DIM] * convw_ref[i:i + 1, :]
    cout_ref[:, 0:(CONV_WIDTH - 2) * CONV_DIM] = cstate_ref[:, CONV_DIM:(CONV_WIDTH - 1) * CONV_DIM]
    cout_ref[:, (CONV_WIDTH - 2) * CONV_DIM:] = new
    conv = _silu(conv)
    xs = conv[:, :SSM_INNER]
    bmat = conv[:, SSM_INNER:SSM_INNER + SSM_GROUPS * N]
    cmat = conv[:, SSM_INNER + SSM_GROUPS * N:]
    dt = _softplus(dt_ref[...] + dtb_ref[...])
    dec = jnp.exp(dt * (-jnp.exp(alog_ref[...])))
    H = SSM_HEADS
    HP = H * P
    exact = lax.Precision.HIGHEST
    eye_n = (lax.broadcasted_iota(jnp.int32, (N, N), 0) == lax.broadcasted_iota(jnp.int32, (N, N), 1)).astype(F32)
    x_t = jnp.concatenate(
        [lax.dot_general(eye_n, xs[:, c * N:(c + 1) * N], NT_DIMS, preferred_element_type=F32, precision=exact)
         for c in range(HP // N)], axis=0)
    rep = (lax.broadcasted_iota(jnp.int32, (HP, H), 0) // P == lax.broadcasted_iota(jnp.int32, (HP, H), 1)).astype(F32)
    per_head = jnp.concatenate([dt, dec, jnp.broadcast_to(dskip_ref[...], (8, H))], axis=0)
    cols = lax.dot_general(rep, per_head, NT_DIMS, preferred_element_type=F32, precision=exact)
    d_col = cols[:, 2 * bt:2 * bt + 1]
    first_group = lax.broadcasted_iota(jnp.int32, (HP, 1), 0) < HP // SSM_GROUPS
    lane_b = lax.broadcasted_iota(jnp.int32, (HP, bt), 1)
    y_t = jnp.zeros((HP, bt), F32)
    for b in range(bt):
        x_col = x_t[:, b:b + 1]
        b_sel = jnp.where(first_group, bmat[b:b + 1, 0:N], bmat[b:b + 1, N:2 * N])
        h_new = (hstate_ref[b].reshape(HP, N) * cols[:, bt + b:bt + b + 1]
                 + (x_col * cols[:, b:b + 1]) * b_sel)
        hout_ref[b] = h_new.reshape(H, P, N)
        c2 = jnp.concatenate([cmat[b:b + 1, 0:N], cmat[b:b + 1, N:2 * N], jnp.zeros((6, N), F32)], axis=0)
        yb = lax.dot_general(h_new.astype(BF16), c2.astype(BF16), NT_DIMS, preferred_element_type=F32)
        y_col = jnp.where(first_group, yb[:, 0:1], yb[:, 1:2]) + d_col * x_col
        y_t = jnp.where(lane_b == b, y_col, y_t)
    eye_b = (lax.broadcasted_iota(jnp.int32, (bt, bt), 0) == lax.broadcasted_iota(jnp.int32, (bt, bt), 1)).astype(F32)
    y = lax.dot_general(eye_b, y_t, NT_DIMS, preferred_element_type=F32, precision=exact)
    y_ref[...] = _gated_group_norm(y, z_ref[...], normw_ref[...]).astype(BF16)


def _ssd_step(xbc, z, dt, cstate, hstate, layer, sw, bt=8):
    convw, convb, dtb, _, alog, _, dskip, normw = sw
    Bs = xbc.shape[0]
    cw = (CONV_WIDTH - 1) * CONV_DIM
    rowb = lambda n: pl.BlockSpec((bt, n), lambda i: (i, 0))
    hshape = (bt, SSM_HEADS, SSM_HEAD_DIM, SSM_STATE)
    return pl.pallas_call(
        _ssd_step_kernel,
        grid=(Bs // bt,),
        in_specs=[rowb(CONV_DIM), rowb(SSM_INNER), rowb(SSM_HEADS),
                  pl.BlockSpec((None, bt, cw), lambda i: (layer, i, 0)),
                  pl.BlockSpec((None,) + hshape, lambda i: (layer, i, 0, 0, 0)),
                  _full(convw.shape), _full(convb.shape), _full(dtb.shape), _full(alog.shape),
                  _full(dskip.shape), _full(normw.shape)],
        out_specs=(rowb(SSM_INNER), rowb(cw), pl.BlockSpec(hshape, lambda i: (i, 0, 0, 0))),
        out_shape=(jax.ShapeDtypeStruct((Bs, SSM_INNER), BF16),
                   jax.ShapeDtypeStruct((Bs, cw), F32),
                   jax.ShapeDtypeStruct(hstate.shape[1:], F32)),
        compiler_params=_cparams("parallel"),
        name="ssd_step",
    )(xbc, z, dt, cstate, hstate, convw, convb, dtb, alog, dskip, normw)


def _lambda_value(lp, lam_init):
    d1 = jnp.sum(lp[0:1, :] * lp[1:2, :], axis=1, keepdims=True)
    d2 = jnp.sum(lp[2:3, :] * lp[3:4, :], axis=1, keepdims=True)
    return jnp.exp(d1) - jnp.exp(d2) + lam_init


def _head_norm(o, nw, lam_init):
    return o * lax.rsqrt(jnp.mean(o * o, axis=-1, keepdims=True) + NORM_EPS) * nw * (1.0 - lam_init)


def _attn_kernel(lp_ref, nw_ref, q_ref, k_ref, v_ref, o_ref, m_ref, l_ref, acc_ref, *, lam_init):
    tq = q_ref.shape[0]
    qi = pl.program_id(2)
    q = q_ref[...]
    lane = lax.broadcasted_iota(jnp.int32, q.shape, 1)
    zero = jnp.zeros_like(q)
    q_maps = (jnp.where(lane < ATT_QK_DIM, q, zero), jnp.where(lane >= ATT_QK_DIM, q, zero))
    m_ref[...] = jnp.full(m_ref.shape, -jnp.inf, F32)
    l_ref[...] = jnp.zeros(l_ref.shape, F32)
    acc_ref[...] = jnp.zeros(acc_ref.shape, F32)
    below_diag = (lax.broadcasted_iota(jnp.int32, (tq, tq), 1) <= lax.broadcasted_iota(jnp.int32, (tq, tq), 0))

    def block(j, diagonal):
        start = pl.multiple_of(j * tq, tq)
        k = k_ref[pl.ds(start, tq), :]
        v = v_ref[pl.ds(start, tq), :]
        for i in range(2):
            s = lax.dot_general(q_maps[i], k, NT_DIMS, preferred_element_type=F32)
            if diagonal:
                s = jnp.where(below_diag, s, -jnp.inf)
            chunks = [s[:, c * LANES:(c + 1) * LANES] for c in range(tq // LANES)]
            top = functools.reduce(jnp.maximum, chunks)
            m_old = m_ref[i]
            m_new = jnp.maximum(m_old, jnp.max(top, axis=1, keepdims=True))
            alpha = jnp.exp(m_old - m_new)
            ps = [jnp.exp(c - m_new) for c in chunks]
            l_ref[i] = alpha * l_ref[i] + functools.reduce(jnp.add, ps)
            p = jnp.concatenate(ps, axis=1).astype(BF16)
            acc_ref[i] = alpha * acc_ref[i] + jnp.dot(p, v, preferred_element_type=F32)
            m_ref[i] = m_new

    def full_block(j, carry):
        block(j, False)
        return carry

    lax.fori_loop(0, qi, full_block, 0)
    block(qi, True)
    lam = _lambda_value(lp_ref[...], lam_init)
    l1 = jnp.sum(l_ref[0], axis=1, keepdims=True)
    l2 = jnp.sum(l_ref[1], axis=1, keepdims=True)
    o = acc_ref[0] / l1 - lam * (acc_ref[1] / l2)
    o_ref[...] = _head_norm(o, nw_ref[...], lam_init).astype(BF16)


def _attention(q, k, v, lp, nw, lam_init, B, S):
    tq = min(512, S)
    nq = S // tq
    qspec = pl.BlockSpec((tq, ATT_V_DIM), lambda b, h, i: (b * nq + i, h))
    kspec = pl.BlockSpec((S, ATT_V_DIM), lambda b, h, i: (b, h))
    return pl.pallas_call(
        functools.partial(_attn_kernel, lam_init=lam_init),
        grid=(B, ATT_HEADS, nq),
        in_specs=[_full(lp.shape), _full(nw.shape), qspec, kspec, kspec],
        out_specs=qspec,
        out_shape=jax.ShapeDtypeStruct((B * S, ATT_COLS), BF16),
        scratch_shapes=[pltpu.VMEM((2, tq, LANES), F32), pltpu.VMEM((2, tq, LANES), F32),
                        pltpu.VMEM((2, tq, ATT_V_DIM), F32)],
        compiler_params=_cparams("parallel", "parallel", "arbitrary"),
        name="diff_attention",
    )(lp, nw, q, k, v)


def _decode_attn_kernel(pt_ref, lp_ref, nw_ref, q_ref, kn_ref, vn_ref, *refs, n_pages, lam_init):
    k_refs = refs[:n_pages]
    v_refs = refs[n_pages:2 * n_pages]
    o_ref = refs[2 * n_pages]
    R = 2 * ATT_HEADS
    page = k_refs[0].shape[2]
    r = lax.broadcasted_iota(jnp.int32, (R, ATT_COLS), 0)
    grp = lax.broadcasted_iota(jnp.int32, (R, ATT_COLS), 1) // ATT_QK_DIM
    target = jnp.where(r < ATT_HEADS, 2 * r, 2 * (r - ATT_HEADS) + 1)
    qf = jnp.where(grp == target, q_ref[0].astype(BF16).astype(F32), 0.0)
    qt = qf.astype(BF16)
    s = [jnp.dot(qt, k_ref[...].reshape(ATT_COLS, page).astype(BF16), preferred_element_type=F32)
         for k_ref in k_refs]
    s_new = jnp.sum(qf * kn_ref[0].astype(BF16).astype(F32), axis=1, keepdims=True)
    m = s_new
    for sj in s:
        m = jnp.maximum(m, jnp.max(sj, axis=1, keepdims=True))
    p = [jnp.exp(sj - m) for sj in s]
    p_new = jnp.exp(s_new - m)
    l = p_new
    for pj in p:
        l = l + jnp.sum(pj, axis=1, keepdims=True)
    inv_l = 1.0 / l
    lam = _lambda_value(lp_ref[...], lam_init)
    diff = lambda t: t - lam * pltpu.roll(t, ATT_HEADS, 0)
    a = [diff(pj * inv_l).astype(BF16) for pj in p]
    a_new = diff(jnp.broadcast_to(p_new * inv_l, (R, 128)))[:, 0:1].astype(BF16).astype(F32)
    row = lax.broadcasted_iota(jnp.int32, (R, ATT_V_DIM), 0)
    out = a_new * vn_ref[0].astype(BF16).astype(F32)
    for h in range(ATT_HEADS):
        acc = jnp.zeros((R, ATT_V_DIM), F32)
        for aj, v_ref in zip(a, v_refs):
            v_h = v_ref[pl.ds(h, page, stride=ATT_HEADS), :].astype(BF16)
            acc = acc + jnp.dot(aj, v_h, preferred_element_type=F32)
        out = out + jnp.where(row == h, acc, 0.0)
    o_ref[0] = _head_norm(out, nw_ref[...], lam_init)


def _decode_attention(pt, layer, q, k_new, v_new, cache_k, cache_v, lp, nw, lam_init):
    Bs, n_pages = pt.shape
    R = 2 * ATT_HEADS
    qspec = pl.BlockSpec((1, 1, ATT_COLS), lambda b, pt: (b, 0, 0))
    vspec = pl.BlockSpec((1, R, ATT_V_DIM), lambda b, pt: (b, 0, 0))
    kpages = [pl.BlockSpec((None, None) + cache_k.shape[2:], lambda b, pt, j=j: (layer, pt[b, j], 0, 0, 0))
              for j in range(n_pages)]
    vpages = [pl.BlockSpec((None, None) + cache_v.shape[2:], lambda b, pt, j=j: (layer, pt[b, j], 0, 0))
              for j in range(n_pages)]
    grid_spec = pltpu.PrefetchScalarGridSpec(
        num_scalar_prefetch=1,
        grid=(Bs,),
        in_specs=[pl.BlockSpec(lp.shape, lambda b, pt: (0, 0)), pl.BlockSpec(nw.shape, lambda b, pt: (0, 0)),
                  qspec, qspec, vspec] + kpages + vpages,
        out_specs=vspec,
    )
    return pl.pallas_call(
        functools.partial(_decode_attn_kernel, n_pages=n_pages, lam_init=lam_init),
        grid_spec=grid_spec,
        out_shape=jax.ShapeDtypeStruct((Bs, R, ATT_V_DIM), F32),
        compiler_params=_cparams("parallel"),
        name="decode_attention",
    )(pt, lp, nw, q, k_new, v_new, *([cache_k] * n_pages), *([cache_v] * n_pages))


def _proj_ln_kernel(*refs, n_in, alpha):
    a_refs = refs[:n_in]
    w_refs = refs[n_in:2 * n_in]
    x_ref, g_ref, b_ref, o_ref = refs[2 * n_in:]
    acc = alpha * x_ref[...]
    for a_ref, w_ref in zip(a_refs, w_refs):
        acc = acc + jnp.dot(a_ref[...].astype(BF16), w_ref[...], preferred_element_type=F32)
    o_ref[...] = _layer_norm(acc, g_ref[...], b_ref[...])


def _proj_ln(acts, ws, x, g, b, alpha, tm):
    T, D = x.shape
    row = lambda n: pl.BlockSpec((tm, n), lambda i: (i, 0))
    return pl.pallas_call(
        functools.partial(_proj_ln_kernel, n_in=len(acts), alpha=alpha),
        grid=(T // tm,),
        in_specs=[row(a.shape[1]) for a in acts] + [_full(w.shape) for w in ws] + [row(D), _full(g.shape), _full(b.shape)],
        out_specs=row(D),
        out_shape=jax.ShapeDtypeStruct((T, D), F32),
        compiler_params=_cparams("parallel"),
        name="proj_ln",
    )(*acts, *ws, x, g, b)


def _matmul_kernel(x_ref, w_ref, *o_refs):
    y = jnp.dot(x_ref[...].astype(BF16), w_ref[...], preferred_element_type=F32)
    for o_ref in o_refs:
        o_ref[...] = y.astype(o_ref.dtype)


def _matmul(x, w, out_dtypes, tm, tn):
    M, K = x.shape
    N = w.shape[1]
    ospec = pl.BlockSpec((tm, tn), lambda i, j: (i, j))
    return pl.pallas_call(
        _matmul_kernel,
        grid=(M // tm, N // tn),
        in_specs=[pl.BlockSpec((tm, K), lambda i, j: (i, 0)), pl.BlockSpec((K, tn), lambda i, j: (0, j))],
        out_specs=tuple(ospec for _ in out_dtypes),
        out_shape=tuple(jax.ShapeDtypeStruct((M, N), d) for d in out_dtypes),
        compiler_params=_cparams("parallel", "parallel"),
        name="matmul",
    )(x, w)


def _softmax_rows(s):
    m = jnp.max(s, axis=1, keepdims=True)
    p = jnp.exp(s - m)
    return p / jnp.sum(p, axis=1, keepdims=True)


def _cross_kernel(x_ref, wq_ref, mk_ref, mv_ref, wo_ref, g_ref, b_ref, o_ref, *, alpha):
    x = x_ref[...]
    D = x.shape[1]
    dh = D // MEM_HEADS
    q = jnp.dot(x.astype(BF16), wq_ref[...], preferred_element_type=F32).astype(BF16)
    outs = []
    for h in range(MEM_HEADS):
        sl = slice(h * dh, (h + 1) * dh)
        s = lax.dot_general(q[:, sl], mk_ref[:, sl], NT_DIMS, preferred_element_type=F32) * (dh ** -0.5)
        outs.append(jnp.dot(_softmax_rows(s).astype(BF16), mv_ref[:, sl], preferred_element_type=F32).astype(BF16))
    o = jnp.concatenate(outs, axis=1)
    y = alpha * x + jnp.dot(o, wo_ref[...], preferred_element_type=F32)
    o_ref[...] = _layer_norm(y, g_ref[...], b_ref[...])


def _cross_attention(x, wq, mk, mv, wo, g, b, alpha, B, S, n_mem, tm):
    T, D = x.shape
    nt = S // tm
    row = pl.BlockSpec((tm, D), lambda bi, i: (bi * nt + i, 0))
    mem = pl.BlockSpec((n_mem, D), lambda bi, i: (bi, 0))
    return pl.pallas_call(
        functools.partial(_cross_kernel, alpha=alpha),
        grid=(B, nt),
        in_specs=[row, _full(wq.shape), mem, mem, _full(wo.shape), _full(g.shape), _full(b.shape)],
        out_specs=row,
        out_shape=jax.ShapeDtypeStruct((T, D), F32),
        compiler_params=_cparams("parallel", "parallel"),
        name="cross_attention",
    )(x, wq, mk, mv, wo, g, b)


def _cross_decode_kernel(q_ref, mk_ref, mv_ref, o_ref):
    R = q_ref.shape[1] // 2
    rows_kv = mk_ref.shape[0]
    dh = 2 * LANES
    part = lax.dot_general(q_ref[0].astype(BF16), mk_ref[...].astype(BF16), NT_DIMS,
                           preferred_element_type=F32)
    s = (part[:R] + pltpu.roll(part[R:], rows_kv - MEM_HEADS, 1)) * (dh ** -0.5)
    own = (lax.broadcasted_iota(jnp.int32, (R, rows_kv), 1) & (R - 1)) == lax.broadcasted_iota(jnp.int32, (R, rows_kv), 0)
    p = _softmax_rows(jnp.where(own, s, -jnp.inf))
    a = jnp.concatenate([p, pltpu.roll(p, MEM_HEADS, 1)], axis=0).astype(BF16)
    o_ref[0] = jnp.dot(a, mv_ref[...].astype(BF16), preferred_element_type=F32)


def _cross_decode(q, mem_k, mem_v, layer):
    Bs, R2, _ = q.shape
    tok = pl.BlockSpec((1, R2, LANES), lambda b: (b, 0, 0))
    mem = pl.BlockSpec((None, None) + mem_k.shape[2:], lambda b: (layer, b, 0, 0))
    return pl.pallas_call(
        _cross_decode_kernel,
        grid=(Bs,),
        in_specs=[tok, mem, mem],
        out_specs=tok,
        out_shape=jax.ShapeDtypeStruct((Bs, R2, LANES), F32),
        compiler_params=_cparams("parallel"),
        name="cross_decode",
    )(q, mem_k, mem_v)


def _ffn_kernel(x_ref, wg_ref, wu_ref, wd_ref, g_ref, b_ref, o_ref, acc_ref, *, alpha):
    f = pl.program_id(1)
    x = x_ref[...]
    xb = x.astype(BF16)

    @pl.when(f == 0)
    def _():
        acc_ref[...] = alpha * x

    gate = jnp.dot(xb, wg_ref[...], preferred_element_type=F32)
    up = jnp.dot(xb, wu_ref[...], preferred_element_type=F32)
    acc_ref[...] += jnp.dot((_silu(gate) * up).astype(BF16), wd_ref[...], preferred_element_type=F32)

    @pl.when(f == pl.num_programs(1) - 1)
    def _():
        o_ref[...] = _layer_norm(acc_ref[...], g_ref[...], b_ref[...])


def _ffn(x, w_gu, w_down, g, b, alpha, tm, tf):
    T, D = x.shape
    F = w_down.shape[0]
    nf = F // tf
    row = pl.BlockSpec((tm, D), lambda i, f: (i, 0))
    return pl.pallas_call(
        functools.partial(_ffn_kernel, alpha=alpha),
        grid=(T // tm, nf),
        in_specs=[row, pl.BlockSpec((D, tf), lambda i, f: (0, f)), pl.BlockSpec((D, tf), lambda i, f: (0, nf + f)),
                  pl.BlockSpec((tf, D), lambda i, f: (f, 0)), _full(g.shape), _full(b.shape)],
        out_specs=row,
        out_shape=jax.ShapeDtypeStruct((T, D), F32),
        scratch_shapes=[pltpu.VMEM((tm, D), F32)],
        compiler_params=_cparams("parallel", "arbitrary"),
        name="ffn",
    )(x, w_gu, w_gu, w_down, g, b)


def _router_kernel(x_ref, wr_ref, gate_ref):
    logits = jnp.dot(x_ref[...].astype(BF16), wr_ref[...], preferred_element_type=F32)
    lane = lax.broadcasted_iota(jnp.int32, logits.shape, 1).astype(F32)
    none = float(N_EXPERTS)
    m1 = jnp.max(logits, axis=1, keepdims=True)
    i1 = jnp.min(jnp.where(logits == m1, lane, none), axis=1, keepdims=True)
    rest = jnp.where(lane == i1, -jnp.inf, logits)
    m2 = jnp.max(rest, axis=1, keepdims=True)
    i2 = jnp.min(jnp.where(rest == m2, lane, none), axis=1, keepdims=True)
    e = jnp.exp(m2 - m1)
    den = 1.0 + e
    gate_ref[...] = jnp.where(lane == i1, 1.0 / den, 0.0) + jnp.where(lane == i2, e / den, 0.0)


def _router(x, wr, tm):
    T, D = x.shape
    return pl.pallas_call(
        _router_kernel,
        grid=(T // tm,),
        in_specs=[pl.BlockSpec((tm, D), lambda i: (i, 0)), _full(wr.shape)],
        out_specs=pl.BlockSpec((tm, N_EXPERTS), lambda i: (i, 0)),
        out_shape=jax.ShapeDtypeStruct((T, N_EXPERTS), F32),
        compiler_params=_cparams("parallel"),
        name="router",
    )(x, wr)


def _moe_kernel(x_ref, gate_ref, wg_ref, wu_ref, wd_ref, g_ref, b_ref, o_ref, acc_ref, *, alpha):
    e = pl.program_id(1)
    f = pl.program_id(2)
    x = x_ref[...]
    xb = x.astype(BF16)

    @pl.when(jnp.logical_and(e == 0, f == 0))
    def _():
        acc_ref[...] = alpha * x

    gates = gate_ref[...]
    lane = lax.broadcasted_iota(jnp.int32, gates.shape, 1)
    ge = jnp.sum(jnp.where(lane == e, gates, 0.0), axis=1, keepdims=True)
    gate = jnp.dot(xb, wg_ref[0], preferred_element_type=F32)
    up = jnp.dot(xb, wu_ref[0], preferred_element_type=F32)
    acc_ref[...] += jnp.dot((_silu(gate) * up).astype(BF16), wd_ref[0], preferred_element_type=F32) * ge

    @pl.when(jnp.logical_and(e == pl.num_programs(1) - 1, f == pl.num_programs(2) - 1))
    def _():
        o_ref[...] = _layer_norm(acc_ref[...], g_ref[...], b_ref[...])


def _moe(x, gates, w_gu, w_down, g, b, alpha, tm, tf):
    T, D = x.shape
    E, F, _ = w_down.shape
    nf = F // tf
    row = pl.BlockSpec((tm, D), lambda i, e, f: (i, 0))
    return pl.pallas_call(
        functools.partial(_moe_kernel, alpha=alpha),
        grid=(T // tm, E, nf),
        in_specs=[row, pl.BlockSpec((tm, E), lambda i, e, f: (i, 0)),
                  pl.BlockSpec((1, D, tf), lambda i, e, f: (e, 0, f)),
                  pl.BlockSpec((1, D, tf), lambda i, e, f: (e, 0, nf + f)),
                  pl.BlockSpec((1, tf, D), lambda i, e, f: (e, f, 0)), _full(g.shape), _full(b.shape)],
        out_specs=row,
        out_shape=jax.ShapeDtypeStruct((T, D), F32),
        scratch_shapes=[pltpu.VMEM((tm, D), F32)],
        compiler_params=_cparams("parallel", "arbitrary", "arbitrary"),
        name="moe",
    )(x, gates, w_gu, w_gu, w_down, g, b)


def _rope_tables(pos):
    half = ATT_QK_DIM // 2
    inv = ROPE_THETA ** (-jnp.arange(0, ATT_QK_DIM, 2, dtype=F32) / ATT_QK_DIM)
    ang = pos.astype(F32)[:, None] * inv[None, :]
    cos, sin = jnp.cos(ang), jnp.sin(ang)
    return jnp.concatenate([cos] * 4, axis=1), jnp.concatenate([-sin, sin, -sin, sin], axis=1)


def _row_tile(n, target):
    t = min(n, target)
    while n % t:
        t //= 2
    return t


def kernel(x_prompt, x_sample, cache_attn_k, cache_attn_v, cache_mem_k, cache_mem_v, state_conv, state_ssm,
           page_table, mem_prompt, ln_g, ln_b, w_in, conv_w, conv_b, dt_bias, a_log, d_skip, ssm_norm_w,
           lambda_params, attn_norm_w, w_out, w_cq, w_ckv, w_co, w_ffn_gu, w_ffn_down, w_router, w_exp_gu,
           w_exp_down):
    B, S, D = x_prompt.shape
    Bs = x_sample.shape[0]
    depth = w_in.shape[0]
    n_phys, page = cache_attn_k.shape[1], cache_attn_k.shape[2]
    n_pages = page_table.shape[1]
    past_len = n_pages * page
    n_mem = mem_prompt.shape[1]
    alpha = (2 * depth) ** 0.25
    assert S % SSD_CHUNK == 0 and x_sample.shape[1] == 1

    tm_p = _row_tile(S, 512)
    tm_s = _row_tile(Bs, 128)
    cos_p, sin_p = _rope_tables(jnp.arange(S, dtype=jnp.int32))
    cos_s, sin_s = _rope_tables(jnp.full((tm_s,), past_len, jnp.int32))

    hp = x_prompt.reshape(B * S, D)
    hs = x_sample.reshape(Bs, D)
    mem = mem_prompt.reshape(B * n_mem, D)
    cache_k = cache_attn_k.transpose(0, 1, 3, 4, 2)
    cache_v = cache_attn_v.reshape(depth, n_phys, page * ATT_HEADS, ATT_V_DIM)
    dh = D // MEM_HEADS
    mem_rows = lambda t: (t.reshape(depth, Bs, n_mem, MEM_HEADS, dh // LANES, LANES).swapaxes(3, 4)
                          .reshape(depth, Bs, n_mem * dh // LANES * MEM_HEADS, LANES))
    cmem_k, cmem_v = mem_rows(cache_mem_k), mem_rows(cache_mem_v)
    cstate = state_conv.reshape(depth, Bs, (CONV_WIDTH - 1) * CONV_DIM)
    pad_heads = lambda t: jnp.pad(t, [(0, 0)] * (t.ndim - 2) + [(0, 2 * ATT_HEADS - t.shape[-2]), (0, 0)])

    splits = (ATT_COLS, 2 * ATT_COLS, 3 * ATT_COLS, 3 * ATT_COLS + SSM_INNER, 3 * ATT_COLS + SSM_INNER + CONV_DIM)
    outs = {n: [] for n in ("kp", "vp", "mkp", "mvp", "cp", "sp", "ks", "vs", "cs", "ss")}
    for l in range(depth):
        lam_init = 0.8 - 0.6 * math.exp(-0.3 * l)
        wl = w_in[l].astype(BF16)
        wq, wk, wv, wz, wxbc, wdt = (wl[:, a:b] for a, b in zip((0,) + splits, splits + (wl.shape[1],)))
        w_proj = (wq, wk, wv, wz, wxbc, wdt, wdt.T)
        sw = (conv_w[l], conv_b[l][None, :], dt_bias[l][None, :], dt_bias[l][:, None], a_log[l][None, :],
              a_log[l][:, None], d_skip[l][None, :], ssm_norm_w[l][None, :])
        lp = lambda_params[l]
        nw = attn_norm_w[l][None, :]
        wo = w_out[l].astype(BF16)
        wo_att, wo_ssm = wo[:ATT_COLS], wo[ATT_COLS:]
        wcq = w_cq[l].astype(BF16)
        wckv = w_ckv[l].astype(BF16)
        wco = w_co[l].astype(BF16)
        g0, g1, g2 = (ln_g[l, i][None, :] for i in range(3))
        b0, b1, b2 = (ln_b[l, i][None, :] for i in range(3))

        q, kf, kb, vf, vb, z, xbc, dt, dtT = _in_proj(hp, cos_p, sin_p, w_proj, tm_p, S // tm_p)
        att = _attention(q, kb, vb, lp, nw, lam_init, B, S)
        y, conv_new, h_last = _ssd(xbc, z, dt, dtT, sw, B, S)
        hp = _proj_ln((att, y), (wo_att, wo_ssm), hp, g0, b0, alpha, tm_p)
        outs["kp"].append(kf.reshape(B, S, 2 * ATT_HEADS, ATT_QK_DIM))
        outs["vp"].append(vf.reshape(B, S, ATT_HEADS, ATT_V_DIM))
        outs["cp"].append(conv_new)
        outs["sp"].append(h_last.reshape(B, SSM_HEADS, SSM_HEAD_DIM, SSM_STATE))
        mkv_f, mkv_b = _matmul(mem, wckv, (F32, BF16), _row_tile(B * n_mem, 512), D)
        outs["mkp"].append(mkv_f[:, :D].reshape(B, n_mem, MEM_HEADS, D // MEM_HEADS))
        outs["mvp"].append(mkv_f[:, D:].reshape(B, n_mem, MEM_HEADS, D // MEM_HEADS))
        hp = _cross_attention(hp, wcq, mkv_b[:, :D], mkv_b[:, D:], wco, g1, b1, alpha, B, S, n_mem, tm_p)

        q, kf, _, vf, _, z, xbc, dt, _ = _in_proj(hs, cos_s, sin_s, w_proj, tm_s, 1)
        att = _decode_attention(page_table, l, q.astype(F32).reshape(Bs, 1, ATT_COLS), kf.reshape(Bs, 1, ATT_COLS),
                                pad_heads(vf.reshape(Bs, ATT_HEADS, ATT_V_DIM)), cache_k, cache_v, lp, nw, lam_init)
        att = att[:, :ATT_HEADS].reshape(Bs, ATT_COLS)
        y, conv_new, h_new = _ssd_step(xbc, z, dt, cstate, state_ssm, l, sw)
        hs = _proj_ln((att, y), (wo_att, wo_ssm), hs, g0, b0, alpha, tm_s)
        outs["ks"].append(kf.reshape(Bs, 1, 2 * ATT_HEADS, ATT_QK_DIM))
        outs["vs"].append(vf.reshape(Bs, 1, ATT_HEADS, ATT_V_DIM))
        outs["cs"].append(conv_new.reshape(Bs, CONV_WIDTH - 1, CONV_DIM))
        outs["ss"].append(h_new)
        (qc,) = _matmul(hs, wcq, (F32,), tm_s, D)
        q_rows = pad_heads(qc.reshape(Bs, MEM_HEADS, dh // LANES, LANES).swapaxes(1, 2))
        oc = _cross_decode(q_rows.reshape(Bs, -1, LANES), cmem_k, cmem_v, l)
        oc = oc.reshape(Bs, dh // LANES, 2 * ATT_HEADS, LANES)[:, :, :MEM_HEADS].swapaxes(1, 2).reshape(Bs, D)
        hs = _proj_ln((oc,), (wco,), hs, g1, b1, alpha, tm_s)

        if l % 2 == 0:
            wgu = w_ffn_gu[l // 2].astype(BF16)
            wd = w_ffn_down[l // 2].astype(BF16)
            tf = wd.shape[0] // 2
            hp = _ffn(hp, wgu, wd, g2, b2, alpha, tm_p, tf)
            hs = _ffn(hs, wgu, wd, g2, b2, alpha, tm_s, tf)
        else:
            wr = w_router[l // 2].astype(BF16)
            wgu = w_exp_gu[l // 2].astype(BF16)
            wd = w_exp_down[l // 2].astype(BF16)
            tf = wd.shape[1] // 2
            hp = _moe(hp, _router(hp, wr, tm_p), wgu, wd, g2, b2, alpha, tm_p, tf)
            hs = _moe(hs, _router(hs, wr, tm_s), wgu, wd, g2, b2, alpha, tm_s, tf)

    st = {n: jnp.stack(v) for n, v in outs.items()}
    return (hp.reshape(B, S, D), hs.reshape(Bs, 1, D), st["kp"], st["vp"], st["mkp"], st["mvp"], st["cp"],
            st["sp"], st["ks"], st["vs"], st["cs"], st["ss"])
```

```python
import functools
import math

import jax
import jax.numpy as jnp
from jax import lax
from jax.experimental import pallas as pl
from jax.experimental.pallas import tpu as pltpu

F32 = jnp.float32
BF16 = jnp.bfloat16

ATT_HEADS = 4
ATT_V_DIM = 128
ATT_QK_DIM = 64
ATT_COLS = 512
SSM_INNER = 512
SSM_HEAD_DIM = 64
SSM_HEADS = 8
SSM_GROUPS = 2
SSM_STATE = 128
CONV_WIDTH = 4
CONV_DIM = 1024
SSD_CHUNK = 128
MEM_HEADS = 4
N_EXPERTS = 8
ROPE_THETA = 10000.0
LN_EPS = 1e-5
NORM_EPS = 1e-5
QK_SCALE = ATT_QK_DIM ** -0.5

LANES = 128
MOE_ROW_TILE = 512
VMEM_LIMIT_BYTES = 56 * 1024 * 1024
NT_DIMS = (((1,), (1,)), ((), ()))


def _cparams(*sem):
    return pltpu.CompilerParams(dimension_semantics=sem, vmem_limit_bytes=VMEM_LIMIT_BYTES)


def _silu(x):
    return x / (1.0 + jnp.exp(-x))


def _softplus(x):
    return jnp.maximum(x, 0.0) + jnp.log1p(jnp.exp(-jnp.abs(x)))


def _layer_norm(y, g, b):
    mu = jnp.mean(y, axis=-1, keepdims=True)
    yc = y - mu
    var = jnp.mean(yc * yc, axis=-1, keepdims=True)
    return yc * lax.rsqrt(var + LN_EPS) * g + b


def _full(shape):
    return pl.BlockSpec(shape, lambda *_: (0,) * len(shape))


def _in_proj_kernel(x_ref, cos_ref, sin_ref, wq_ref, wk_ref, wv_ref, wz_ref, wxbc_ref, wdt_ref, wdtT_ref,
                    q_ref, kf_ref, kb_ref, vf_ref, vb_ref, z_ref, xbc_ref, dt_ref, dtT_ref):
    tm = x_ref.shape[0]
    xb = x_ref[...].astype(BF16)
    cos = jnp.concatenate([cos_ref[...]] * 4, axis=1)
    sin = jnp.concatenate([sin_ref[...]] * 4, axis=1)
    lane = lax.broadcasted_iota(jnp.int32, (tm, ATT_COLS), 1)
    first_half = (lane & (ATT_QK_DIM - 1)) < (ATT_QK_DIM // 2)

    def rope(t):
        partner = jnp.where(first_half, pltpu.roll(t, ATT_COLS - ATT_QK_DIM // 2, 1),
                            pltpu.roll(t, ATT_QK_DIM // 2, 1))
        return t * cos + partner * sin

    q = rope(jnp.dot(xb, wq_ref[...], preferred_element_type=F32))
    q_ref[...] = (q * QK_SCALE).astype(BF16)
    k = rope(jnp.dot(xb, wk_ref[...], preferred_element_type=F32))
    kf_ref[...] = k
    kb_ref[...] = k.astype(BF16)
    v = jnp.dot(xb, wv_ref[...], preferred_element_type=F32)
    vf_ref[...] = v
    vb_ref[...] = v.astype(BF16)
    z_ref[...] = jnp.dot(xb, wz_ref[...], preferred_element_type=F32)
    xbc_ref[...] = jnp.dot(xb, wxbc_ref[...], preferred_element_type=F32)
    dt_ref[...] = jnp.dot(xb, wdt_ref[...], preferred_element_type=F32)
    dtT_ref[...] = lax.dot_general(wdtT_ref[...], xb, NT_DIMS, preferred_element_type=F32)


def _in_proj(x, cos_t, sin_t, w, tm, n_pos_blocks):
    T, D = x.shape
    wq, wk, wv, wz, wxbc, wdt, wdtT = w
    row = lambda n: pl.BlockSpec((tm, n), lambda i: (i, 0))
    pos = pl.BlockSpec((tm, 128), lambda i: (i % n_pos_blocks, 0))
    out_shape = (
        jax.ShapeDtypeStruct((T, ATT_COLS), BF16),
        jax.ShapeDtypeStruct((T, ATT_COLS), F32),
        jax.ShapeDtypeStruct((T, ATT_COLS), BF16),
        jax.ShapeDtypeStruct((T, ATT_COLS), F32),
        jax.ShapeDtypeStruct((T, ATT_COLS), BF16),
        jax.ShapeDtypeStruct((T, SSM_INNER), F32),
        jax.ShapeDtypeStruct((T, CONV_DIM), F32),
        jax.ShapeDtypeStruct((T, SSM_HEADS), F32),
        jax.ShapeDtypeStruct((SSM_HEADS, T), F32),
    )
    out_specs = (row(ATT_COLS), row(ATT_COLS), row(ATT_COLS), row(ATT_COLS), row(ATT_COLS),
                 row(SSM_INNER), row(CONV_DIM), row(SSM_HEADS),
                 pl.BlockSpec((SSM_HEADS, tm), lambda i: (0, i)))
    return pl.pallas_call(
        _in_proj_kernel,
        grid=(T // tm,),
        in_specs=[row(D), pos, pos, _full(wq.shape), _full(wk.shape), _full(wv.shape), _full(wz.shape),
                  _full(wxbc.shape), _full(wdt.shape), _full(wdtT.shape)],
        out_specs=out_specs,
        out_shape=out_shape,
        compiler_params=_cparams("parallel"),
        name="in_proj",
    )(x, cos_t, sin_t, wq, wk, wv, wz, wxbc, wdt, wdtT)


def _gated_group_norm(y, z, norm_w):
    y = y * _silu(z)
    half = SSM_INNER // SSM_GROUPS
    y2 = y * y
    ms0 = jnp.mean(y2[:, :half], axis=-1, keepdims=True)
    ms1 = jnp.mean(y2[:, half:], axis=-1, keepdims=True)
    lane = lax.broadcasted_iota(jnp.int32, y.shape, 1)
    scale = jnp.where(lane < half, lax.rsqrt(ms0 + NORM_EPS), lax.rsqrt(ms1 + NORM_EPS))
    return y * scale * norm_w


def _ssd_kernel(xbc_ref, z_ref, dt_ref, dtT_ref, convw_ref, convb_ref, dtb_ref, dtbT_ref, alog_ref, alogT_ref,
                dskip_ref, normw_ref, y_ref, conv_out_ref, state_out_ref, ext_ref, h_ref):
    Q = SSD_CHUNK
    c = pl.program_id(1)
    last = pl.num_programs(1) - 1

    @pl.when(c == 0)
    def _():
        ext_ref[0:8, :] = jnp.zeros((8, CONV_DIM), F32)
        h_ref[...] = jnp.zeros(h_ref.shape, F32)

    ext_ref[8:8 + Q, :] = xbc_ref[...]
    conv = convb_ref[...]
    for i in range(CONV_WIDTH):
        conv = conv + ext_ref[5 + i:5 + i + Q, :] * convw_ref[i:i + 1, :]
    ext_ref[0:8, :] = xbc_ref[Q - 8:Q, :]
    conv = _silu(conv)
    xs = conv[:, :SSM_INNER]
    bmat = conv[:, SSM_INNER:SSM_INNER + SSM_GROUPS * SSM_STATE]
    cmat = conv[:, SSM_INNER + SSM_GROUPS * SSM_STATE:]

    dt_col = _softplus(dt_ref[...] + dtb_ref[...])
    a_col = dt_col * (-jnp.exp(alog_ref[...]))
    a_row = _softplus(dtT_ref[...] + dtbT_ref[...]) * (-jnp.exp(alogT_ref[...]))
    ri = lax.broadcasted_iota(jnp.int32, (Q, Q), 0)
    ci = lax.broadcasted_iota(jnp.int32, (Q, Q), 1)
    causal = ci <= ri
    tril = causal.astype(F32)
    triu = (ri <= ci).astype(F32)
    acs_col = jnp.dot(tril, a_col, preferred_element_type=F32, precision=lax.Precision.HIGHEST)
    acs_row = jnp.dot(a_row, triu, preferred_element_type=F32, precision=lax.Precision.HIGHEST)

    lo = ci < SSM_HEAD_DIM
    top = ri < SSM_HEAD_DIM
    pairs = []
    for k in range(SSM_HEADS // 2):
        h0, h1 = 2 * k, 2 * k + 1
        g = k // (SSM_HEADS // 2 // SSM_GROUPS)
        b_g = bmat[:, g * SSM_STATE:(g + 1) * SSM_STATE].astype(BF16)
        c_g = cmat[:, g * SSM_STATE:(g + 1) * SSM_STATE].astype(BF16)
        cb = lax.dot_general(c_g, b_g, NT_DIMS, preferred_element_type=F32)
        xs_pair = xs[:, 2 * SSM_HEAD_DIM * k:2 * SSM_HEAD_DIM * (k + 1)]
        dtc = jnp.where(lo, dt_col[:, h0:h0 + 1], dt_col[:, h1:h1 + 1])
        xdt = xs_pair * dtc
        acol = jnp.where(lo, acs_col[:, h0:h0 + 1], acs_col[:, h1:h1 + 1])
        alast0 = acs_row[h0:h0 + 1, Q - 1:Q]
        alast1 = acs_row[h1:h1 + 1, Q - 1:Q]
        alast = jnp.where(lo[0:1, :], alast0, alast1)
        xw = xdt * jnp.exp(alast - acol)
        y_pair = jnp.where(lo, dskip_ref[0:1, h0:h0 + 1], dskip_ref[0:1, h1:h1 + 1]) * xs_pair
        for h, sel in ((h0, lo), (h1, jnp.logical_not(lo))):
            seg = acs_col[:, h:h + 1] - acs_row[h:h + 1, :]
            decay = jnp.exp(jnp.where(causal, seg, -jnp.inf))
            m = (cb * decay).astype(BF16)
            y_pair = y_pair + jnp.dot(m, jnp.where(sel, xdt, 0.0).astype(BF16), preferred_element_type=F32)
        states = jnp.dot(xw.T.astype(BF16), b_g, preferred_element_type=F32)
        h_prev = h_ref[k]
        y_off = lax.dot_general(c_g, h_prev.astype(BF16), NT_DIMS, preferred_element_type=F32)
        y_pair = y_pair + y_off * jnp.exp(acol)
        dec = jnp.where(top[:, 0:1], jnp.exp(alast0), jnp.exp(alast1))
        h_ref[k] = h_prev * dec + states
        pairs.append(y_pair)
    y = jnp.concatenate(pairs, axis=1)
    y_ref[...] = _gated_group_norm(y, z_ref[...], normw_ref[...]).astype(BF16)

    @pl.when(c == last)
    def _():
        conv_out_ref[0] = xbc_ref[Q - (CONV_WIDTH - 1):Q, :]
        state_out_ref[0] = h_ref[...]


def _ssd(xbc, z, dt, dtT, sw, B, S):
    convw, convb, dtb, dtbT, alog, alogT, dskip, normw = sw
    nc = S // SSD_CHUNK
    Q = SSD_CHUNK
    rowb = lambda n: pl.BlockSpec((Q, n), lambda b, c: (b * nc + c, 0))
    npair = SSM_HEADS // 2
    return pl.pallas_call(
        _ssd_kernel,
        grid=(B, nc),
        in_specs=[rowb(CONV_DIM), rowb(SSM_INNER), rowb(SSM_HEADS),
                  pl.BlockSpec((SSM_HEADS, Q), lambda b, c: (0, b * nc + c)),
                  _full(convw.shape), _full(convb.shape), _full(dtb.shape), _full(dtbT.shape),
                  _full(alog.shape), _full(alogT.shape), _full(dskip.shape), _full(normw.shape)],
        out_specs=(rowb(SSM_INNER),
                   pl.BlockSpec((1, CONV_WIDTH - 1, CONV_DIM), lambda b, c: (b, 0, 0)),
                   pl.BlockSpec((1, npair, 2 * SSM_HEAD_DIM, SSM_STATE), lambda b, c: (b, 0, 0, 0))),
        out_shape=(jax.ShapeDtypeStruct((B * S, SSM_INNER), BF16),
                   jax.ShapeDtypeStruct((B, CONV_WIDTH - 1, CONV_DIM), F32),
                   jax.ShapeDtypeStruct((B, npair, 2 * SSM_HEAD_DIM, SSM_STATE), F32)),
        scratch_shapes=[pltpu.VMEM((Q + 8, CONV_DIM), F32),
                        pltpu.VMEM((npair, 2 * SSM_HEAD_DIM, SSM_STATE), F32)],
        compiler_params=_cparams("parallel", "arbitrary"),
        name="ssd_scan",
    )(xbc, z, dt, dtT, convw, convb, dtb, dtbT, alog, alogT, dskip, normw)


def _ssd_step_kernel(xbc_ref, z_ref, dt_ref, cstate_ref, hstate_ref, convw_ref, convb_ref, dtb_ref, alog_ref,
                     dskip_ref, normw_ref, y_ref, cout_ref, hout_ref):
    bt = xbc_ref.shape[0]
    P, N = SSM_HEAD_DIM, SSM_STATE
    new = xbc_ref[...]
    conv = convb_ref[...] + new * convw_ref[CONV_WIDTH - 1:CONV_WIDTH, :]
    for i in range(CONV_WIDTH - 1):
        conv = conv + cstate_ref[:, i * CONV_DIM:(i + 1) * CONV_DIM] * convw_ref[i:i + 1, :]
    cout_ref[:, 0:(CONV_WIDTH - 2) * CONV_DIM] = cstate_ref[:, CONV_DIM:(CONV_WIDTH - 1) * CONV_DIM]
    cout_ref[:, (CONV_WIDTH - 2) * CONV_DIM:] = new
    conv = _silu(conv)
    xs = conv[:, :SSM_INNER]
    bmat = conv[:, SSM_INNER:SSM_INNER + SSM_GROUPS * N]
    cmat = conv[:, SSM_INNER + SSM_GROUPS * N:]
    dt = _softplus(dt_ref[...] + dtb_ref[...])
    dec = jnp.exp(dt * (-jnp.exp(alog_ref[...])))
    H = SSM_HEADS
    HP = H * P
    exact = lax.Precision.HIGHEST
    eye_n = (lax.broadcasted_iota(jnp.int32, (N, N), 0) == lax.broadcasted_iota(jnp.int32, (N, N), 1)).astype(F32)
    x_t = jnp.concatenate(
        [lax.dot_general(eye_n, xs[:, c * N:(c + 1) * N], NT_DIMS, preferred_element_type=F32, precision=exact)
         for c in range(HP // N)], axis=0)
    rep = (lax.broadcasted_iota(jnp.int32, (HP, H), 0) // P == lax.broadcasted_iota(jnp.int32, (HP, H), 1)).astype(F32)
    per_head = jnp.concatenate([dt, dec, jnp.broadcast_to(dskip_ref[...], (8, H))], axis=0)
    cols = lax.dot_general(rep, per_head, NT_DIMS, preferred_element_type=F32, precision=exact)
    d_col = cols[:, 2 * bt:2 * bt + 1]
    first_group = lax.broadcasted_iota(jnp.int32, (HP, 1), 0) < HP // SSM_GROUPS
    lane_b = lax.broadcasted_iota(jnp.int32, (HP, bt), 1)
    y_t = jnp.zeros((HP, bt), F32)
    for b in range(bt):
        x_col = x_t[:, b:b + 1]
        b_sel = jnp.where(first_group, bmat[b:b + 1, 0:N], bmat[b:b + 1, N:2 * N])
        h_new = (hstate_ref[b].reshape(HP, N) * cols[:, bt + b:bt + b + 1]
                 + (x_col * cols[:, b:b + 1]) * b_sel)
        hout_ref[b] = h_new.reshape(H, P, N)
        c2 = jnp.concatenate([cmat[b:b + 1, 0:N], cmat[b:b + 1, N:2 * N], jnp.zeros((6, N), F32)], axis=0)
        yb = lax.dot_general(h_new.astype(BF16), c2.astype(BF16), NT_DIMS, preferred_element_type=F32)
        y_col = jnp.where(first_group, yb[:, 0:1], yb[:, 1:2]) + d_col * x_col
        y_t = jnp.where(lane_b == b, y_col, y_t)
    eye_b = (lax.broadcasted_iota(jnp.int32, (bt, bt), 0) == lax.broadcasted_iota(jnp.int32, (bt, bt), 1)).astype(F32)
    y = lax.dot_general(eye_b, y_t, NT_DIMS, preferred_element_type=F32, precision=exact)
    y_ref[...] = _gated_group_norm(y, z_ref[...], normw_ref[...]).astype(BF16)


def _ssd_step(xbc, z, dt, cstate, hstate, layer, sw, bt=8):
    convw, convb, dtb, _, alog, _, dskip, normw = sw
    Bs = xbc.shape[0]
    cw = (CONV_WIDTH - 1) * CONV_DIM
    rowb = lambda n: pl.BlockSpec((bt, n), lambda i: (i, 0))
    hshape = (bt, SSM_HEADS, SSM_HEAD_DIM, SSM_STATE)
    return pl.pallas_call(
        _ssd_step_kernel,
        grid=(Bs // bt,),
        in_specs=[rowb(CONV_DIM), rowb(SSM_INNER), rowb(SSM_HEADS),
                  pl.BlockSpec((None, bt, cw), lambda i: (layer, i, 0)),
                  pl.BlockSpec((None,) + hshape, lambda i: (layer, i, 0, 0, 0)),
                  _full(convw.shape), _full(convb.shape), _full(dtb.shape), _full(alog.shape),
                  _full(dskip.shape), _full(normw.shape)],
        out_specs=(rowb(SSM_INNER), rowb(cw), pl.BlockSpec(hshape, lambda i: (i, 0, 0, 0))),
        out_shape=(jax.ShapeDtypeStruct((Bs, SSM_INNER), BF16),
                   jax.ShapeDtypeStruct((Bs, cw), F32),
                   jax.ShapeDtypeStruct(hstate.shape[1:], F32)),
        compiler_params=_cparams("parallel"),
        name="ssd_step",
    )(xbc, z, dt, cstate, hstate, convw, convb, dtb, alog, dskip, normw)


def _lambda_value(lp, lam_init):
    d1 = jnp.sum(lp[0:1, :] * lp[1:2, :], axis=1, keepdims=True)
    d2 = jnp.sum(lp[2:3, :] * lp[3:4, :], axis=1, keepdims=True)
    return jnp.exp(d1) - jnp.exp(d2) + lam_init


def _head_norm(o, nw, lam_init):
    return o * lax.rsqrt(jnp.mean(o * o, axis=-1, keepdims=True) + NORM_EPS) * nw * (1.0 - lam_init)


def _attn_kernel(lp_ref, nw_ref, q_ref, k_ref, v_ref, o_ref, m_ref, l_ref, acc_ref, *, lam_init):
    tq = q_ref.shape[0]
    qi = pl.program_id(2)
    q = q_ref[...]
    lane = lax.broadcasted_iota(jnp.int32, q.shape, 1)
    zero = jnp.zeros_like(q)
    q_maps = (jnp.where(lane < ATT_QK_DIM, q, zero), jnp.where(lane >= ATT_QK_DIM, q, zero))
    m_ref[...] = jnp.full(m_ref.shape, -jnp.inf, F32)
    l_ref[...] = jnp.zeros(l_ref.shape, F32)
    acc_ref[...] = jnp.zeros(acc_ref.shape, F32)
    below_diag = (lax.broadcasted_iota(jnp.int32, (tq, tq), 1) <= lax.broadcasted_iota(jnp.int32, (tq, tq), 0))

    def block(j, diagonal):
        start = pl.multiple_of(j * tq, tq)
        k = k_ref[pl.ds(start, tq), :]
        v = v_ref[pl.ds(start, tq), :]
        for i in range(2):
            s = lax.dot_general(q_maps[i], k, NT_DIMS, preferred_element_type=F32)
            if diagonal:
                s = jnp.where(below_diag, s, -jnp.inf)
            chunks = [s[:, c * LANES:(c + 1) * LANES] for c in range(tq // LANES)]
            top = functools.reduce(jnp.maximum, chunks)
            m_old = m_ref[i]
            m_new = jnp.maximum(m_old, jnp.max(top, axis=1, keepdims=True))
            alpha = jnp.exp(m_old - m_new)
            ps = [jnp.exp(c - m_new) for c in chunks]
            l_ref[i] = alpha * l_ref[i] + functools.reduce(jnp.add, ps)
            p = jnp.concatenate(ps, axis=1).astype(BF16)
            acc_ref[i] = alpha * acc_ref[i] + jnp.dot(p, v, preferred_element_type=F32)
            m_ref[i] = m_new

    def full_block(j, carry):
        block(j, False)
        return carry

    lax.fori_loop(0, qi, full_block, 0)
    block(qi, True)
    lam = _lambda_value(lp_ref[...], lam_init)
    l1 = jnp.sum(l_ref[0], axis=1, keepdims=True)
    l2 = jnp.sum(l_ref[1], axis=1, keepdims=True)
    o = acc_ref[0] / l1 - lam * (acc_ref[1] / l2)
    o_ref[...] = _head_norm(o, nw_ref[...], lam_init).astype(BF16)


def _attention(q, k, v, lp, nw, lam_init, B, S):
    tq = min(512, S)
    nq = S // tq
    qspec = pl.BlockSpec((tq, ATT_V_DIM), lambda b, h, i: (b * nq + i, h))
    kspec = pl.BlockSpec((S, ATT_V_DIM), lambda b, h, i: (b, h))
    return pl.pallas_call(
        functools.partial(_attn_kernel, lam_init=lam_init),
        grid=(B, ATT_HEADS, nq),
        in_specs=[_full(lp.shape), _full(nw.shape), qspec, kspec, kspec],
        out_specs=qspec,
        out_shape=jax.ShapeDtypeStruct((B * S, ATT_COLS), BF16),
        scratch_shapes=[pltpu.VMEM((2, tq, LANES), F32), pltpu.VMEM((2, tq, LANES), F32),
                        pltpu.VMEM((2, tq, ATT_V_DIM), F32)],
        compiler_params=_cparams("parallel", "parallel", "arbitrary"),
        name="diff_attention",
    )(lp, nw, q, k, v)


def _decode_attn_kernel(pt_ref, lp_ref, nw_ref, q_ref, kn_ref, vn_ref, *refs, n_pages, lam_init):
    k_refs = refs[:n_pages]
    v_refs = refs[n_pages:2 * n_pages]
    o_ref = refs[2 * n_pages]
    R = 2 * ATT_HEADS
    page = k_refs[0].shape[2]
    r = lax.broadcasted_iota(jnp.int32, (R, ATT_COLS), 0)
    grp = lax.broadcasted_iota(jnp.int32, (R, ATT_COLS), 1) // ATT_QK_DIM
    target = jnp.where(r < ATT_HEADS, 2 * r, 2 * (r - ATT_HEADS) + 1)
    qf = jnp.where(grp == target, q_ref[0].astype(BF16).astype(F32), 0.0)
    qt = qf.astype(BF16)
    s = [jnp.dot(qt, k_ref[...].reshape(ATT_COLS, page).astype(BF16), preferred_element_type=F32)
         for k_ref in k_refs]
    s_new = jnp.sum(qf * kn_ref[0].astype(BF16).astype(F32), axis=1, keepdims=True)
    m = s_new
    for sj in s:
        m = jnp.maximum(m, jnp.max(sj, axis=1, keepdims=True))
    p = [jnp.exp(sj - m) for sj in s]
    p_new = jnp.exp(s_new - m)
    l = p_new
    for pj in p:
        l = l + jnp.sum(pj, axis=1, keepdims=True)
    inv_l = 1.0 / l
    lam = _lambda_value(lp_ref[...], lam_init)
    diff = lambda t: t - lam * pltpu.roll(t, ATT_HEADS, 0)
    a = [diff(pj * inv_l).astype(BF16) for pj in p]
    a_new = diff(jnp.broadcast_to(p_new * inv_l, (R, 128)))[:, 0:1].astype(BF16).astype(F32)
    row = lax.broadcasted_iota(jnp.int32, (R, ATT_V_DIM), 0)
    out = a_new * vn_ref[0].astype(BF16).astype(F32)
    for h in range(ATT_HEADS):
        acc = jnp.zeros((R, ATT_V_DIM), F32)
        for aj, v_ref in zip(a, v_refs):
            v_h = v_ref[pl.ds(h, page, stride=ATT_HEADS), :].astype(BF16)
            acc = acc + jnp.dot(aj, v_h, preferred_element_type=F32)
        out = out + jnp.where(row == h, acc, 0.0)
    o_ref[0] = _head_norm(out, nw_ref[...], lam_init)


def _decode_attention(pt, layer, q, k_new, v_new, cache_k, cache_v, lp, nw, lam_init):
    Bs, n_pages = pt.shape
    R = 2 * ATT_HEADS
    qspec = pl.BlockSpec((1, 1, ATT_COLS), lambda b, pt: (b, 0, 0))
    vspec = pl.BlockSpec((1, R, ATT_V_DIM), lambda b, pt: (b, 0, 0))
    kpages = [pl.BlockSpec((None, None) + cache_k.shape[2:], lambda b, pt, j=j: (layer, pt[b, j], 0, 0, 0))
              for j in range(n_pages)]
    vpages = [pl.BlockSpec((None, None) + cache_v.shape[2:], lambda b, pt, j=j: (layer, pt[b, j], 0, 0))
              for j in range(n_pages)]
    grid_spec = pltpu.PrefetchScalarGridSpec(
        num_scalar_prefetch=1,
        grid=(Bs,),
        in_specs=[pl.BlockSpec(lp.shape, lambda b, pt: (0, 0)), pl.BlockSpec(nw.shape, lambda b, pt: (0, 0)),
                  qspec, qspec, vspec] + kpages + vpages,
        out_specs=vspec,
    )
    return pl.pallas_call(
        functools.partial(_decode_attn_kernel, n_pages=n_pages, lam_init=lam_init),
        grid_spec=grid_spec,
        out_shape=jax.ShapeDtypeStruct((Bs, R, ATT_V_DIM), F32),
        compiler_params=_cparams("parallel"),
        name="decode_attention",
    )(pt, lp, nw, q, k_new, v_new, *([cache_k] * n_pages), *([cache_v] * n_pages))


def _proj_ln_kernel(*refs, n_in, alpha):
    a_refs = refs[:n_in]
    w_refs = refs[n_in:2 * n_in]
    x_ref, g_ref, b_ref, o_ref = refs[2 * n_in:]
    acc = alpha * x_ref[...]
    for a_ref, w_ref in zip(a_refs, w_refs):
        acc = acc + jnp.dot(a_ref[...].astype(BF16), w_ref[...], preferred_element_type=F32)
    o_ref[...] = _layer_norm(acc, g_ref[...], b_ref[...])


def _proj_ln(acts, ws, x, g, b, alpha, tm):
    T, D = x.shape
    row = lambda n: pl.BlockSpec((tm, n), lambda i: (i, 0))
    return pl.pallas_call(
        functools.partial(_proj_ln_kernel, n_in=len(acts), alpha=alpha),
        grid=(T // tm,),
        in_specs=[row(a.shape[1]) for a in acts] + [_full(w.shape) for w in ws] + [row(D), _full(g.shape), _full(b.shape)],
        out_specs=row(D),
        out_shape=jax.ShapeDtypeStruct((T, D), F32),
        compiler_params=_cparams("parallel"),
        name="proj_ln",
    )(*acts, *ws, x, g, b)


def _matmul_kernel(x_ref, w_ref, *o_refs):
    y = jnp.dot(x_ref[...].astype(BF16), w_ref[...], preferred_element_type=F32)
    for o_ref in o_refs:
        o_ref[...] = y.astype(o_ref.dtype)


def _matmul(x, w, out_dtypes, tm, tn):
    M, K = x.shape
    N = w.shape[1]
    ospec = pl.BlockSpec((tm, tn), lambda i, j: (i, j))
    return pl.pallas_call(
        _matmul_kernel,
        grid=(M // tm, N // tn),
        in_specs=[pl.BlockSpec((tm, K), lambda i, j: (i, 0)), pl.BlockSpec((K, tn), lambda i, j: (0, j))],
        out_specs=tuple(ospec for _ in out_dtypes),
        out_shape=tuple(jax.ShapeDtypeStruct((M, N), d) for d in out_dtypes),
        compiler_params=_cparams("parallel", "parallel"),
        name="matmul",
    )(x, w)


def _softmax_rows(s):
    m = jnp.max(s, axis=1, keepdims=True)
    p = jnp.exp(s - m)
    return p / jnp.sum(p, axis=1, keepdims=True)


def _cross_kernel(x_ref, wq_ref, mk_ref, mv_ref, wo_ref, g_ref, b_ref, o_ref, *, alpha):
    x = x_ref[...]
    D = x.shape[1]
    dh = D // MEM_HEADS
    q = jnp.dot(x.astype(BF16), wq_ref[...], preferred_element_type=F32).astype(BF16)
    outs = []
    for h in range(MEM_HEADS):
        sl = slice(h * dh, (h + 1) * dh)
        s = lax.dot_general(q[:, sl], mk_ref[:, sl], NT_DIMS, preferred_element_type=F32) * (dh ** -0.5)
        outs.append(jnp.dot(_softmax_rows(s).astype(BF16), mv_ref[:, sl], preferred_element_type=F32).astype(BF16))
    o = jnp.concatenate(outs, axis=1)
    y = alpha * x + jnp.dot(o, wo_ref[...], preferred_element_type=F32)
    o_ref[...] = _layer_norm(y, g_ref[...], b_ref[...])


def _cross_attention(x, wq, mk, mv, wo, g, b, alpha, B, S, n_mem, tm):
    T, D = x.shape
    nt = S // tm
    row = pl.BlockSpec((tm, D), lambda bi, i: (bi * nt + i, 0))
    mem = pl.BlockSpec((n_mem, D), lambda bi, i: (bi, 0))
    return pl.pallas_call(
        functools.partial(_cross_kernel, alpha=alpha),
        grid=(B, nt),
        in_specs=[row, _full(wq.shape), mem, mem, _full(wo.shape), _full(g.shape), _full(b.shape)],
        out_specs=row,
        out_shape=jax.ShapeDtypeStruct((T, D), F32),
        compiler_params=_cparams("parallel", "parallel"),
        name="cross_attention",
    )(x, wq, mk, mv, wo, g, b)


def _cross_decode_kernel(q_ref, mk_ref, mv_ref, o_ref):
    R = q_ref.shape[1] // 2
    rows_kv = mk_ref.shape[0]
    dh = 2 * LANES
    part = lax.dot_general(q_ref[0].astype(BF16), mk_ref[...].astype(BF16), NT_DIMS,
                           preferred_element_type=F32)
    s = (part[:R] + pltpu.roll(part[R:], rows_kv - MEM_HEADS, 1)) * (dh ** -0.5)
    own = (lax.broadcasted_iota(jnp.int32, (R, rows_kv), 1) & (R - 1)) == lax.broadcasted_iota(jnp.int32, (R, rows_kv), 0)
    p = _softmax_rows(jnp.where(own, s, -jnp.inf))
    a = jnp.concatenate([p, pltpu.roll(p, MEM_HEADS, 1)], axis=0).astype(BF16)
    o_ref[0] = jnp.dot(a, mv_ref[...].astype(BF16), preferred_element_type=F32)


def _cross_decode(q, mem_k, mem_v, layer):
    Bs, R2, _ = q.shape
    tok = pl.BlockSpec((1, R2, LANES), lambda b: (b, 0, 0))
    mem = pl.BlockSpec((None, None) + mem_k.shape[2:], lambda b: (layer, b, 0, 0))
    return pl.pallas_call(
        _cross_decode_kernel,
        grid=(Bs,),
        in_specs=[tok, mem, mem],
        out_specs=tok,
        out_shape=jax.ShapeDtypeStruct((Bs, R2, LANES), F32),
        compiler_params=_cparams("parallel"),
        name="cross_decode",
    )(q, mem_k, mem_v)


def _ffn_kernel(x_ref, wg_ref, wu_ref, wd_ref, g_ref, b_ref, o_ref, acc_ref, *, alpha):
    f = pl.program_id(1)
    x = x_ref[...]
    xb = x.astype(BF16)

    @pl.when(f == 0)
    def _():
        acc_ref[...] = alpha * x

    gate = jnp.dot(xb, wg_ref[...], preferred_element_type=F32)
    up = jnp.dot(xb, wu_ref[...], preferred_element_type=F32)
    acc_ref[...] += jnp.dot((_silu(gate) * up).astype(BF16), wd_ref[...], preferred_element_type=F32)

    @pl.when(f == pl.num_programs(1) - 1)
    def _():
        o_ref[...] = _layer_norm(acc_ref[...], g_ref[...], b_ref[...])


def _ffn(x, w_gu, w_down, g, b, alpha, tm, tf):
    T, D = x.shape
    F = w_down.shape[0]
    nf = F // tf
    row = pl.BlockSpec((tm, D), lambda i, f: (i, 0))
    return pl.pallas_call(
        functools.partial(_ffn_kernel, alpha=alpha),
        grid=(T // tm, nf),
        in_specs=[row, pl.BlockSpec((D, tf), lambda i, f: (0, f)), pl.BlockSpec((D, tf), lambda i, f: (0, nf + f)),
                  pl.BlockSpec((tf, D), lambda i, f: (f, 0)), _full(g.shape), _full(b.shape)],
        out_specs=row,
        out_shape=jax.ShapeDtypeStruct((T, D), F32),
        scratch_shapes=[pltpu.VMEM((tm, D), F32)],
        compiler_params=_cparams("parallel", "arbitrary"),
        name="ffn",
    )(x, w_gu, w_gu, w_down, g, b)


def _router_kernel(x_ref, wr_ref, gate_ref, top_i_ref, top_g_ref):
    logits = jnp.dot(x_ref[...].astype(BF16), wr_ref[...], preferred_element_type=F32)
    lane = lax.broadcasted_iota(jnp.int32, logits.shape, 1).astype(F32)
    none = float(N_EXPERTS)
    m1 = jnp.max(logits, axis=1, keepdims=True)
    i1 = jnp.min(jnp.where(logits == m1, lane, none), axis=1, keepdims=True)
    rest = jnp.where(lane == i1, -jnp.inf, logits)
    m2 = jnp.max(rest, axis=1, keepdims=True)
    i2 = jnp.min(jnp.where(rest == m2, lane, none), axis=1, keepdims=True)
    e = jnp.exp(m2 - m1)
    den = 1.0 + e
    gate_ref[...] = jnp.where(lane == i1, 1.0 / den, 0.0) + jnp.where(lane == i2, e / den, 0.0)
    first = lax.broadcasted_iota(jnp.int32, top_i_ref.shape, 1) == 0
    top_i_ref[...] = jnp.where(first, i1, i2).astype(jnp.int32)
    top_g_ref[...] = jnp.where(first, 1.0 / den, e / den)


def _router(x, wr, tm):
    T, D = x.shape
    top = pl.BlockSpec((tm, 2), lambda i: (i, 0))
    return pl.pallas_call(
        _router_kernel,
        grid=(T // tm,),
        in_specs=[pl.BlockSpec((tm, D), lambda i: (i, 0)), _full(wr.shape)],
        out_specs=(pl.BlockSpec((tm, N_EXPERTS), lambda i: (i, 0)), top, top),
        out_shape=(jax.ShapeDtypeStruct((T, N_EXPERTS), F32), jax.ShapeDtypeStruct((T, 2), jnp.int32),
                   jax.ShapeDtypeStruct((T, 2), F32)),
        compiler_params=_cparams("parallel"),
        name="router",
    )(x, wr)


def _moe_plan(top_i, tm_e, n_tiles):
    T = top_i.shape[0]
    chosen = (top_i[:, :, None] == jnp.arange(N_EXPERTS, dtype=jnp.int32)[None, None, :]).any(axis=1)
    chosen = chosen.astype(jnp.int32)
    count = chosen.sum(axis=0)
    rank = jnp.cumsum(chosen, axis=0) - chosen
    tiles = (count + tm_e - 1) // tm_e
    tile_end = jnp.cumsum(tiles)
    row_start = (tile_end - tiles) * tm_e
    dest = jnp.take_along_axis(row_start[None, :] + rank, top_i, axis=1).astype(jnp.int32)
    token = jnp.broadcast_to(jnp.arange(T, dtype=jnp.int32)[:, None], (T, 2))
    src = jnp.zeros((n_tiles * tm_e,), jnp.int32).at[dest.reshape(-1)].set(token.reshape(-1))
    n_used = tile_end[-1]
    tile = jnp.arange(n_tiles, dtype=jnp.int32)
    tile_expert = jnp.searchsorted(tile_end, jnp.minimum(tile, n_used - 1), side="right").astype(jnp.int32)
    return src, dest, tile_expert, n_used.reshape(1).astype(jnp.int32)


def _moe_ffn_kernel(te_ref, nv_ref, src_ref, x_hbm, wg_ref, wu_ref, wd_ref, y_ref, xrows_ref, xb_ref, acc_ref, sem):
    tm_e, n_chunk = xrows_ref.shape[1], xrows_ref.shape[2]
    i = pl.program_id(0)
    f = pl.program_id(1)
    last_f = pl.num_programs(1) - 1
    n_used = nv_ref[0]
    slot = lax.rem(i, 2)

    def start_gather(tile, slot):
        def body(r, carry):
            pltpu.make_async_copy(x_hbm.at[src_ref[tile * tm_e + r]], xrows_ref.at[slot, r], sem.at[slot]).start()
            return carry
        lax.fori_loop(0, tm_e, body, 0)

    def wait_gather(slot):
        pltpu.make_async_copy(x_hbm.at[pl.ds(0, tm_e)], xrows_ref.at[slot], sem.at[slot]).wait()

    @pl.when(f == 0)
    def _():
        @pl.when(i == 0)
        def _():
            start_gather(0, 0)

        @pl.when(i < n_used)
        def _():
            wait_gather(slot)
            for c in range(n_chunk):
                xb_ref[:, c * LANES:(c + 1) * LANES] = xrows_ref[slot, :, c, :].astype(BF16)

        @pl.when(i + 1 < n_used)
        def _():
            start_gather(i + 1, 1 - slot)

    @pl.when(i < n_used)
    def _():
        xb = xb_ref[...]
        gate = jnp.dot(xb, wg_ref[...], preferred_element_type=F32)
        up = jnp.dot(xb, wu_ref[...], preferred_element_type=F32)
        part = jnp.dot((_silu(gate) * up).astype(BF16), wd_ref[...], preferred_element_type=F32)

        @pl.when(f == 0)
        def _():
            acc_ref[...] = part

        @pl.when(f > 0)
        def _():
            acc_ref[...] += part

        @pl.when(f == last_f)
        def _():
            y = acc_ref[...]
            for c in range(n_chunk):
                y_ref[:, c, :] = y[:, c * LANES:(c + 1) * LANES]

    @pl.when(jnp.logical_and(i >= n_used, f == last_f))
    def _():
        y_ref[...] = jnp.zeros(y_ref.shape, F32)


def _moe_ffn(x_rows, src, tile_expert, n_used, w_gu, w_down, tm_e, tf):
    T, n_chunk, _ = x_rows.shape
    D = n_chunk * LANES
    E, F, _ = w_down.shape
    nf = F // tf
    n_tiles = tile_expert.shape[0]
    grid_spec = pltpu.PrefetchScalarGridSpec(
        num_scalar_prefetch=3,
        grid=(n_tiles, nf),
        in_specs=[pl.BlockSpec(memory_space=pl.ANY),
                  pl.BlockSpec((None, D, tf), lambda i, f, te, nv, src: (te[i], 0, f)),
                  pl.BlockSpec((None, D, tf), lambda i, f, te, nv, src: (te[i], 0, nf + f)),
                  pl.BlockSpec((None, tf, D), lambda i, f, te, nv, src: (te[i], f, 0))],
        out_specs=pl.BlockSpec((tm_e, n_chunk, LANES), lambda i, f, te, nv, src: (i, 0, 0)),
        scratch_shapes=[pltpu.VMEM((2, tm_e, n_chunk, LANES), F32), pltpu.VMEM((tm_e, D), BF16),
                        pltpu.VMEM((tm_e, D), F32), pltpu.SemaphoreType.DMA((2,))],
    )
    return pl.pallas_call(
        _moe_ffn_kernel,
        grid_spec=grid_spec,
        out_shape=jax.ShapeDtypeStruct((n_tiles * tm_e, n_chunk, LANES), F32),
        compiler_params=_cparams("arbitrary", "arbitrary"),
        name="moe_ffn",
    )(tile_expert, n_used, src, x_rows, w_gu, w_gu, w_down)


def _moe_combine_kernel(d1_ref, d2_ref, x_ref, gate_ref, g_ref, b_ref, y_hbm, o_ref, yrows_ref, ysum_ref, sem, *, alpha):
    tm, n_chunk = yrows_ref.shape[2], yrows_ref.shape[3]
    i = pl.program_id(0)
    slot = lax.rem(i, 2)

    def start_gather(tile, slot):
        def body(r, carry):
            t = tile * tm + r
            pltpu.make_async_copy(y_hbm.at[d1_ref[t]], yrows_ref.at[slot, 0, r], sem.at[slot]).start()
            pltpu.make_async_copy(y_hbm.at[d2_ref[t]], yrows_ref.at[slot, 1, r], sem.at[slot]).start()
            return carry
        lax.fori_loop(0, tm, body, 0)

    @pl.when(i == 0)
    def _():
        start_gather(0, 0)

    for k in range(2):
        pltpu.make_async_copy(y_hbm.at[pl.ds(0, tm)], yrows_ref.at[slot, k], sem.at[slot]).wait()

    @pl.when(i + 1 < pl.num_programs(0))
    def _():
        start_gather(i + 1, 1 - slot)

    g1 = gate_ref[:, 0:1]
    g2 = gate_ref[:, 1:2]
    for c in range(n_chunk):
        ysum_ref[:, c * LANES:(c + 1) * LANES] = yrows_ref[slot, 0, :, c, :] * g1 + yrows_ref[slot, 1, :, c, :] * g2
    o_ref[...] = _layer_norm(alpha * x_ref[...] + ysum_ref[...], g_ref[...], b_ref[...])


def _moe_combine(x, y_rows, dest, top_g, g, b, alpha, tm):
    T, D = x.shape
    n_chunk = D // LANES
    row = pl.BlockSpec((tm, D), lambda i, d1, d2: (i, 0))
    grid_spec = pltpu.PrefetchScalarGridSpec(
        num_scalar_prefetch=2,
        grid=(T // tm,),
        in_specs=[row, pl.BlockSpec((tm, 2), lambda i, d1, d2: (i, 0)),
                  pl.BlockSpec(g.shape, lambda i, d1, d2: (0, 0)), pl.BlockSpec(b.shape, lambda i, d1, d2: (0, 0)),
                  pl.BlockSpec(memory_space=pl.ANY)],
        out_specs=row,
        scratch_shapes=[pltpu.VMEM((2, 2, tm, n_chunk, LANES), F32), pltpu.VMEM((tm, D), F32),
                        pltpu.SemaphoreType.DMA((2,))],
    )
    return pl.pallas_call(
        functools.partial(_moe_combine_kernel, alpha=alpha),
        grid_spec=grid_spec,
        out_shape=jax.ShapeDtypeStruct((T, D), F32),
        compiler_params=_cparams("arbitrary"),
        name="moe_combine",
    )(dest[:, 0], dest[:, 1], x, top_g, g, b, y_rows)


def _moe_kernel(x_ref, gate_ref, wg_ref, wu_ref, wd_ref, g_ref, b_ref, o_ref, acc_ref, *, alpha):
    e = pl.program_id(1)
    f = pl.program_id(2)
    x = x_ref[...]
    xb = x.astype(BF16)

    @pl.when(jnp.logical_and(e == 0, f == 0))
    def _():
        acc_ref[...] = alpha * x

    gates = gate_ref[...]
    lane = lax.broadcasted_iota(jnp.int32, gates.shape, 1)
    ge = jnp.sum(jnp.where(lane == e, gates, 0.0), axis=1, keepdims=True)
    gate = jnp.dot(xb, wg_ref[0], preferred_element_type=F32)
    up = jnp.dot(xb, wu_ref[0], preferred_element_type=F32)
    acc_ref[...] += jnp.dot((_silu(gate) * up).astype(BF16), wd_ref[0], preferred_element_type=F32) * ge

    @pl.when(jnp.logical_and(e == pl.num_programs(1) - 1, f == pl.num_programs(2) - 1))
    def _():
        o_ref[...] = _layer_norm(acc_ref[...], g_ref[...], b_ref[...])


def _moe(x, gates, w_gu, w_down, g, b, alpha, tm, tf):
    T, D = x.shape
    E, F, _ = w_down.shape
    nf = F // tf
    row = pl.BlockSpec((tm, D), lambda i, e, f: (i, 0))
    return pl.pallas_call(
        functools.partial(_moe_kernel, alpha=alpha),
        grid=(T // tm, E, nf),
        in_specs=[row, pl.BlockSpec((tm, E), lambda i, e, f: (i, 0)),
                  pl.BlockSpec((1, D, tf), lambda i, e, f: (e, 0, f)),
                  pl.BlockSpec((1, D, tf), lambda i, e, f: (e, 0, nf + f)),
                  pl.BlockSpec((1, tf, D), lambda i, e, f: (e, f, 0)), _full(g.shape), _full(b.shape)],
        out_specs=row,
        out_shape=jax.ShapeDtypeStruct((T, D), F32),
        scratch_shapes=[pltpu.VMEM((tm, D), F32)],
        compiler_params=_cparams("parallel", "arbitrary", "arbitrary"),
        name="moe",
    )(x, gates, w_gu, w_gu, w_down, g, b)


def _rope_tables(pos):
    half = ATT_QK_DIM // 2
    inv = ROPE_THETA ** (-jnp.arange(0, ATT_QK_DIM, 2, dtype=F32) / ATT_QK_DIM)
    ang = pos.astype(F32)[:, None] * inv[None, :]
    cos, sin = jnp.cos(ang), jnp.sin(ang)
    return jnp.concatenate([cos] * 4, axis=1), jnp.concatenate([-sin, sin, -sin, sin], axis=1)


def _row_tile(n, target):
    t = min(n, target)
    while n % t:
        t //= 2
    return t


def kernel(x_prompt, x_sample, cache_attn_k, cache_attn_v, cache_mem_k, cache_mem_v, state_conv, state_ssm,
           page_table, mem_prompt, ln_g, ln_b, w_in, conv_w, conv_b, dt_bias, a_log, d_skip, ssm_norm_w,
           lambda_params, attn_norm_w, w_out, w_cq, w_ckv, w_co, w_ffn_gu, w_ffn_down, w_router, w_exp_gu,
           w_exp_down):
    B, S, D = x_prompt.shape
    Bs = x_sample.shape[0]
    depth = w_in.shape[0]
    n_phys, page = cache_attn_k.shape[1], cache_attn_k.shape[2]
    n_pages = page_table.shape[1]
    past_len = n_pages * page
    n_mem = mem_prompt.shape[1]
    alpha = (2 * depth) ** 0.25
    assert S % SSD_CHUNK == 0 and x_sample.shape[1] == 1

    tm_p = _row_tile(S, 512)
    tm_s = _row_tile(Bs, 128)
    cos_p, sin_p = _rope_tables(jnp.arange(S, dtype=jnp.int32))
    cos_s, sin_s = _rope_tables(jnp.full((tm_s,), past_len, jnp.int32))

    hp = x_prompt.reshape(B * S, D)
    hs = x_sample.reshape(Bs, D)
    mem = mem_prompt.reshape(B * n_mem, D)
    cache_k = cache_attn_k.transpose(0, 1, 3, 4, 2)
    cache_v = cache_attn_v.reshape(depth, n_phys, page * ATT_HEADS, ATT_V_DIM)
    dh = D // MEM_HEADS
    mem_rows = lambda t: (t.reshape(depth, Bs, n_mem, MEM_HEADS, dh // LANES, LANES).swapaxes(3, 4)
                          .reshape(depth, Bs, n_mem * dh // LANES * MEM_HEADS, LANES))
    cmem_k, cmem_v = mem_rows(cache_mem_k), mem_rows(cache_mem_v)
    cstate = state_conv.reshape(depth, Bs, (CONV_WIDTH - 1) * CONV_DIM)
    pad_heads = lambda t: jnp.pad(t, [(0, 0)] * (t.ndim - 2) + [(0, 2 * ATT_HEADS - t.shape[-2]), (0, 0)])

    splits = (ATT_COLS, 2 * ATT_COLS, 3 * ATT_COLS, 3 * ATT_COLS + SSM_INNER, 3 * ATT_COLS + SSM_INNER + CONV_DIM)
    outs = {n: [] for n in ("kp", "vp", "mkp", "mvp", "cp", "sp", "ks", "vs", "cs", "ss")}
    for l in range(depth):
        lam_init = 0.8 - 0.6 * math.exp(-0.3 * l)
        wl = w_in[l].astype(BF16)
        wq, wk, wv, wz, wxbc, wdt = (wl[:, a:b] for a, b in zip((0,) + splits, splits + (wl.shape[1],)))
        w_proj = (wq, wk, wv, wz, wxbc, wdt, wdt.T)
        sw = (conv_w[l], conv_b[l][None, :], dt_bias[l][None, :], dt_bias[l][:, None], a_log[l][None, :],
              a_log[l][:, None], d_skip[l][None, :], ssm_norm_w[l][None, :])
        lp = lambda_params[l]
        nw = attn_norm_w[l][None, :]
        wo = w_out[l].astype(BF16)
        wo_att, wo_ssm = wo[:ATT_COLS], wo[ATT_COLS:]
        wcq = w_cq[l].astype(BF16)
        wckv = w_ckv[l].astype(BF16)
        wco = w_co[l].astype(BF16)
        g0, g1, g2 = (ln_g[l, i][None, :] for i in range(3))
        b0, b1, b2 = (ln_b[l, i][None, :] for i in range(3))

        q, kf, kb, vf, vb, z, xbc, dt, dtT = _in_proj(hp, cos_p, sin_p, w_proj, tm_p, S // tm_p)
        att = _attention(q, kb, vb, lp, nw, lam_init, B, S)
        y, conv_new, h_last = _ssd(xbc, z, dt, dtT, sw, B, S)
        hp = _proj_ln((att, y), (wo_att, wo_ssm), hp, g0, b0, alpha, tm_p)
        outs["kp"].append(kf.reshape(B, S, 2 * ATT_HEADS, ATT_QK_DIM))
        outs["vp"].append(vf.reshape(B, S, ATT_HEADS, ATT_V_DIM))
        outs["cp"].append(conv_new)
        outs["sp"].append(h_last.reshape(B, SSM_HEADS, SSM_HEAD_DIM, SSM_STATE))
        mkv_f, mkv_b = _matmul(mem, wckv, (F32, BF16), _row_tile(B * n_mem, 512), D)
        outs["mkp"].append(mkv_f[:, :D].reshape(B, n_mem, MEM_HEADS, D // MEM_HEADS))
        outs["mvp"].append(mkv_f[:, D:].reshape(B, n_mem, MEM_HEADS, D // MEM_HEADS))
        hp = _cross_attention(hp, wcq, mkv_b[:, :D], mkv_b[:, D:], wco, g1, b1, alpha, B, S, n_mem, tm_p)

        q, kf, _, vf, _, z, xbc, dt, _ = _in_proj(hs, cos_s, sin_s, w_proj, tm_s, 1)
        att = _decode_attention(page_table, l, q.astype(F32).reshape(Bs, 1, ATT_COLS), kf.reshape(Bs, 1, ATT_COLS),
                                pad_heads(vf.reshape(Bs, ATT_HEADS, ATT_V_DIM)), cache_k, cache_v, lp, nw, lam_init)
        att = att[:, :ATT_HEADS].reshape(Bs, ATT_COLS)
        y, conv_new, h_new = _ssd_step(xbc, z, dt, cstate, state_ssm, l, sw)
        hs = _proj_ln((att, y), (wo_att, wo_ssm), hs, g0, b0, alpha, tm_s)
        outs["ks"].append(kf.reshape(Bs, 1, 2 * ATT_HEADS, ATT_QK_DIM))
        outs["vs"].append(vf.reshape(Bs, 1, ATT_HEADS, ATT_V_DIM))
        outs["cs"].append(conv_new.reshape(Bs, CONV_WIDTH - 1, CONV_DIM))
        outs["ss"].append(h_new)
        (qc,) = _matmul(hs, wcq, (F32,), tm_s, D)
        q_rows = pad_heads(qc.reshape(Bs, MEM_HEADS, dh // LANES, LANES).swapaxes(1, 2))
        oc = _cross_decode(q_rows.reshape(Bs, -1, LANES), cmem_k, cmem_v, l)
        oc = oc.reshape(Bs, dh // LANES, 2 * ATT_HEADS, LANES)[:, :, :MEM_HEADS].swapaxes(1, 2).reshape(Bs, D)
        hs = _proj_ln((oc,), (wco,), hs, g1, b1, alpha, tm_s)

        if l % 2 == 0:
            wgu = w_ffn_gu[l // 2].astype(BF16)
            wd = w_ffn_down[l // 2].astype(BF16)
            tf = wd.shape[0] // 2
            hp = _ffn(hp, wgu, wd, g2, b2, alpha, tm_p, tf)
            hs = _ffn(hs, wgu, wd, g2, b2, alpha, tm_s, tf)
        else:
            wr = w_router[l // 2].astype(BF16)
            wgu = w_exp_gu[l // 2].astype(BF16)
            wd = w_exp_down[l // 2].astype(BF16)
            tf = wd.shape[1] // 2
            _, top_i, top_g = _router(hp, wr, tm_p)
            tm_e = _row_tile(2 * B * S, MOE_ROW_TILE)
            src, dest, tile_expert, n_used = _moe_plan(top_i, tm_e, 2 * B * S // tm_e + N_EXPERTS)
            y_rows = _moe_ffn(hp.reshape(B * S, D // LANES, LANES), src, tile_expert, n_used, wgu, wd, tm_e, tf)
            hp = _moe_combine(hp, y_rows, dest, top_g, g2, b2, alpha, _row_tile(B * S, 256))
            hs = _moe(hs, _router(hs, wr, tm_s)[0], wgu, wd, g2, b2, alpha, tm_s, tf)

    st = {n: jnp.stack(v) for n, v in outs.items()}
    return (hp.reshape(B, S, D), hs.reshape(Bs, 1, D), st["kp"], st["vp"], st["mkp"], st["mvp"], st["cp"],
            st["sp"], st["ks"], st["vs"], st["cs"], st["ss"])
```

```python
import functools
import math

import jax
import jax.numpy as jnp
from jax import lax
from jax.experimental import pallas as pl
from jax.experimental.pallas import tpu as pltpu

F32 = jnp.float32
BF16 = jnp.bfloat16

ATT_HEADS = 4
ATT_V_DIM = 128
ATT_QK_DIM = 64
ATT_COLS = 512
SSM_INNER = 512
SSM_HEAD_DIM = 64
SSM_HEADS = 8
SSM_GROUPS = 2
SSM_STATE = 128
CONV_WIDTH = 4
CONV_DIM = 1024
SSD_CHUNK = 128
MEM_HEADS = 4
N_EXPERTS = 8
ROPE_THETA = 10000.0
LN_EPS = 1e-5
NORM_EPS = 1e-5
QK_SCALE = ATT_QK_DIM ** -0.5

LANES = 128
MOE_ROW_TILE = 512
ROW_COPY_UNROLL = 8
VMEM_LIMIT_BYTES = 56 * 1024 * 1024
NT_DIMS = (((1,), (1,)), ((), ()))


def _cparams(*sem):
    return pltpu.CompilerParams(dimension_semantics=sem, vmem_limit_bytes=VMEM_LIMIT_BYTES)


def _silu(x):
    return x / (1.0 + jnp.exp(-x))


def _softplus(x):
    return jnp.maximum(x, 0.0) + jnp.log1p(jnp.exp(-jnp.abs(x)))


def _layer_norm(y, g, b):
    mu = jnp.mean(y, axis=-1, keepdims=True)
    yc = y - mu
    var = jnp.mean(yc * yc, axis=-1, keepdims=True)
    return yc * lax.rsqrt(var + LN_EPS) * g + b


def _full(shape):
    return pl.BlockSpec(shape, lambda *_: (0,) * len(shape))


def _in_proj_kernel(x_ref, cos_ref, sin_ref, wq_ref, wk_ref, wv_ref, wz_ref, wxbc_ref, wdt_ref, wdtT_ref,
                    q_ref, kf_ref, kb_ref, vf_ref, vb_ref, z_ref, xbc_ref, dt_ref, dtT_ref):
    tm = x_ref.shape[0]
    xb = x_ref[...].astype(BF16)
    cos = jnp.concatenate([cos_ref[...]] * 4, axis=1)
    sin = jnp.concatenate([sin_ref[...]] * 4, axis=1)
    lane = lax.broadcasted_iota(jnp.int32, (tm, ATT_COLS), 1)
    first_half = (lane & (ATT_QK_DIM - 1)) < (ATT_QK_DIM // 2)

    def rope(t):
        partner = jnp.where(first_half, pltpu.roll(t, ATT_COLS - ATT_QK_DIM // 2, 1),
                            pltpu.roll(t, ATT_QK_DIM // 2, 1))
        return t * cos + partner * sin

    q = rope(jnp.dot(xb, wq_ref[...], preferred_element_type=F32))
    q_ref[...] = (q * QK_SCALE).astype(BF16)
    k = rope(jnp.dot(xb, wk_ref[...], preferred_element_type=F32))
    kf_ref[...] = k
    kb_ref[...] = k.astype(BF16)
    v = jnp.dot(xb, wv_ref[...], preferred_element_type=F32)
    vf_ref[...] = v
    vb_ref[...] = v.astype(BF16)
    z_ref[...] = jnp.dot(xb, wz_ref[...], preferred_element_type=F32)
    xbc_ref[...] = jnp.dot(xb, wxbc_ref[...], preferred_element_type=F32)
    dt_ref[...] = jnp.dot(xb, wdt_ref[...], preferred_element_type=F32)
    dtT_ref[...] = lax.dot_general(wdtT_ref[...], xb, NT_DIMS, preferred_element_type=F32)


def _in_proj(x, cos_t, sin_t, w, tm, n_pos_blocks):
    T, D = x.shape
    wq, wk, wv, wz, wxbc, wdt, wdtT = w
    row = lambda n: pl.BlockSpec((tm, n), lambda i: (i, 0))
    pos = pl.BlockSpec((tm, 128), lambda i: (i % n_pos_blocks, 0))
    out_shape = (
        jax.ShapeDtypeStruct((T, ATT_COLS), BF16),
        jax.ShapeDtypeStruct((T, ATT_COLS), F32),
        jax.ShapeDtypeStruct((T, ATT_COLS), BF16),
        jax.ShapeDtypeStruct((T, ATT_COLS), F32),
        jax.ShapeDtypeStruct((T, ATT_COLS), BF16),
        jax.ShapeDtypeStruct((T, SSM_INNER), F32),
        jax.ShapeDtypeStruct((T, CONV_DIM), F32),
        jax.ShapeDtypeStruct((T, SSM_HEADS), F32),
        jax.ShapeDtypeStruct((SSM_HEADS, T), F32),
    )
    out_specs = (row(ATT_COLS), row(ATT_COLS), row(ATT_COLS), row(ATT_COLS), row(ATT_COLS),
                 row(SSM_INNER), row(CONV_DIM), row(SSM_HEADS),
                 pl.BlockSpec((SSM_HEADS, tm), lambda i: (0, i)))
    return pl.pallas_call(
        _in_proj_kernel,
        grid=(T // tm,),
        in_specs=[row(D), pos, pos, _full(wq.shape), _full(wk.shape), _full(wv.shape), _full(wz.shape),
                  _full(wxbc.shape), _full(wdt.shape), _full(wdtT.shape)],
        out_specs=out_specs,
        out_shape=out_shape,
        compiler_params=_cparams("parallel"),
        name="in_proj",
    )(x, cos_t, sin_t, wq, wk, wv, wz, wxbc, wdt, wdtT)


def _gated_group_norm(y, z, norm_w):
    y = y * _silu(z)
    half = SSM_INNER // SSM_GROUPS
    y2 = y * y
    ms0 = jnp.mean(y2[:, :half], axis=-1, keepdims=True)
    ms1 = jnp.mean(y2[:, half:], axis=-1, keepdims=True)
    lane = lax.broadcasted_iota(jnp.int32, y.shape, 1)
    scale = jnp.where(lane < half, lax.rsqrt(ms0 + NORM_EPS), lax.rsqrt(ms1 + NORM_EPS))
    return y * scale * norm_w


def _ssd_kernel(xbc_ref, z_ref, dt_ref, dtT_ref, convw_ref, convb_ref, dtb_ref, dtbT_ref, alog_ref, alogT_ref,
                dskip_ref, normw_ref, y_ref, conv_out_ref, state_out_ref, ext_ref, h_ref):
    Q = SSD_CHUNK
    c = pl.program_id(1)
    last = pl.num_programs(1) - 1

    @pl.when(c == 0)
    def _():
        ext_ref[0:8, :] = jnp.zeros((8, CONV_DIM), F32)
        h_ref[...] = jnp.zeros(h_ref.shape, F32)

    ext_ref[8:8 + Q, :] = xbc_ref[...]
    conv = convb_ref[...]
    for i in range(CONV_WIDTH):
        conv = conv + ext_ref[5 + i:5 + i + Q, :] * convw_ref[i:i + 1, :]
    ext_ref[0:8, :] = xbc_ref[Q - 8:Q, :]
    conv = _silu(conv)
    xs = conv[:, :SSM_INNER]
    bmat = conv[:, SSM_INNER:SSM_INNER + SSM_GROUPS * SSM_STATE]
    cmat = conv[:, SSM_INNER + SSM_GROUPS * SSM_STATE:]

    dt_col = _softplus(dt_ref[...] + dtb_ref[...])
    a_col = dt_col * (-jnp.exp(alog_ref[...]))
    a_row = _softplus(dtT_ref[...] + dtbT_ref[...]) * (-jnp.exp(alogT_ref[...]))
    ri = lax.broadcasted_iota(jnp.int32, (Q, Q), 0)
    ci = lax.broadcasted_iota(jnp.int32, (Q, Q), 1)
    causal = ci <= ri
    tril = causal.astype(F32)
    triu = (ri <= ci).astype(F32)
    acs_col = jnp.dot(tril, a_col, preferred_element_type=F32, precision=lax.Precision.HIGHEST)
    acs_row = jnp.dot(a_row, triu, preferred_element_type=F32, precision=lax.Precision.HIGHEST)

    lo = ci < SSM_HEAD_DIM
    top = ri < SSM_HEAD_DIM
    pairs = []
    for k in range(SSM_HEADS // 2):
        h0, h1 = 2 * k, 2 * k + 1
        g = k // (SSM_HEADS // 2 // SSM_GROUPS)
        b_g = bmat[:, g * SSM_STATE:(g + 1) * SSM_STATE].astype(BF16)
        c_g = cmat[:, g * SSM_STATE:(g + 1) * SSM_STATE].astype(BF16)
        cb = lax.dot_general(c_g, b_g, NT_DIMS, preferred_element_type=F32)
        xs_pair = xs[:, 2 * SSM_HEAD_DIM * k:2 * SSM_HEAD_DIM * (k + 1)]
        dtc = jnp.where(lo, dt_col[:, h0:h0 + 1], dt_col[:, h1:h1 + 1])
        xdt = xs_pair * dtc
        acol = jnp.where(lo, acs_col[:, h0:h0 + 1], acs_col[:, h1:h1 + 1])
        alast0 = acs_row[h0:h0 + 1, Q - 1:Q]
        alast1 = acs_row[h1:h1 + 1, Q - 1:Q]
        alast = jnp.where(lo[0:1, :], alast0, alast1)
        xw = xdt * jnp.exp(alast - acol)
        y_pair = jnp.where(lo, dskip_ref[0:1, h0:h0 + 1], dskip_ref[0:1, h1:h1 + 1]) * xs_pair
        for h, sel in ((h0, lo), (h1, jnp.logical_not(lo))):
            seg = acs_col[:, h:h + 1] - acs_row[h:h + 1, :]
            decay = jnp.exp(jnp.where(causal, seg, -jnp.inf))
            m = (cb * decay).astype(BF16)
            y_pair = y_pair + jnp.dot(m, jnp.where(sel, xdt, 0.0).astype(BF16), preferred_element_type=F32)
        states = jnp.dot(xw.T.astype(BF16), b_g, preferred_element_type=F32)
        h_prev = h_ref[k]
        y_off = lax.dot_general(c_g, h_prev.astype(BF16), NT_DIMS, preferred_element_type=F32)
        y_pair = y_pair + y_off * jnp.exp(acol)
        dec = jnp.where(top[:, 0:1], jnp.exp(alast0), jnp.exp(alast1))
        h_ref[k] = h_prev * dec + states
        pairs.append(y_pair)
    y = jnp.concatenate(pairs, axis=1)
    y_ref[...] = _gated_group_norm(y, z_ref[...], normw_ref[...]).astype(BF16)

    @pl.when(c == last)
    def _():
        conv_out_ref[0] = xbc_ref[Q - (CONV_WIDTH - 1):Q, :]
        state_out_ref[0] = h_ref[...]


def _ssd(xbc, z, dt, dtT, sw, B, S):
    convw, convb, dtb, dtbT, alog, alogT, dskip, normw = sw
    nc = S // SSD_CHUNK
    Q = SSD_CHUNK
    rowb = lambda n: pl.BlockSpec((Q, n), lambda b, c: (b * nc + c, 0))
    npair = SSM_HEADS // 2
    return pl.pallas_call(
        _ssd_kernel,
        grid=(B, nc),
        in_specs=[rowb(CONV_DIM), rowb(SSM_INNER), rowb(SSM_HEADS),
                  pl.BlockSpec((SSM_HEADS, Q), lambda b, c: (0, b * nc + c)),
                  _full(convw.shape), _full(convb.shape), _full(dtb.shape), _full(dtbT.shape),
                  _full(alog.shape), _full(alogT.shape), _full(dskip.shape), _full(normw.shape)],
        out_specs=(rowb(SSM_INNER),
                   pl.BlockSpec((1, CONV_WIDTH - 1, CONV_DIM), lambda b, c: (b, 0, 0)),
                   pl.BlockSpec((1, npair, 2 * SSM_HEAD_DIM, SSM_STATE), lambda b, c: (b, 0, 0, 0))),
        out_shape=(jax.ShapeDtypeStruct((B * S, SSM_INNER), BF16),
                   jax.ShapeDtypeStruct((B, CONV_WIDTH - 1, CONV_DIM), F32),
                   jax.ShapeDtypeStruct((B, npair, 2 * SSM_HEAD_DIM, SSM_STATE), F32)),
        scratch_shapes=[pltpu.VMEM((Q + 8, CONV_DIM), F32),
                        pltpu.VMEM((npair, 2 * SSM_HEAD_DIM, SSM_STATE), F32)],
        compiler_params=_cparams("parallel", "arbitrary"),
        name="ssd_scan",
    )(xbc, z, dt, dtT, convw, convb, dtb, dtbT, alog, alogT, dskip, normw)


def _ssd_step_kernel(xbc_ref, z_ref, dt_ref, cstate_ref, hstate_ref, convw_ref, convb_ref, dtb_ref, alog_ref,
                     dskip_ref, normw_ref, y_ref, cout_ref, hout_ref):
    bt = xbc_ref.shape[0]
    P, N = SSM_HEAD_DIM, SSM_STATE
    new = xbc_ref[...]
    conv = convb_ref[...] + new * convw_ref[CONV_WIDTH - 1:CONV_WIDTH, :]
    for i in range(CONV_WIDTH - 1):
        conv = conv + cstate_ref[:, i * CONV_DIM:(i + 1) * CONV_DIM] * convw_ref[i:i + 1, :]
    cout_ref[:, 0:(CONV_WIDTH - 2) * CONV_DIM] = cstate_ref[:, CONV_DIM:(CONV_WIDTH - 1) * CONV_DIM]
    cout_ref[:, (CONV_WIDTH - 2) * CONV_DIM:] = new
    conv = _silu(conv)
    xs = conv[:, :SSM_INNER]
    bmat = conv[:, SSM_INNER:SSM_INNER + SSM_GROUPS * N]
    cmat = conv[:, SSM_INNER + SSM_GROUPS * N:]
    dt = _softplus(dt_ref[...] + dtb_ref[...])
    dec = jnp.exp(dt * (-jnp.exp(alog_ref[...])))
    H = SSM_HEADS
    HP = H * P
    exact = lax.Precision.HIGHEST
    eye_n = (lax.broadcasted_iota(jnp.int32, (N, N), 0) == lax.broadcasted_iota(jnp.int32, (N, N), 1)).astype(F32)
    x_t = jnp.concatenate(
        [lax.dot_general(eye_n, xs[:, c * N:(c + 1) * N], NT_DIMS, preferred_element_type=F32, precision=exact)
         for c in range(HP // N)], axis=0)
    rep = (lax.broadcasted_iota(jnp.int32, (HP, H), 0) // P == lax.broadcasted_iota(jnp.int32, (HP, H), 1)).astype(F32)
    per_head = jnp.concatenate([dt, dec, jnp.broadcast_to(dskip_ref[...], (8, H))], axis=0)
    cols = lax.dot_general(rep, per_head, NT_DIMS, preferred_element_type=F32, precision=exact)
    d_col = cols[:, 2 * bt:2 * bt + 1]
    first_group = lax.broadcasted_iota(jnp.int32, (HP, 1), 0) < HP // SSM_GROUPS
    lane_b = lax.broadcasted_iota(jnp.int32, (HP, bt), 1)
    y_t = jnp.zeros((HP, bt), F32)
    for b in range(bt):
        x_col = x_t[:, b:b + 1]
        b_sel = jnp.where(first_group, bmat[b:b + 1, 0:N], bmat[b:b + 1, N:2 * N])
        h_new = (hstate_ref[b].reshape(HP, N) * cols[:, bt + b:bt + b + 1]
                 + (x_col * cols[:, b:b + 1]) * b_sel)
        hout_ref[b] = h_new.reshape(H, P, N)
        c2 = jnp.concatenate([cmat[b:b + 1, 0:N], cmat[b:b + 1, N:2 * N], jnp.zeros((6, N), F32)], axis=0)
        yb = lax.dot_general(h_new.astype(BF16), c2.astype(BF16), NT_DIMS, preferred_element_type=F32)
        y_col = jnp.where(first_group, yb[:, 0:1], yb[:, 1:2]) + d_col * x_col
        y_t = jnp.where(lane_b == b, y_col, y_t)
    eye_b = (lax.broadcasted_iota(jnp.int32, (bt, bt), 0) == lax.broadcasted_iota(jnp.int32, (bt, bt), 1)).astype(F32)
    y = lax.dot_general(eye_b, y_t, NT_DIMS, preferred_element_type=F32, precision=exact)
    y_ref[...] = _gated_group_norm(y, z_ref[...], normw_ref[...]).astype(BF16)


def _ssd_step(xbc, z, dt, cstate, hstate, layer, sw, bt=8):
    convw, convb, dtb, _, alog, _, dskip, normw = sw
    Bs = xbc.shape[0]
    cw = (CONV_WIDTH - 1) * CONV_DIM
    rowb = lambda n: pl.BlockSpec((bt, n), lambda i: (i, 0))
    hshape = (bt, SSM_HEADS, SSM_HEAD_DIM, SSM_STATE)
    return pl.pallas_call(
        _ssd_step_kernel,
        grid=(Bs // bt,),
        in_specs=[rowb(CONV_DIM), rowb(SSM_INNER), rowb(SSM_HEADS),
                  pl.BlockSpec((None, bt, cw), lambda i: (layer, i, 0)),
                  pl.BlockSpec((None,) + hshape, lambda i: (layer, i, 0, 0, 0)),
                  _full(convw.shape), _full(convb.shape), _full(dtb.shape), _full(alog.shape),
                  _full(dskip.shape), _full(normw.shape)],
        out_specs=(rowb(SSM_INNER), rowb(cw), pl.BlockSpec(hshape, lambda i: (i, 0, 0, 0))),
        out_shape=(jax.ShapeDtypeStruct((Bs, SSM_INNER), BF16),
                   jax.ShapeDtypeStruct((Bs, cw), F32),
                   jax.ShapeDtypeStruct(hstate.shape[1:], F32)),
        compiler_params=_cparams("parallel"),
        name="ssd_step",
    )(xbc, z, dt, cstate, hstate, convw, convb, dtb, alog, dskip, normw)


def _lambda_value(lp, lam_init):
    d1 = jnp.sum(lp[0:1, :] * lp[1:2, :], axis=1, keepdims=True)
    d2 = jnp.sum(lp[2:3, :] * lp[3:4, :], axis=1, keepdims=True)
    return jnp.exp(d1) - jnp.exp(d2) + lam_init


def _head_norm(o, nw, lam_init):
    return o * lax.rsqrt(jnp.mean(o * o, axis=-1, keepdims=True) + NORM_EPS) * nw * (1.0 - lam_init)


def _attn_kernel(lp_ref, nw_ref, q_ref, k_ref, v_ref, o_ref, m_ref, l_ref, acc_ref, *, lam_init):
    tq = q_ref.shape[0]
    qi = pl.program_id(2)
    q = q_ref[...]
    lane = lax.broadcasted_iota(jnp.int32, q.shape, 1)
    zero = jnp.zeros_like(q)
    q_maps = (jnp.where(lane < ATT_QK_DIM, q, zero), jnp.where(lane >= ATT_QK_DIM, q, zero))
    m_ref[...] = jnp.full(m_ref.shape, -jnp.inf, F32)
    l_ref[...] = jnp.zeros(l_ref.shape, F32)
    acc_ref[...] = jnp.zeros(acc_ref.shape, F32)
    below_diag = (lax.broadcasted_iota(jnp.int32, (tq, tq), 1) <= lax.broadcasted_iota(jnp.int32, (tq, tq), 0))

    def block(j, diagonal):
        start = pl.multiple_of(j * tq, tq)
        k = k_ref[pl.ds(start, tq), :]
        v = v_ref[pl.ds(start, tq), :]
        for i in range(2):
            s = lax.dot_general(q_maps[i], k, NT_DIMS, preferred_element_type=F32)
            if diagonal:
                s = jnp.where(below_diag, s, -jnp.inf)
            chunks = [s[:, c * LANES:(c + 1) * LANES] for c in range(tq // LANES)]
            top = functools.reduce(jnp.maximum, chunks)
            m_old = m_ref[i]
            m_new = jnp.maximum(m_old, jnp.max(top, axis=1, keepdims=True))
            alpha = jnp.exp(m_old - m_new)
            ps = [jnp.exp(c - m_new) for c in chunks]
            l_ref[i] = alpha * l_ref[i] + functools.reduce(jnp.add, ps)
            p = jnp.concatenate(ps, axis=1).astype(BF16)
            acc_ref[i] = alpha * acc_ref[i] + jnp.dot(p, v, preferred_element_type=F32)
            m_ref[i] = m_new

    def full_block(j, carry):
        block(j, False)
        return carry

    lax.fori_loop(0, qi, full_block, 0)
    block(qi, True)
    lam = _lambda_value(lp_ref[...], lam_init)
    l1 = jnp.sum(l_ref[0], axis=1, keepdims=True)
    l2 = jnp.sum(l_ref[1], axis=1, keepdims=True)
    o = acc_ref[0] / l1 - lam * (acc_ref[1] / l2)
    o_ref[...] = _head_norm(o, nw_ref[...], lam_init).astype(BF16)


def _attention(q, k, v, lp, nw, lam_init, B, S):
    tq = min(512, S)
    nq = S // tq
    qspec = pl.BlockSpec((tq, ATT_V_DIM), lambda b, h, i: (b * nq + i, h))
    kspec = pl.BlockSpec((S, ATT_V_DIM), lambda b, h, i: (b, h))
    return pl.pallas_call(
        functools.partial(_attn_kernel, lam_init=lam_init),
        grid=(B, ATT_HEADS, nq),
        in_specs=[_full(lp.shape), _full(nw.shape), qspec, kspec, kspec],
        out_specs=qspec,
        out_shape=jax.ShapeDtypeStruct((B * S, ATT_COLS), BF16),
        scratch_shapes=[pltpu.VMEM((2, tq, LANES), F32), pltpu.VMEM((2, tq, LANES), F32),
                        pltpu.VMEM((2, tq, ATT_V_DIM), F32)],
        compiler_params=_cparams("parallel", "parallel", "arbitrary"),
        name="diff_attention",
    )(lp, nw, q, k, v)


def _decode_attn_kernel(pt_ref, lp_ref, nw_ref, q_ref, kn_ref, vn_ref, *refs, n_pages, lam_init):
    k_refs = refs[:n_pages]
    v_refs = refs[n_pages:2 * n_pages]
    o_ref = refs[2 * n_pages]
    R = 2 * ATT_HEADS
    page = k_refs[0].shape[2]
    r = lax.broadcasted_iota(jnp.int32, (R, ATT_COLS), 0)
    grp = lax.broadcasted_iota(jnp.int32, (R, ATT_COLS), 1) // ATT_QK_DIM
    target = jnp.where(r < ATT_HEADS, 2 * r, 2 * (r - ATT_HEADS) + 1)
    qf = jnp.where(grp == target, q_ref[0].astype(BF16).astype(F32), 0.0)
    qt = qf.astype(BF16)
    s = [jnp.dot(qt, k_ref[...].reshape(ATT_COLS, page).astype(BF16), preferred_element_type=F32)
         for k_ref in k_refs]
    s_new = jnp.sum(qf * kn_ref[0].astype(BF16).astype(F32), axis=1, keepdims=True)
    m = s_new
    for sj in s:
        m = jnp.maximum(m, jnp.max(sj, axis=1, keepdims=True))
    p = [jnp.exp(sj - m) for sj in s]
    p_new = jnp.exp(s_new - m)
    l = p_new
    for pj in p:
        l = l + jnp.sum(pj, axis=1, keepdims=True)
    inv_l = 1.0 / l
    lam = _lambda_value(lp_ref[...], lam_init)
    diff = lambda t: t - lam * pltpu.roll(t, ATT_HEADS, 0)
    a = [diff(pj * inv_l).astype(BF16) for pj in p]
    a_new = diff(jnp.broadcast_to(p_new * inv_l, (R, 128)))[:, 0:1].astype(BF16).astype(F32)
    row = lax.broadcasted_iota(jnp.int32, (R, ATT_V_DIM), 0)
    out = a_new * vn_ref[0].astype(BF16).astype(F32)
    for h in range(ATT_HEADS):
        acc = jnp.zeros((R, ATT_V_DIM), F32)
        for aj, v_ref in zip(a, v_refs):
            v_h = v_ref[pl.ds(h, page, stride=ATT_HEADS), :].astype(BF16)
            acc = acc + jnp.dot(aj, v_h, preferred_element_type=F32)
        out = out + jnp.where(row == h, acc, 0.0)
    o_ref[0] = _head_norm(out, nw_ref[...], lam_init)


def _decode_attention(pt, layer, q, k_new, v_new, cache_k, cache_v, lp, nw, lam_init):
    Bs, n_pages = pt.shape
    R = 2 * ATT_HEADS
    qspec = pl.BlockSpec((1, 1, ATT_COLS), lambda b, pt: (b, 0, 0))
    vspec = pl.BlockSpec((1, R, ATT_V_DIM), lambda b, pt: (b, 0, 0))
    kpages = [pl.BlockSpec((None, None) + cache_k.shape[2:], lambda b, pt, j=j: (layer, pt[b, j], 0, 0, 0))
              for j in range(n_pages)]
    vpages = [pl.BlockSpec((None, None) + cache_v.shape[2:], lambda b, pt, j=j: (layer, pt[b, j], 0, 0))
              for j in range(n_pages)]
    grid_spec = pltpu.PrefetchScalarGridSpec(
        num_scalar_prefetch=1,
        grid=(Bs,),
        in_specs=[pl.BlockSpec(lp.shape, lambda b, pt: (0, 0)), pl.BlockSpec(nw.shape, lambda b, pt: (0, 0)),
                  qspec, qspec, vspec] + kpages + vpages,
        out_specs=vspec,
    )
    return pl.pallas_call(
        functools.partial(_decode_attn_kernel, n_pages=n_pages, lam_init=lam_init),
        grid_spec=grid_spec,
        out_shape=jax.ShapeDtypeStruct((Bs, R, ATT_V_DIM), F32),
        compiler_params=_cparams("parallel"),
        name="decode_attention",
    )(pt, lp, nw, q, k_new, v_new, *([cache_k] * n_pages), *([cache_v] * n_pages))


def _proj_ln_kernel(*refs, n_in, alpha):
    a_refs = refs[:n_in]
    w_refs = refs[n_in:2 * n_in]
    x_ref, g_ref, b_ref, o_ref = refs[2 * n_in:]
    acc = alpha * x_ref[...]
    for a_ref, w_ref in zip(a_refs, w_refs):
        acc = acc + jnp.dot(a_ref[...].astype(BF16), w_ref[...], preferred_element_type=F32)
    o_ref[...] = _layer_norm(acc, g_ref[...], b_ref[...])


def _proj_ln(acts, ws, x, g, b, alpha, tm):
    T, D = x.shape
    row = lambda n: pl.BlockSpec((tm, n), lambda i: (i, 0))
    return pl.pallas_call(
        functools.partial(_proj_ln_kernel, n_in=len(acts), alpha=alpha),
        grid=(T // tm,),
        in_specs=[row(a.shape[1]) for a in acts] + [_full(w.shape) for w in ws] + [row(D), _full(g.shape), _full(b.shape)],
        out_specs=row(D),
        out_shape=jax.ShapeDtypeStruct((T, D), F32),
        compiler_params=_cparams("parallel"),
        name="proj_ln",
    )(*acts, *ws, x, g, b)


def _matmul_kernel(x_ref, w_ref, *o_refs):
    y = jnp.dot(x_ref[...].astype(BF16), w_ref[...], preferred_element_type=F32)
    for o_ref in o_refs:
        o_ref[...] = y.astype(o_ref.dtype)


def _matmul(x, w, out_dtypes, tm, tn):
    M, K = x.shape
    N = w.shape[1]
    ospec = pl.BlockSpec((tm, tn), lambda i, j: (i, j))
    return pl.pallas_call(
        _matmul_kernel,
        grid=(M // tm, N // tn),
        in_specs=[pl.BlockSpec((tm, K), lambda i, j: (i, 0)), pl.BlockSpec((K, tn), lambda i, j: (0, j))],
        out_specs=tuple(ospec for _ in out_dtypes),
        out_shape=tuple(jax.ShapeDtypeStruct((M, N), d) for d in out_dtypes),
        compiler_params=_cparams("parallel", "parallel"),
        name="matmul",
    )(x, w)


def _softmax_rows(s):
    m = jnp.max(s, axis=1, keepdims=True)
    p = jnp.exp(s - m)
    return p / jnp.sum(p, axis=1, keepdims=True)


def _cross_kernel(x_ref, wq_ref, mk_ref, mv_ref, wo_ref, g_ref, b_ref, o_ref, *, alpha):
    x = x_ref[...]
    D = x.shape[1]
    dh = D // MEM_HEADS
    q = jnp.dot(x.astype(BF16), wq_ref[...], preferred_element_type=F32).astype(BF16)
    outs = []
    for h in range(MEM_HEADS):
        sl = slice(h * dh, (h + 1) * dh)
        s = lax.dot_general(q[:, sl], mk_ref[:, sl], NT_DIMS, preferred_element_type=F32) * (dh ** -0.5)
        outs.append(jnp.dot(_softmax_rows(s).astype(BF16), mv_ref[:, sl], preferred_element_type=F32).astype(BF16))
    o = jnp.concatenate(outs, axis=1)
    y = alpha * x + jnp.dot(o, wo_ref[...], preferred_element_type=F32)
    o_ref[...] = _layer_norm(y, g_ref[...], b_ref[...])


def _cross_attention(x, wq, mk, mv, wo, g, b, alpha, B, S, n_mem, tm):
    T, D = x.shape
    nt = S // tm
    row = pl.BlockSpec((tm, D), lambda bi, i: (bi * nt + i, 0))
    mem = pl.BlockSpec((n_mem, D), lambda bi, i: (bi, 0))
    return pl.pallas_call(
        functools.partial(_cross_kernel, alpha=alpha),
        grid=(B, nt),
        in_specs=[row, _full(wq.shape), mem, mem, _full(wo.shape), _full(g.shape), _full(b.shape)],
        out_specs=row,
        out_shape=jax.ShapeDtypeStruct((T, D), F32),
        compiler_params=_cparams("parallel", "parallel"),
        name="cross_attention",
    )(x, wq, mk, mv, wo, g, b)


def _cross_decode_kernel(q_ref, mk_ref, mv_ref, o_ref):
    R = q_ref.shape[1] // 2
    rows_kv = mk_ref.shape[0]
    dh = 2 * LANES
    part = lax.dot_general(q_ref[0].astype(BF16), mk_ref[...].astype(BF16), NT_DIMS,
                           preferred_element_type=F32)
    s = (part[:R] + pltpu.roll(part[R:], rows_kv - MEM_HEADS, 1)) * (dh ** -0.5)
    own = (lax.broadcasted_iota(jnp.int32, (R, rows_kv), 1) & (R - 1)) == lax.broadcasted_iota(jnp.int32, (R, rows_kv), 0)
    p = _softmax_rows(jnp.where(own, s, -jnp.inf))
    a = jnp.concatenate([p, pltpu.roll(p, MEM_HEADS, 1)], axis=0).astype(BF16)
    o_ref[0] = jnp.dot(a, mv_ref[...].astype(BF16), preferred_element_type=F32)


def _cross_decode(q, mem_k, mem_v, layer):
    Bs, R2, _ = q.shape
    tok = pl.BlockSpec((1, R2, LANES), lambda b: (b, 0, 0))
    mem = pl.BlockSpec((None, None) + mem_k.shape[2:], lambda b: (layer, b, 0, 0))
    return pl.pallas_call(
        _cross_decode_kernel,
        grid=(Bs,),
        in_specs=[tok, mem, mem],
        out_specs=tok,
        out_shape=jax.ShapeDtypeStruct((Bs, R2, LANES), F32),
        compiler_params=_cparams("parallel"),
        name="cross_decode",
    )(q, mem_k, mem_v)


def _ffn_kernel(x_ref, wg_ref, wu_ref, wd_ref, g_ref, b_ref, o_ref, acc_ref, *, alpha):
    f = pl.program_id(1)
    x = x_ref[...]
    xb = x.astype(BF16)

    @pl.when(f == 0)
    def _():
        acc_ref[...] = alpha * x

    gate = jnp.dot(xb, wg_ref[...], preferred_element_type=F32)
    up = jnp.dot(xb, wu_ref[...], preferred_element_type=F32)
    acc_ref[...] += jnp.dot((_silu(gate) * up).astype(BF16), wd_ref[...], preferred_element_type=F32)

    @pl.when(f == pl.num_programs(1) - 1)
    def _():
        o_ref[...] = _layer_norm(acc_ref[...], g_ref[...], b_ref[...])


def _ffn(x, w_gu, w_down, g, b, alpha, tm, tf):
    T, D = x.shape
    F = w_down.shape[0]
    nf = F // tf
    row = pl.BlockSpec((tm, D), lambda i, f: (i, 0))
    return pl.pallas_call(
        functools.partial(_ffn_kernel, alpha=alpha),
        grid=(T // tm, nf),
        in_specs=[row, pl.BlockSpec((D, tf), lambda i, f: (0, f)), pl.BlockSpec((D, tf), lambda i, f: (0, nf + f)),
                  pl.BlockSpec((tf, D), lambda i, f: (f, 0)), _full(g.shape), _full(b.shape)],
        out_specs=row,
        out_shape=jax.ShapeDtypeStruct((T, D), F32),
        scratch_shapes=[pltpu.VMEM((tm, D), F32)],
        compiler_params=_cparams("parallel", "arbitrary"),
        name="ffn",
    )(x, w_gu, w_gu, w_down, g, b)


def _router_kernel(x_ref, wr_ref, gate_ref, top_i_ref, top_g_ref):
    logits = jnp.dot(x_ref[...].astype(BF16), wr_ref[...], preferred_element_type=F32)
    lane = lax.broadcasted_iota(jnp.int32, logits.shape, 1).astype(F32)
    none = float(N_EXPERTS)
    m1 = jnp.max(logits, axis=1, keepdims=True)
    i1 = jnp.min(jnp.where(logits == m1, lane, none), axis=1, keepdims=True)
    rest = jnp.where(lane == i1, -jnp.inf, logits)
    m2 = jnp.max(rest, axis=1, keepdims=True)
    i2 = jnp.min(jnp.where(rest == m2, lane, none), axis=1, keepdims=True)
    e = jnp.exp(m2 - m1)
    den = 1.0 + e
    gate_ref[...] = jnp.where(lane == i1, 1.0 / den, 0.0) + jnp.where(lane == i2, e / den, 0.0)
    first = lax.broadcasted_iota(jnp.int32, top_i_ref.shape, 1) == 0
    top_i_ref[...] = jnp.where(first, i1, i2).astype(jnp.int32)
    top_g_ref[...] = jnp.where(first, 1.0 / den, e / den)


def _router(x, wr, tm):
    T, D = x.shape
    top = pl.BlockSpec((tm, 2), lambda i: (i, 0))
    return pl.pallas_call(
        _router_kernel,
        grid=(T // tm,),
        in_specs=[pl.BlockSpec((tm, D), lambda i: (i, 0)), _full(wr.shape)],
        out_specs=(pl.BlockSpec((tm, N_EXPERTS), lambda i: (i, 0)), top, top),
        out_shape=(jax.ShapeDtypeStruct((T, N_EXPERTS), F32), jax.ShapeDtypeStruct((T, 2), jnp.int32),
                   jax.ShapeDtypeStruct((T, 2), F32)),
        compiler_params=_cparams("parallel"),
        name="router",
    )(x, wr)


def _moe_plan(top_i, tm_e, n_tiles):
    T = top_i.shape[0]
    chosen = (top_i[:, :, None] == jnp.arange(N_EXPERTS, dtype=jnp.int32)[None, None, :]).any(axis=1)
    chosen = chosen.astype(jnp.int32)
    count = chosen.sum(axis=0)
    rank = jnp.cumsum(chosen, axis=0) - chosen
    tiles = (count + tm_e - 1) // tm_e
    tile_end = jnp.cumsum(tiles)
    row_start = (tile_end - tiles) * tm_e
    dest = jnp.take_along_axis(row_start[None, :] + rank, top_i, axis=1).astype(jnp.int32)
    token = jnp.broadcast_to(jnp.arange(T, dtype=jnp.int32)[:, None], (T, 2))
    src = jnp.zeros((n_tiles * tm_e,), jnp.int32).at[dest.reshape(-1)].set(token.reshape(-1))
    n_used = tile_end[-1]
    tile = jnp.arange(n_tiles, dtype=jnp.int32)
    tile_expert = (tile_end[None, :] <= jnp.minimum(tile, n_used - 1)[:, None]).sum(axis=1).astype(jnp.int32)
    return src, dest, tile_expert, n_used.reshape(1).astype(jnp.int32)


def _row_copy(src_hbm, src_row, dst_ref, dst_row, sem):
    return pltpu.make_async_copy(src_hbm.at[pl.ds(src_row, 1), :], dst_ref.at[pl.ds(dst_row, 1), :], sem)


def _moe_ffn_kernel(te_ref, nv_ref, src_ref, x_hbm, wg_ref, wu_ref, wd_ref, y_ref, xrows_ref, xb_ref, sem):
    tm_e = xrows_ref.shape[1]
    i = pl.program_id(0)
    f = pl.program_id(1)
    n_used = nv_ref[0]
    slot = lax.rem(i, 2)

    def start_gather(tile, slot):
        def body(r8, carry):
            for u in range(ROW_COPY_UNROLL):
                r = r8 * ROW_COPY_UNROLL + u
                _row_copy(x_hbm, src_ref[tile * tm_e + r], xrows_ref.at[slot], r, sem.at[slot]).start()
            return carry
        lax.fori_loop(0, tm_e // ROW_COPY_UNROLL, body, 0)

    @pl.when(f == 0)
    def _():
        @pl.when(i == 0)
        def _():
            start_gather(0, 0)

        @pl.when(i < n_used)
        def _():
            pltpu.make_async_copy(x_hbm.at[pl.ds(0, tm_e), :], xrows_ref.at[slot], sem.at[slot]).wait()
            xb_ref[...] = xrows_ref[slot].astype(BF16)

        @pl.when(i + 1 < n_used)
        def _():
            start_gather(i + 1, 1 - slot)

    @pl.when(i < n_used)
    def _():
        xb = xb_ref[...]
        gate = jnp.dot(xb, wg_ref[...], preferred_element_type=F32)
        up = jnp.dot(xb, wu_ref[...], preferred_element_type=F32)
        part = jnp.dot((_silu(gate) * up).astype(BF16), wd_ref[...], preferred_element_type=F32)

        @pl.when(f == 0)
        def _():
            y_ref[...] = part

        @pl.when(f > 0)
        def _():
            y_ref[...] += part

    @pl.when(jnp.logical_and(i >= n_used, f == 0))
    def _():
        y_ref[...] = jnp.zeros(y_ref.shape, F32)


def _moe_ffn(x, src, tile_expert, n_used, w_gu, w_down, tm_e, tf):
    T, D = x.shape
    E, F, _ = w_down.shape
    nf = F // tf
    n_tiles = tile_expert.shape[0]
    grid_spec = pltpu.PrefetchScalarGridSpec(
        num_scalar_prefetch=3,
        grid=(n_tiles, nf),
        in_specs=[pl.BlockSpec(memory_space=pl.ANY),
                  pl.BlockSpec((None, D, tf), lambda i, f, te, nv, src: (te[i], 0, f)),
                  pl.BlockSpec((None, D, tf), lambda i, f, te, nv, src: (te[i], 0, nf + f)),
                  pl.BlockSpec((None, tf, D), lambda i, f, te, nv, src: (te[i], f, 0))],
        out_specs=pl.BlockSpec((tm_e, D), lambda i, f, te, nv, src: (i, 0)),
        scratch_shapes=[pltpu.VMEM((2, tm_e, D), F32), pltpu.VMEM((tm_e, D), BF16), pltpu.SemaphoreType.DMA((2,))],
    )
    return pl.pallas_call(
        _moe_ffn_kernel,
        grid_spec=grid_spec,
        out_shape=jax.ShapeDtypeStruct((n_tiles * tm_e, D), F32),
        compiler_params=_cparams("arbitrary", "arbitrary"),
        name="moe_ffn",
    )(tile_expert, n_used, src, x, w_gu, w_gu, w_down)


def _moe_combine_kernel(d1_ref, d2_ref, x_ref, gate_ref, g_ref, b_ref, y_hbm, o_ref, yrows_ref, sem, *, alpha):
    tm = yrows_ref.shape[2]
    i = pl.program_id(0)
    slot = lax.rem(i, 2)

    def start_gather(tile, slot):
        def body(r8, carry):
            for u in range(ROW_COPY_UNROLL):
                r = r8 * ROW_COPY_UNROLL + u
                _row_copy(y_hbm, d1_ref[tile * tm + r], yrows_ref.at[slot, 0], r, sem.at[slot]).start()
                _row_copy(y_hbm, d2_ref[tile * tm + r], yrows_ref.at[slot, 1], r, sem.at[slot]).start()
            return carry
        lax.fori_loop(0, tm // ROW_COPY_UNROLL, body, 0)

    @pl.when(i == 0)
    def _():
        start_gather(0, 0)

    for k in range(2):
        pltpu.make_async_copy(y_hbm.at[pl.ds(0, tm), :], yrows_ref.at[slot, k], sem.at[slot]).wait()

    @pl.when(i + 1 < pl.num_programs(0))
    def _():
        start_gather(i + 1, 1 - slot)

    y = yrows_ref[slot, 0] * gate_ref[:, 0:1] + yrows_ref[slot, 1] * gate_ref[:, 1:2]
    o_ref[...] = _layer_norm(alpha * x_ref[...] + y, g_ref[...], b_ref[...])


def _moe_combine(x, y_rows, dest, top_g, g, b, alpha, tm):
    T, D = x.shape
    row = pl.BlockSpec((tm, D), lambda i, d1, d2: (i, 0))
    grid_spec = pltpu.PrefetchScalarGridSpec(
        num_scalar_prefetch=2,
        grid=(T // tm,),
        in_specs=[row, pl.BlockSpec((tm, 2), lambda i, d1, d2: (i, 0)),
                  pl.BlockSpec(g.shape, lambda i, d1, d2: (0, 0)), pl.BlockSpec(b.shape, lambda i, d1, d2: (0, 0)),
                  pl.BlockSpec(memory_space=pl.ANY)],
        out_specs=row,
        scratch_shapes=[pltpu.VMEM((2, 2, tm, D), F32), pltpu.SemaphoreType.DMA((2,))],
    )
    return pl.pallas_call(
        functools.partial(_moe_combine_kernel, alpha=alpha),
        grid_spec=grid_spec,
        out_shape=jax.ShapeDtypeStruct((T, D), F32),
        compiler_params=_cparams("arbitrary"),
        name="moe_combine",
    )(dest[:, 0], dest[:, 1], x, top_g, g, b, y_rows)


def _moe_kernel(x_ref, gate_ref, wg_ref, wu_ref, wd_ref, g_ref, b_ref, o_ref, acc_ref, *, alpha):
    e = pl.program_id(1)
    f = pl.program_id(2)
    x = x_ref[...]
    xb = x.astype(BF16)

    @pl.when(jnp.logical_and(e == 0, f == 0))
    def _():
        acc_ref[...] = alpha * x

    gates = gate_ref[...]
    lane = lax.broadcasted_iota(jnp.int32, gates.shape, 1)
    ge = jnp.sum(jnp.where(lane == e, gates, 0.0), axis=1, keepdims=True)
    gate = jnp.dot(xb, wg_ref[0], preferred_element_type=F32)
    up = jnp.dot(xb, wu_ref[0], preferred_element_type=F32)
    acc_ref[...] += jnp.dot((_silu(gate) * up).astype(BF16), wd_ref[0], preferred_element_type=F32) * ge

    @pl.when(jnp.logical_and(e == pl.num_programs(1) - 1, f == pl.num_programs(2) - 1))
    def _():
        o_ref[...] = _layer_norm(acc_ref[...], g_ref[...], b_ref[...])


def _moe(x, gates, w_gu, w_down, g, b, alpha, tm, tf):
    T, D = x.shape
    E, F, _ = w_down.shape
    nf = F // tf
    row = pl.BlockSpec((tm, D), lambda i, e, f: (i, 0))
    return pl.pallas_call(
        functools.partial(_moe_kernel, alpha=alpha),
        grid=(T // tm, E, nf),
        in_specs=[row, pl.BlockSpec((tm, E), lambda i, e, f: (i, 0)),
                  pl.BlockSpec((1, D, tf), lambda i, e, f: (e, 0, f)),
                  pl.BlockSpec((1, D, tf), lambda i, e, f: (e, 0, nf + f)),
                  pl.BlockSpec((1, tf, D), lambda i, e, f: (e, f, 0)), _full(g.shape), _full(b.shape)],
        out_specs=row,
        out_shape=jax.ShapeDtypeStruct((T, D), F32),
        scratch_shapes=[pltpu.VMEM((tm, D), F32)],
        compiler_params=_cparams("parallel", "arbitrary", "arbitrary"),
        name="moe",
    )(x, gates, w_gu, w_gu, w_down, g, b)


def _rope_tables(pos):
    half = ATT_QK_DIM // 2
    inv = ROPE_THETA ** (-jnp.arange(0, ATT_QK_DIM, 2, dtype=F32) / ATT_QK_DIM)
    ang = pos.astype(F32)[:, None] * inv[None, :]
    cos, sin = jnp.cos(ang), jnp.sin(ang)
    return jnp.concatenate([cos] * 4, axis=1), jnp.concatenate([-sin, sin, -sin, sin], axis=1)


def _row_tile(n, target):
    t = min(n, target)
    while n % t:
        t //= 2
    return t


def kernel(x_prompt, x_sample, cache_attn_k, cache_attn_v, cache_mem_k, cache_mem_v, state_conv, state_ssm,
           page_table, mem_prompt, ln_g, ln_b, w_in, conv_w, conv_b, dt_bias, a_log, d_skip, ssm_norm_w,
           lambda_params, attn_norm_w, w_out, w_cq, w_ckv, w_co, w_ffn_gu, w_ffn_down, w_router, w_exp_gu,
           w_exp_down):
    B, S, D = x_prompt.shape
    Bs = x_sample.shape[0]
    depth = w_in.shape[0]
    n_phys, page = cache_attn_k.shape[1], cache_attn_k.shape[2]
    n_pages = page_table.shape[1]
    past_len = n_pages * page
    n_mem = mem_prompt.shape[1]
    alpha = (2 * depth) ** 0.25
    assert S % SSD_CHUNK == 0 and x_sample.shape[1] == 1

    tm_p = _row_tile(S, 512)
    tm_s = _row_tile(Bs, 128)
    cos_p, sin_p = _rope_tables(jnp.arange(S, dtype=jnp.int32))
    cos_s, sin_s = _rope_tables(jnp.full((tm_s,), past_len, jnp.int32))

    hp = x_prompt.reshape(B * S, D)
    hs = x_sample.reshape(Bs, D)
    mem = mem_prompt.reshape(B * n_mem, D)
    cache_k = cache_attn_k.transpose(0, 1, 3, 4, 2)
    cache_v = cache_attn_v.reshape(depth, n_phys, page * ATT_HEADS, ATT_V_DIM)
    dh = D // MEM_HEADS
    mem_rows = lambda t: (t.reshape(depth, Bs, n_mem, MEM_HEADS, dh // LANES, LANES).swapaxes(3, 4)
                          .reshape(depth, Bs, n_mem * dh // LANES * MEM_HEADS, LANES))
    cmem_k, cmem_v = mem_rows(cache_mem_k), mem_rows(cache_mem_v)
    cstate = state_conv.reshape(depth, Bs, (CONV_WIDTH - 1) * CONV_DIM)
    pad_heads = lambda t: jnp.pad(t, [(0, 0)] * (t.ndim - 2) + [(0, 2 * ATT_HEADS - t.shape[-2]), (0, 0)])

    splits = (ATT_COLS, 2 * ATT_COLS, 3 * ATT_COLS, 3 * ATT_COLS + SSM_INNER, 3 * ATT_COLS + SSM_INNER + CONV_DIM)
    outs = {n: [] for n in ("kp", "vp", "mkp", "mvp", "cp", "sp", "ks", "vs", "cs", "ss")}
    for l in range(depth):
        lam_init = 0.8 - 0.6 * math.exp(-0.3 * l)
        wl = w_in[l].astype(BF16)
        wq, wk, wv, wz, wxbc, wdt = (wl[:, a:b] for a, b in zip((0,) + splits, splits + (wl.shape[1],)))
        w_proj = (wq, wk, wv, wz, wxbc, wdt, wdt.T)
        sw = (conv_w[l], conv_b[l][None, :], dt_bias[l][None, :], dt_bias[l][:, None], a_log[l][None, :],
              a_log[l][:, None], d_skip[l][None, :], ssm_norm_w[l][None, :])
        lp = lambda_params[l]
        nw = attn_norm_w[l][None, :]
        wo = w_out[l].astype(BF16)
        wo_att, wo_ssm = wo[:ATT_COLS], wo[ATT_COLS:]
        wcq = w_cq[l].astype(BF16)
        wckv = w_ckv[l].astype(BF16)
        wco = w_co[l].astype(BF16)
        g0, g1, g2 = (ln_g[l, i][None, :] for i in range(3))
        b0, b1, b2 = (ln_b[l, i][None, :] for i in range(3))

        q, kf, kb, vf, vb, z, xbc, dt, dtT = _in_proj(hp, cos_p, sin_p, w_proj, tm_p, S // tm_p)
        att = _attention(q, kb, vb, lp, nw, lam_init, B, S)
        y, conv_new, h_last = _ssd(xbc, z, dt, dtT, sw, B, S)
        hp = _proj_ln((att, y), (wo_att, wo_ssm), hp, g0, b0, alpha, tm_p)
        outs["kp"].append(kf.reshape(B, S, 2 * ATT_HEADS, ATT_QK_DIM))
        outs["vp"].append(vf.reshape(B, S, ATT_HEADS, ATT_V_DIM))
        outs["cp"].append(conv_new)
        outs["sp"].append(h_last.reshape(B, SSM_HEADS, SSM_HEAD_DIM, SSM_STATE))
        mkv_f, mkv_b = _matmul(mem, wckv, (F32, BF16), _row_tile(B * n_mem, 512), D)
        outs["mkp"].append(mkv_f[:, :D].reshape(B, n_mem, MEM_HEADS, D // MEM_HEADS))
        outs["mvp"].append(mkv_f[:, D:].reshape(B, n_mem, MEM_HEADS, D // MEM_HEADS))
        hp = _cross_attention(hp, wcq, mkv_b[:, :D], mkv_b[:, D:], wco, g1, b1, alpha, B, S, n_mem, tm_p)

        q, kf, _, vf, _, z, xbc, dt, _ = _in_proj(hs, cos_s, sin_s, w_proj, tm_s, 1)
        att = _decode_attention(page_table, l, q.astype(F32).reshape(Bs, 1, ATT_COLS), kf.reshape(Bs, 1, ATT_COLS),
                                pad_heads(vf.reshape(Bs, ATT_HEADS, ATT_V_DIM)), cache_k, cache_v, lp, nw, lam_init)
        att = att[:, :ATT_HEADS].reshape(Bs, ATT_COLS)
        y, conv_new, h_new = _ssd_step(xbc, z, dt, cstate, state_ssm, l, sw)
        hs = _proj_ln((att, y), (wo_att, wo_ssm), hs, g0, b0, alpha, tm_s)
        outs["ks"].append(kf.reshape(Bs, 1, 2 * ATT_HEADS, ATT_QK_DIM))
        outs["vs"].append(vf.reshape(Bs, 1, ATT_HEADS, ATT_V_DIM))
        outs["cs"].append(conv_new.reshape(Bs, CONV_WIDTH - 1, CONV_DIM))
        outs["ss"].append(h_new)
        (qc,) = _matmul(hs, wcq, (F32,), tm_s, D)
        q_rows = pad_heads(qc.reshape(Bs, MEM_HEADS, dh // LANES, LANES).swapaxes(1, 2))
        oc = _cross_decode(q_rows.reshape(Bs, -1, LANES), cmem_k, cmem_v, l)
        oc = oc.reshape(Bs, dh // LANES, 2 * ATT_HEADS, LANES)[:, :, :MEM_HEADS].swapaxes(1, 2).reshape(Bs, D)
        hs = _proj_ln((oc,), (wco,), hs, g1, b1, alpha, tm_s)

        if l % 2 == 0:
            wgu = w_ffn_gu[l // 2].astype(BF16)
            wd = w_ffn_down[l // 2].astype(BF16)
            tf = wd.shape[0] // 2
            hp = _ffn(hp, wgu, wd, g2, b2, alpha, tm_p, tf)
            hs = _ffn(hs, wgu, wd, g2, b2, alpha, tm_s, tf)
        else:
            wr = w_router[l // 2].astype(BF16)
            wgu = w_exp_gu[l // 2].astype(BF16)
            wd = w_exp_down[l // 2].astype(BF16)
            tf = wd.shape[1] // 2
            _, top_i, top_g = _router(hp, wr, tm_p)
            tm_e = _row_tile(2 * B * S, MOE_ROW_TILE)
            src, dest, tile_expert, n_used = _moe_plan(top_i, tm_e, 2 * B * S // tm_e + N_EXPERTS)
            y_rows = _moe_ffn(hp, src, tile_expert, n_used, wgu, wd, tm_e, tf)
            hp = _moe_combine(hp, y_rows, dest, top_g, g2, b2, alpha, _row_tile(B * S, 256))
            hs = _moe(hs, _router(hs, wr, tm_s)[0], wgu, wd, g2, b2, alpha, tm_s, tf)

    st = {n: jnp.stack(v) for n, v in outs.items()}
    return (hp.reshape(B, S, D), hs.reshape(Bs, 1, D), st["kp"], st["vp"], st["mkp"], st["mvp"], st["cp"],
            st["sp"], st["ks"], st["vs"], st["cs"], st["ss"])
```

```python
import functools
import math

import jax
import jax.numpy as jnp
from jax import lax
from jax.experimental import pallas as pl
from jax.experimental.pallas import tpu as pltpu

F32 = jnp.float32
BF16 = jnp.bfloat16

ATT_HEADS = 4
ATT_V_DIM = 128
ATT_QK_DIM = 64
ATT_COLS = 512
SSM_INNER = 512
SSM_HEAD_DIM = 64
SSM_HEADS = 8
SSM_GROUPS = 2
SSM_STATE = 128
CONV_WIDTH = 4
CONV_DIM = 1024
SSD_CHUNK = 128
MEM_HEADS = 4
N_EXPERTS = 8
ROPE_THETA = 10000.0
LN_EPS = 1e-5
NORM_EPS = 1e-5
QK_SCALE = ATT_QK_DIM ** -0.5

LANES = 128
MOE_ROW_TILE = 512
ROW_COPY_UNROLL = 8
VMEM_LIMIT_BYTES = 56 * 1024 * 1024
NT_DIMS = (((1,), (1,)), ((), ()))


def _cparams(*sem):
    return pltpu.CompilerParams(dimension_semantics=sem, vmem_limit_bytes=VMEM_LIMIT_BYTES)


def _silu(x):
    return x / (1.0 + jnp.exp(-x))


def _softplus(x):
    return jnp.maximum(x, 0.0) + jnp.log1p(jnp.exp(-jnp.abs(x)))


def _layer_norm(y, g, b):
    mu = jnp.mean(y, axis=-1, keepdims=True)
    yc = y - mu
    var = jnp.mean(yc * yc, axis=-1, keepdims=True)
    return yc * lax.rsqrt(var + LN_EPS) * g + b


def _full(shape):
    return pl.BlockSpec(shape, lambda *_: (0,) * len(shape))


def _in_proj_kernel(x_ref, cos_ref, sin_ref, wq_ref, wk_ref, wv_ref, wz_ref, wxbc_ref, wdt_ref, wdtT_ref, *rest,
                    cache_layout):
    q_ref, kf_ref, kb_ref, vf_ref, vb_ref, z_ref, xbc_ref, dt_ref, dtT_ref = rest[-9:]
    tm = x_ref.shape[0]
    xb = x_ref[...].astype(BF16)
    cos = jnp.concatenate([cos_ref[...]] * 4, axis=1)
    sin = jnp.concatenate([sin_ref[...]] * 4, axis=1)
    lane = lax.broadcasted_iota(jnp.int32, (tm, ATT_COLS), 1)
    first_half = (lane & (ATT_QK_DIM - 1)) < (ATT_QK_DIM // 2)

    def rope(t):
        partner = jnp.where(first_half, pltpu.roll(t, ATT_COLS - ATT_QK_DIM // 2, 1),
                            pltpu.roll(t, ATT_QK_DIM // 2, 1))
        return t * cos + partner * sin

    q = rope(jnp.dot(xb, wq_ref[...], preferred_element_type=F32))
    q_ref[...] = (q * QK_SCALE).astype(BF16)
    k = rope(jnp.dot(xb, wk_ref[...], preferred_element_type=F32))
    kb_ref[...] = k.astype(BF16)
    v = jnp.dot(xb, wv_ref[...], preferred_element_type=F32)
    vb_ref[...] = v.astype(BF16)
    if cache_layout:
        n_prev = kf_ref.shape[0] - 1
        if n_prev:
            kf_ref[0:n_prev] = rest[0][...]
            vf_ref[0:n_prev] = rest[1][...]
        kf_ref[n_prev] = k.T
        for h in range(ATT_HEADS):
            vf_ref[n_prev, pl.ds(h, tm, stride=ATT_HEADS), :] = v[:, h * ATT_V_DIM:(h + 1) * ATT_V_DIM]
    else:
        kf_ref[...] = k
        vf_ref[...] = v
    z_ref[...] = jnp.dot(xb, wz_ref[...], preferred_element_type=F32)
    xbc_ref[...] = jnp.dot(xb, wxbc_ref[...], preferred_element_type=F32)
    dt_ref[...] = jnp.dot(xb, wdt_ref[...], preferred_element_type=F32)
    dtT_ref[...] = lax.dot_general(wdtT_ref[...], xb, NT_DIMS, preferred_element_type=F32)


def _in_proj(x, cos_t, sin_t, w, tm, n_pos_blocks, cache=None):
    T, D = x.shape
    wq, wk, wv, wz, wxbc, wdt, wdtT = w
    row = lambda n: pl.BlockSpec((tm, n), lambda i: (i, 0))
    pos = pl.BlockSpec((tm, 128), lambda i: (i % n_pos_blocks, 0))
    kv_shape = jax.ShapeDtypeStruct((T, ATT_COLS), F32)
    k_shape, v_shape, k_spec, v_spec, prev, prev_specs = kv_shape, kv_shape, row(ATT_COLS), row(ATT_COLS), (), []
    if cache is not None:
        (prev_kv,) = cache
        S = n_pos_blocks * tm
        k_block = lambda n: pl.BlockSpec((n, None, ATT_COLS, tm), lambda i: (0, i // n_pos_blocks, 0, i % n_pos_blocks))
        v_block = lambda n: pl.BlockSpec((n, ATT_HEADS * tm, ATT_V_DIM), lambda i: (0, i, 0))
        n_prev = 0
        if prev_kv is not None:
            prev = tuple(prev_kv)
            n_prev = prev[0].shape[0]
            prev_specs = [k_block(n_prev), v_block(n_prev)]
        k_shape = jax.ShapeDtypeStruct((n_prev + 1, T // S, ATT_COLS, S), F32)
        v_shape = jax.ShapeDtypeStruct((n_prev + 1, ATT_HEADS * T, ATT_V_DIM), F32)
        k_spec, v_spec = k_block(n_prev + 1), v_block(n_prev + 1)
    out_shape = (
        jax.ShapeDtypeStruct((T, ATT_COLS), BF16),
        k_shape,
        jax.ShapeDtypeStruct((T, ATT_COLS), BF16),
        v_shape,
        jax.ShapeDtypeStruct((T, ATT_COLS), BF16),
        jax.ShapeDtypeStruct((T, SSM_INNER), F32),
        jax.ShapeDtypeStruct((T, CONV_DIM), F32),
        jax.ShapeDtypeStruct((T, SSM_HEADS), F32),
        jax.ShapeDtypeStruct((SSM_HEADS, T), F32),
    )
    out_specs = (row(ATT_COLS), k_spec, row(ATT_COLS), v_spec, row(ATT_COLS),
                 row(SSM_INNER), row(CONV_DIM), row(SSM_HEADS),
                 pl.BlockSpec((SSM_HEADS, tm), lambda i: (0, i)))
    return pl.pallas_call(
        functools.partial(_in_proj_kernel, cache_layout=cache is not None),
        grid=(T // tm,),
        in_specs=[row(D), pos, pos, _full(wq.shape), _full(wk.shape), _full(wv.shape), _full(wz.shape),
                  _full(wxbc.shape), _full(wdt.shape), _full(wdtT.shape)] + prev_specs,
        out_specs=out_specs,
        out_shape=out_shape,
        compiler_params=_cparams("parallel"),
        name="in_proj",
    )(x, cos_t, sin_t, wq, wk, wv, wz, wxbc, wdt, wdtT, *prev)


def _gated_group_norm(y, z, norm_w):
    y = y * _silu(z)
    half = SSM_INNER // SSM_GROUPS
    y2 = y * y
    ms0 = jnp.mean(y2[:, :half], axis=-1, keepdims=True)
    ms1 = jnp.mean(y2[:, half:], axis=-1, keepdims=True)
    lane = lax.broadcasted_iota(jnp.int32, y.shape, 1)
    scale = jnp.where(lane < half, lax.rsqrt(ms0 + NORM_EPS), lax.rsqrt(ms1 + NORM_EPS))
    return y * scale * norm_w


def _ssd_kernel(xbc_ref, z_ref, dt_ref, dtT_ref, convw_ref, convb_ref, dtb_ref, dtbT_ref, alog_ref, alogT_ref,
                dskip_ref, normw_ref, y_ref, conv_out_ref, state_out_ref, ext_ref, h_ref):
    Q = SSD_CHUNK
    c = pl.program_id(1)
    last = pl.num_programs(1) - 1

    @pl.when(c == 0)
    def _():
        ext_ref[0:8, :] = jnp.zeros((8, CONV_DIM), F32)
        h_ref[...] = jnp.zeros(h_ref.shape, F32)

    ext_ref[8:8 + Q, :] = xbc_ref[...]
    conv = convb_ref[...]
    for i in range(CONV_WIDTH):
        conv = conv + ext_ref[5 + i:5 + i + Q, :] * convw_ref[i:i + 1, :]
    ext_ref[0:8, :] = xbc_ref[Q - 8:Q, :]
    conv = _silu(conv)
    xs = conv[:, :SSM_INNER]
    bmat = conv[:, SSM_INNER:SSM_INNER + SSM_GROUPS * SSM_STATE]
    cmat = conv[:, SSM_INNER + SSM_GROUPS * SSM_STATE:]

    dt_col = _softplus(dt_ref[...] + dtb_ref[...])
    a_col = dt_col * (-jnp.exp(alog_ref[...]))
    a_row = _softplus(dtT_ref[...] + dtbT_ref[...]) * (-jnp.exp(alogT_ref[...]))
    ri = lax.broadcasted_iota(jnp.int32, (Q, Q), 0)
    ci = lax.broadcasted_iota(jnp.int32, (Q, Q), 1)
    causal = ci <= ri
    tril = causal.astype(F32)
    triu = (ri <= ci).astype(F32)
    acs_col = jnp.dot(tril, a_col, preferred_element_type=F32, precision=lax.Precision.HIGHEST)
    acs_row = jnp.dot(a_row, triu, preferred_element_type=F32, precision=lax.Precision.HIGHEST)

    lo = ci < SSM_HEAD_DIM
    top = ri < SSM_HEAD_DIM
    pairs = []
    for k in range(SSM_HEADS // 2):
        h0, h1 = 2 * k, 2 * k + 1
        g = k // (SSM_HEADS // 2 // SSM_GROUPS)
        b_g = bmat[:, g * SSM_STATE:(g + 1) * SSM_STATE].astype(BF16)
        c_g = cmat[:, g * SSM_STATE:(g + 1) * SSM_STATE].astype(BF16)
        cb = lax.dot_general(c_g, b_g, NT_DIMS, preferred_element_type=F32)
        xs_pair = xs[:, 2 * SSM_HEAD_DIM * k:2 * SSM_HEAD_DIM * (k + 1)]
        dtc = jnp.where(lo, dt_col[:, h0:h0 + 1], dt_col[:, h1:h1 + 1])
        xdt = xs_pair * dtc
        acol = jnp.where(lo, acs_col[:, h0:h0 + 1], acs_col[:, h1:h1 + 1])
        alast0 = acs_row[h0:h0 + 1, Q - 1:Q]
        alast1 = acs_row[h1:h1 + 1, Q - 1:Q]
        alast = jnp.where(lo[0:1, :], alast0, alast1)
        xw = xdt * jnp.exp(alast - acol)
        y_pair = jnp.where(lo, dskip_ref[0:1, h0:h0 + 1], dskip_ref[0:1, h1:h1 + 1]) * xs_pair
        for h, sel in ((h0, lo), (h1, jnp.logical_not(lo))):
            seg = acs_col[:, h:h + 1] - acs_row[h:h + 1, :]
            decay = jnp.exp(jnp.where(causal, seg, -jnp.inf))
            m = (cb * decay).astype(BF16)
            y_pair = y_pair + jnp.dot(m, jnp.where(sel, xdt, 0.0).astype(BF16), preferred_element_type=F32)
        states = jnp.dot(xw.T.astype(BF16), b_g, preferred_element_type=F32)
        h_prev = h_ref[k]
        y_off = lax.dot_general(c_g, h_prev.astype(BF16), NT_DIMS, preferred_element_type=F32)
        y_pair = y_pair + y_off * jnp.exp(acol)
        dec = jnp.where(top[:, 0:1], jnp.exp(alast0), jnp.exp(alast1))
        h_ref[k] = h_prev * dec + states
        pairs.append(y_pair)
    y = jnp.concatenate(pairs, axis=1)
    y_ref[...] = _gated_group_norm(y, z_ref[...], normw_ref[...]).astype(BF16)

    @pl.when(c == last)
    def _():
        conv_out_ref[0] = xbc_ref[Q - (CONV_WIDTH - 1):Q, :]
        state_out_ref[0] = h_ref[...]


def _ssd(xbc, z, dt, dtT, sw, B, S):
    convw, convb, dtb, dtbT, alog, alogT, dskip, normw = sw
    nc = S // SSD_CHUNK
    Q = SSD_CHUNK
    rowb = lambda n: pl.BlockSpec((Q, n), lambda b, c: (b * nc + c, 0))
    npair = SSM_HEADS // 2
    return pl.pallas_call(
        _ssd_kernel,
        grid=(B, nc),
        in_specs=[rowb(CONV_DIM), rowb(SSM_INNER), rowb(SSM_HEADS),
                  pl.BlockSpec((SSM_HEADS, Q), lambda b, c: (0, b * nc + c)),
                  _full(convw.shape), _full(convb.shape), _full(dtb.shape), _full(dtbT.shape),
                  _full(alog.shape), _full(alogT.shape), _full(dskip.shape), _full(normw.shape)],
        out_specs=(rowb(SSM_INNER),
                   pl.BlockSpec((1, CONV_WIDTH - 1, CONV_DIM), lambda b, c: (b, 0, 0)),
                   pl.BlockSpec((1, npair, 2 * SSM_HEAD_DIM, SSM_STATE), lambda b, c: (b, 0, 0, 0))),
        out_shape=(jax.ShapeDtypeStruct((B * S, SSM_INNER), BF16),
                   jax.ShapeDtypeStruct((B, CONV_WIDTH - 1, CONV_DIM), F32),
                   jax.ShapeDtypeStruct((B, npair, 2 * SSM_HEAD_DIM, SSM_STATE), F32)),
        scratch_shapes=[pltpu.VMEM((Q + 8, CONV_DIM), F32),
                        pltpu.VMEM((npair, 2 * SSM_HEAD_DIM, SSM_STATE), F32)],
        compiler_params=_cparams("parallel", "arbitrary"),
        name="ssd_scan",
    )(xbc, z, dt, dtT, convw, convb, dtb, dtbT, alog, alogT, dskip, normw)


def _ssd_step_kernel(xbc_ref, z_ref, dt_ref, cstate_ref, hstate_ref, convw_ref, convb_ref, dtb_ref, alog_ref,
                     dskip_ref, normw_ref, *rest):
    y_ref, cout_ref, hout_ref = rest[-3:]
    n_prev = hout_ref.shape[0] - 1
    if n_prev:
        hout_ref[0:n_prev] = rest[0][...]
    bt = xbc_ref.shape[0]
    P, N = SSM_HEAD_DIM, SSM_STATE
    new = xbc_ref[...]
    conv = convb_ref[...] + new * convw_ref[CONV_WIDTH - 1:CONV_WIDTH, :]
    for i in range(CONV_WIDTH - 1):
        conv = conv + cstate_ref[:, i * CONV_DIM:(i + 1) * CONV_DIM] * convw_ref[i:i + 1, :]
    cout_ref[:, 0:(CONV_WIDTH - 2) * CONV_DIM] = cstate_ref[:, CONV_DIM:(CONV_WIDTH - 1) * CONV_DIM]
    cout_ref[:, (CONV_WIDTH - 2) * CONV_DIM:] = new
    conv = _silu(conv)
    xs = conv[:, :SSM_INNER]
    bmat = conv[:, SSM_INNER:SSM_INNER + SSM_GROUPS * N]
    cmat = conv[:, SSM_INNER + SSM_GROUPS * N:]
    dt = _softplus(dt_ref[...] + dtb_ref[...])
    dec = jnp.exp(dt * (-jnp.exp(alog_ref[...])))
    H = SSM_HEADS
    HP = H * P
    exact = lax.Precision.HIGHEST
    eye_n = (lax.broadcasted_iota(jnp.int32, (N, N), 0) == lax.broadcasted_iota(jnp.int32, (N, N), 1)).astype(F32)
    x_t = jnp.concatenate(
        [lax.dot_general(eye_n, xs[:, c * N:(c + 1) * N], NT_DIMS, preferred_element_type=F32, precision=exact)
         for c in range(HP // N)], axis=0)
    rep = (lax.broadcasted_iota(jnp.int32, (HP, H), 0) // P == lax.broadcasted_iota(jnp.int32, (HP, H), 1)).astype(F32)
    per_head = jnp.concatenate([dt, dec, jnp.broadcast_to(dskip_ref[...], (8, H))], axis=0)
    cols = lax.dot_general(rep, per_head, NT_DIMS, preferred_element_type=F32, precision=exact)
    d_col = cols[:, 2 * bt:2 * bt + 1]
    first_group = lax.broadcasted_iota(jnp.int32, (HP, 1), 0) < HP // SSM_GROUPS
    lane_b = lax.broadcasted_iota(jnp.int32, (HP, bt), 1)
    y_t = jnp.zeros((HP, bt), F32)
    for b in range(bt):
        x_col = x_t[:, b:b + 1]
        b_sel = jnp.where(first_group, bmat[b:b + 1, 0:N], bmat[b:b + 1, N:2 * N])
        h_new = (hstate_ref[b].reshape(HP, N) * cols[:, bt + b:bt + b + 1]
                 + (x_col * cols[:, b:b + 1]) * b_sel)
        hout_ref[n_prev, b] = h_new.reshape(H, P, N)
        c2 = jnp.concatenate([cmat[b:b + 1, 0:N], cmat[b:b + 1, N:2 * N], jnp.zeros((6, N), F32)], axis=0)
        yb = lax.dot_general(h_new.astype(BF16), c2.astype(BF16), NT_DIMS, preferred_element_type=F32)
        y_col = jnp.where(first_group, yb[:, 0:1], yb[:, 1:2]) + d_col * x_col
        y_t = jnp.where(lane_b == b, y_col, y_t)
    eye_b = (lax.broadcasted_iota(jnp.int32, (bt, bt), 0) == lax.broadcasted_iota(jnp.int32, (bt, bt), 1)).astype(F32)
    y = lax.dot_general(eye_b, y_t, NT_DIMS, preferred_element_type=F32, precision=exact)
    y_ref[...] = _gated_group_norm(y, z_ref[...], normw_ref[...]).astype(BF16)


def _ssd_step(xbc, z, dt, cstate, hstate, layer, sw, h_prev=None, bt=8):
    convw, convb, dtb, _, alog, _, dskip, normw = sw
    Bs = xbc.shape[0]
    cw = (CONV_WIDTH - 1) * CONV_DIM
    rowb = lambda n: pl.BlockSpec((bt, n), lambda i: (i, 0))
    hshape = (bt, SSM_HEADS, SSM_HEAD_DIM, SSM_STATE)
    layers = lambda n: pl.BlockSpec((n,) + hshape, lambda i: (0, i, 0, 0, 0))
    prev = () if h_prev is None else (h_prev,)
    n_prev = 0 if h_prev is None else h_prev.shape[0]
    return pl.pallas_call(
        _ssd_step_kernel,
        grid=(Bs // bt,),
        in_specs=[rowb(CONV_DIM), rowb(SSM_INNER), rowb(SSM_HEADS),
                  pl.BlockSpec((None, bt, cw), lambda i: (layer, i, 0)),
                  pl.BlockSpec((None,) + hshape, lambda i: (layer, i, 0, 0, 0)),
                  _full(convw.shape), _full(convb.shape), _full(dtb.shape), _full(alog.shape),
                  _full(dskip.shape), _full(normw.shape)] + [layers(n_prev)] * len(prev),
        out_specs=(rowb(SSM_INNER), rowb(cw), layers(n_prev + 1)),
        out_shape=(jax.ShapeDtypeStruct((Bs, SSM_INNER), BF16),
                   jax.ShapeDtypeStruct((Bs, cw), F32),
                   jax.ShapeDtypeStruct((n_prev + 1,) + hstate.shape[1:], F32)),
        compiler_params=_cparams("parallel"),
        name="ssd_step",
    )(xbc, z, dt, cstate, hstate, convw, convb, dtb, alog, dskip, normw, *prev)


def _lambda_value(lp, lam_init):
    d1 = jnp.sum(lp[0:1, :] * lp[1:2, :], axis=1, keepdims=True)
    d2 = jnp.sum(lp[2:3, :] * lp[3:4, :], axis=1, keepdims=True)
    return jnp.exp(d1) - jnp.exp(d2) + lam_init


def _head_norm(o, nw, lam_init):
    return o * lax.rsqrt(jnp.mean(o * o, axis=-1, keepdims=True) + NORM_EPS) * nw * (1.0 - lam_init)


def _attn_kernel(lp_ref, nw_ref, q_ref, k_ref, v_ref, o_ref, m_ref, l_ref, acc_ref, *, lam_init):
    tq = q_ref.shape[0]
    qi = pl.program_id(2)
    q = q_ref[...]
    lane = lax.broadcasted_iota(jnp.int32, q.shape, 1)
    zero = jnp.zeros_like(q)
    q_maps = (jnp.where(lane < ATT_QK_DIM, q, zero), jnp.where(lane >= ATT_QK_DIM, q, zero))
    m_ref[...] = jnp.full(m_ref.shape, -jnp.inf, F32)
    l_ref[...] = jnp.zeros(l_ref.shape, F32)
    acc_ref[...] = jnp.zeros(acc_ref.shape, F32)
    below_diag = (lax.broadcasted_iota(jnp.int32, (tq, tq), 1) <= lax.broadcasted_iota(jnp.int32, (tq, tq), 0))

    def block(j, diagonal):
        start = pl.multiple_of(j * tq, tq)
        k = k_ref[pl.ds(start, tq), :]
        v = v_ref[pl.ds(start, tq), :]
        for i in range(2):
            s = lax.dot_general(q_maps[i], k, NT_DIMS, preferred_element_type=F32)
            if diagonal:
                s = jnp.where(below_diag, s, -jnp.inf)
            chunks = [s[:, c * LANES:(c + 1) * LANES] for c in range(tq // LANES)]
            top = functools.reduce(jnp.maximum, chunks)
            m_old = m_ref[i]
            m_new = jnp.maximum(m_old, jnp.max(top, axis=1, keepdims=True))
            alpha = jnp.exp(m_old - m_new)
            ps = [jnp.exp(c - m_new) for c in chunks]
            l_ref[i] = alpha * l_ref[i] + functools.reduce(jnp.add, ps)
            p = jnp.concatenate(ps, axis=1).astype(BF16)
            acc_ref[i] = alpha * acc_ref[i] + jnp.dot(p, v, preferred_element_type=F32)
            m_ref[i] = m_new

    def full_block(j, carry):
        block(j, False)
        return carry

    lax.fori_loop(0, qi, full_block, 0)
    block(qi, True)
    lam = _lambda_value(lp_ref[...], lam_init)
    l1 = jnp.sum(l_ref[0], axis=1, keepdims=True)
    l2 = jnp.sum(l_ref[1], axis=1, keepdims=True)
    o = acc_ref[0] / l1 - lam * (acc_ref[1] / l2)
    o_ref[...] = _head_norm(o, nw_ref[...], lam_init).astype(BF16)


def _attention(q, k, v, lp, nw, lam_init, B, S):
    tq = min(512, S)
    nq = S // tq
    qspec = pl.BlockSpec((tq, ATT_V_DIM), lambda b, h, i: (b * nq + i, h))
    kspec = pl.BlockSpec((S, ATT_V_DIM), lambda b, h, i: (b, h))
    return pl.pallas_call(
        functools.partial(_attn_kernel, lam_init=lam_init),
        grid=(B, ATT_HEADS, nq),
        in_specs=[_full(lp.shape), _full(nw.shape), qspec, kspec, kspec],
        out_specs=qspec,
        out_shape=jax.ShapeDtypeStruct((B * S, ATT_COLS), BF16),
        scratch_shapes=[pltpu.VMEM((2, tq, LANES), F32), pltpu.VMEM((2, tq, LANES), F32),
                        pltpu.VMEM((2, tq, ATT_V_DIM), F32)],
        compiler_params=_cparams("parallel", "parallel", "arbitrary"),
        name="diff_attention",
    )(lp, nw, q, k, v)


def _decode_attn_kernel(pt_ref, lp_ref, nw_ref, q_ref, kn_ref, vn_ref, *refs, n_pages, lam_init):
    k_refs = refs[:n_pages]
    v_refs = refs[n_pages:2 * n_pages]
    o_ref = refs[2 * n_pages]
    R = 2 * ATT_HEADS
    page = k_refs[0].shape[2]
    r = lax.broadcasted_iota(jnp.int32, (R, ATT_COLS), 0)
    grp = lax.broadcasted_iota(jnp.int32, (R, ATT_COLS), 1) // ATT_QK_DIM
    target = jnp.where(r < ATT_HEADS, 2 * r, 2 * (r - ATT_HEADS) + 1)
    qf = jnp.where(grp == target, q_ref[0].astype(BF16).astype(F32), 0.0)
    qt = qf.astype(BF16)
    s = [jnp.dot(qt, k_ref[...].reshape(ATT_COLS, page).astype(BF16), preferred_element_type=F32)
         for k_ref in k_refs]
    s_new = jnp.sum(qf * kn_ref[0].astype(BF16).astype(F32), axis=1, keepdims=True)
    m = s_new
    for sj in s:
        m = jnp.maximum(m, jnp.max(sj, axis=1, keepdims=True))
    p = [jnp.exp(sj - m) for sj in s]
    p_new = jnp.exp(s_new - m)
    l = p_new
    for pj in p:
        l = l + jnp.sum(pj, axis=1, keepdims=True)
    inv_l = 1.0 / l
    lam = _lambda_value(lp_ref[...], lam_init)
    diff = lambda t: t - lam * pltpu.roll(t, ATT_HEADS, 0)
    a = [diff(pj * inv_l).astype(BF16) for pj in p]
    a_new = diff(jnp.broadcast_to(p_new * inv_l, (R, 128)))[:, 0:1].astype(BF16).astype(F32)
    row = lax.broadcasted_iota(jnp.int32, (R, ATT_V_DIM), 0)
    out = a_new * vn_ref[0].astype(BF16).astype(F32)
    for h in range(ATT_HEADS):
        acc = jnp.zeros((R, ATT_V_DIM), F32)
        for aj, v_ref in zip(a, v_refs):
            v_h = v_ref[pl.ds(h, page, stride=ATT_HEADS), :].astype(BF16)
            acc = acc + jnp.dot(aj, v_h, preferred_element_type=F32)
        out = out + jnp.where(row == h, acc, 0.0)
    o_ref[0] = _head_norm(out, nw_ref[...], lam_init)


def _decode_attention(pt, layer, q, k_new, v_new, cache_k, cache_v, lp, nw, lam_init):
    Bs, n_pages = pt.shape
    R = 2 * ATT_HEADS
    qspec = pl.BlockSpec((1, 1, ATT_COLS), lambda b, pt: (b, 0, 0))
    vspec = pl.BlockSpec((1, R, ATT_V_DIM), lambda b, pt: (b, 0, 0))
    kpages = [pl.BlockSpec((None, None) + cache_k.shape[2:], lambda b, pt, j=j: (layer, pt[b, j], 0, 0, 0))
              for j in range(n_pages)]
    vpages = [pl.BlockSpec((None, None) + cache_v.shape[2:], lambda b, pt, j=j: (layer, pt[b, j], 0, 0))
              for j in range(n_pages)]
    grid_spec = pltpu.PrefetchScalarGridSpec(
        num_scalar_prefetch=1,
        grid=(Bs,),
        in_specs=[pl.BlockSpec(lp.shape, lambda b, pt: (0, 0)), pl.BlockSpec(nw.shape, lambda b, pt: (0, 0)),
                  qspec, qspec, vspec] + kpages + vpages,
        out_specs=vspec,
    )
    return pl.pallas_call(
        functools.partial(_decode_attn_kernel, n_pages=n_pages, lam_init=lam_init),
        grid_spec=grid_spec,
        out_shape=jax.ShapeDtypeStruct((Bs, R, ATT_V_DIM), F32),
        compiler_params=_cparams("parallel"),
        name="decode_attention",
    )(pt, lp, nw, q, k_new, v_new, *([cache_k] * n_pages), *([cache_v] * n_pages))


def _proj_ln_kernel(*refs, n_in, alpha):
    a_refs = refs[:n_in]
    w_refs = refs[n_in:2 * n_in]
    x_ref, g_ref, b_ref, o_ref = refs[2 * n_in:]
    acc = alpha * x_ref[...]
    for a_ref, w_ref in zip(a_refs, w_refs):
        acc = acc + jnp.dot(a_ref[...].astype(BF16), w_ref[...], preferred_element_type=F32)
    o_ref[...] = _layer_norm(acc, g_ref[...], b_ref[...])


def _proj_ln(acts, ws, x, g, b, alpha, tm):
    T, D = x.shape
    row = lambda n: pl.BlockSpec((tm, n), lambda i: (i, 0))
    return pl.pallas_call(
        functools.partial(_proj_ln_kernel, n_in=len(acts), alpha=alpha),
        grid=(T // tm,),
        in_specs=[row(a.shape[1]) for a in acts] + [_full(w.shape) for w in ws] + [row(D), _full(g.shape), _full(b.shape)],
        out_specs=row(D),
        out_shape=jax.ShapeDtypeStruct((T, D), F32),
        compiler_params=_cparams("parallel"),
        name="proj_ln",
    )(*acts, *ws, x, g, b)


def _matmul_kernel(x_ref, w_ref, *o_refs):
    y = jnp.dot(x_ref[...].astype(BF16), w_ref[...], preferred_element_type=F32)
    for o_ref in o_refs:
        o_ref[...] = y.astype(o_ref.dtype)


def _matmul(x, w, out_dtypes, tm, tn):
    M, K = x.shape
    N = w.shape[1]
    ospec = pl.BlockSpec((tm, tn), lambda i, j: (i, j))
    return pl.pallas_call(
        _matmul_kernel,
        grid=(M // tm, N // tn),
        in_specs=[pl.BlockSpec((tm, K), lambda i, j: (i, 0)), pl.BlockSpec((K, tn), lambda i, j: (0, j))],
        out_specs=tuple(ospec for _ in out_dtypes),
        out_shape=tuple(jax.ShapeDtypeStruct((M, N), d) for d in out_dtypes),
        compiler_params=_cparams("parallel", "parallel"),
        name="matmul",
    )(x, w)


def _mem_kv_kernel(x_ref, w_ref, *rest):
    mk_ref, mv_ref, mkb_ref, mvb_ref = rest[-4:]
    tm = x_ref.shape[0]
    D = w_ref.shape[1] // 2
    dh = D // MEM_HEADS
    halves = dh // LANES
    y = jnp.dot(x_ref[...].astype(BF16), w_ref[...], preferred_element_type=F32)
    mkb_ref[...] = y[:, :D].astype(BF16)
    mvb_ref[...] = y[:, D:].astype(BF16)
    n_prev = mk_ref.shape[0] - 1
    if n_prev:
        mk_ref[0:n_prev] = rest[0][...]
        mv_ref[0:n_prev] = rest[1][...]
    for out_ref, base in ((mk_ref, 0), (mv_ref, D)):
        for h in range(MEM_HEADS):
            for c in range(halves):
                col = base + h * dh + c * LANES
                out_ref[n_prev, pl.ds(c * MEM_HEADS + h, tm, stride=halves * MEM_HEADS), :] = y[:, col:col + LANES]


def _mem_kv(mem, w, prev, tm):
    M, D = mem.shape
    rows_per_token = D // LANES
    rows = lambda n: pl.BlockSpec((n, tm * rows_per_token, LANES), lambda i: (0, i, 0))
    flat = pl.BlockSpec((tm, D), lambda i: (i, 0))
    prev = () if prev is None else tuple(prev)
    n_prev = prev[0].shape[0] if prev else 0
    rows_shape = jax.ShapeDtypeStruct((n_prev + 1, M * rows_per_token, LANES), F32)
    return pl.pallas_call(
        _mem_kv_kernel,
        grid=(M // tm,),
        in_specs=[flat, _full(w.shape)] + [rows(n_prev)] * len(prev),
        out_specs=(rows(n_prev + 1), rows(n_prev + 1), flat, flat),
        out_shape=(rows_shape, rows_shape, jax.ShapeDtypeStruct((M, D), BF16), jax.ShapeDtypeStruct((M, D), BF16)),
        compiler_params=_cparams("parallel"),
        name="mem_kv",
    )(mem, w, *prev)


def _softmax_rows(s):
    m = jnp.max(s, axis=1, keepdims=True)
    p = jnp.exp(s - m)
    return p / jnp.sum(p, axis=1, keepdims=True)


def _cross_kernel(x_ref, wq_ref, mk_ref, mv_ref, wo_ref, g_ref, b_ref, o_ref, *, alpha):
    x = x_ref[...]
    D = x.shape[1]
    dh = D // MEM_HEADS
    q = jnp.dot(x.astype(BF16), wq_ref[...], preferred_element_type=F32).astype(BF16)
    outs = []
    for h in range(MEM_HEADS):
        sl = slice(h * dh, (h + 1) * dh)
        s = lax.dot_general(q[:, sl], mk_ref[:, sl], NT_DIMS, preferred_element_type=F32) * (dh ** -0.5)
        outs.append(jnp.dot(_softmax_rows(s).astype(BF16), mv_ref[:, sl], preferred_element_type=F32).astype(BF16))
    o = jnp.concatenate(outs, axis=1)
    y = alpha * x + jnp.dot(o, wo_ref[...], preferred_element_type=F32)
    o_ref[...] = _layer_norm(y, g_ref[...], b_ref[...])


def _cross_attention(x, wq, mk, mv, wo, g, b, alpha, B, S, n_mem, tm):
    T, D = x.shape
    nt = S // tm
    row = pl.BlockSpec((tm, D), lambda bi, i: (bi * nt + i, 0))
    mem = pl.BlockSpec((n_mem, D), lambda bi, i: (bi, 0))
    return pl.pallas_call(
        functools.partial(_cross_kernel, alpha=alpha),
        grid=(B, nt),
        in_specs=[row, _full(wq.shape), mem, mem, _full(wo.shape), _full(g.shape), _full(b.shape)],
        out_specs=row,
        out_shape=jax.ShapeDtypeStruct((T, D), F32),
        compiler_params=_cparams("parallel", "parallel"),
        name="cross_attention",
    )(x, wq, mk, mv, wo, g, b)


def _cross_decode_kernel(q_ref, mk_ref, mv_ref, o_ref):
    R = q_ref.shape[1] // 2
    rows_kv = mk_ref.shape[0]
    dh = 2 * LANES
    part = lax.dot_general(q_ref[0].astype(BF16), mk_ref[...].astype(BF16), NT_DIMS,
                           preferred_element_type=F32)
    s = (part[:R] + pltpu.roll(part[R:], rows_kv - MEM_HEADS, 1)) * (dh ** -0.5)
    own = (lax.broadcasted_iota(jnp.int32, (R, rows_kv), 1) & (R - 1)) == lax.broadcasted_iota(jnp.int32, (R, rows_kv), 0)
    p = _softmax_rows(jnp.where(own, s, -jnp.inf))
    a = jnp.concatenate([p, pltpu.roll(p, MEM_HEADS, 1)], axis=0).astype(BF16)
    o_ref[0] = jnp.dot(a, mv_ref[...].astype(BF16), preferred_element_type=F32)


def _cross_decode(q, mem_k, mem_v, layer):
    Bs, R2, _ = q.shape
    tok = pl.BlockSpec((1, R2, LANES), lambda b: (b, 0, 0))
    mem = pl.BlockSpec((None, None) + mem_k.shape[2:], lambda b: (layer, b, 0, 0))
    return pl.pallas_call(
        _cross_decode_kernel,
        grid=(Bs,),
        in_specs=[tok, mem, mem],
        out_specs=tok,
        out_shape=jax.ShapeDtypeStruct((Bs, R2, LANES), F32),
        compiler_params=_cparams("parallel"),
        name="cross_decode",
    )(q, mem_k, mem_v)


def _ffn_kernel(x_ref, wg_ref, wu_ref, wd_ref, g_ref, b_ref, o_ref, acc_ref, *, alpha):
    f = pl.program_id(1)
    x = x_ref[...]
    xb = x.astype(BF16)

    @pl.when(f == 0)
    def _():
        acc_ref[...] = alpha * x

    gate = jnp.dot(xb, wg_ref[...], preferred_element_type=F32)
    up = jnp.dot(xb, wu_ref[...], preferred_element_type=F32)
    acc_ref[...] += jnp.dot((_silu(gate) * up).astype(BF16), wd_ref[...], preferred_element_type=F32)

    @pl.when(f == pl.num_programs(1) - 1)
    def _():
        o_ref[...] = _layer_norm(acc_ref[...], g_ref[...], b_ref[...])


def _ffn(x, w_gu, w_down, g, b, alpha, tm, tf):
    T, D = x.shape
    F = w_down.shape[0]
    nf = F // tf
    row = pl.BlockSpec((tm, D), lambda i, f: (i, 0))
    return pl.pallas_call(
        functools.partial(_ffn_kernel, alpha=alpha),
        grid=(T // tm, nf),
        in_specs=[row, pl.BlockSpec((D, tf), lambda i, f: (0, f)), pl.BlockSpec((D, tf), lambda i, f: (0, nf + f)),
                  pl.BlockSpec((tf, D), lambda i, f: (f, 0)), _full(g.shape), _full(b.shape)],
        out_specs=row,
        out_shape=jax.ShapeDtypeStruct((T, D), F32),
        scratch_shapes=[pltpu.VMEM((tm, D), F32)],
        compiler_params=_cparams("parallel", "arbitrary"),
        name="ffn",
    )(x, w_gu, w_gu, w_down, g, b)


def _router_kernel(x_ref, wr_ref, gate_ref, top_i_ref, top_g_ref):
    logits = jnp.dot(x_ref[...].astype(BF16), wr_ref[...], preferred_element_type=F32)
    lane = lax.broadcasted_iota(jnp.int32, logits.shape, 1).astype(F32)
    none = float(N_EXPERTS)
    m1 = jnp.max(logits, axis=1, keepdims=True)
    i1 = jnp.min(jnp.where(logits == m1, lane, none), axis=1, keepdims=True)
    rest = jnp.where(lane == i1, -jnp.inf, logits)
    m2 = jnp.max(rest, axis=1, keepdims=True)
    i2 = jnp.min(jnp.where(rest == m2, lane, none), axis=1, keepdims=True)
    e = jnp.exp(m2 - m1)
    den = 1.0 + e
    gate_ref[...] = jnp.where(lane == i1, 1.0 / den, 0.0) + jnp.where(lane == i2, e / den, 0.0)
    first = lax.broadcasted_iota(jnp.int32, top_i_ref.shape, 1) == 0
    top_i_ref[...] = jnp.where(first, i1, i2).astype(jnp.int32)
    top_g_ref[...] = jnp.where(first, 1.0 / den, e / den)


def _router(x, wr, tm):
    T, D = x.shape
    top = pl.BlockSpec((tm, 2), lambda i: (i, 0))
    return pl.pallas_call(
        _router_kernel,
        grid=(T // tm,),
        in_specs=[pl.BlockSpec((tm, D), lambda i: (i, 0)), _full(wr.shape)],
        out_specs=(pl.BlockSpec((tm, N_EXPERTS), lambda i: (i, 0)), top, top),
        out_shape=(jax.ShapeDtypeStruct((T, N_EXPERTS), F32), jax.ShapeDtypeStruct((T, 2), jnp.int32),
                   jax.ShapeDtypeStruct((T, 2), F32)),
        compiler_params=_cparams("parallel"),
        name="router",
    )(x, wr)


def _moe_plan(top_i, tm_e, n_tiles):
    T = top_i.shape[0]
    chosen = (top_i[:, :, None] == jnp.arange(N_EXPERTS, dtype=jnp.int32)[None, None, :]).any(axis=1)
    chosen = chosen.astype(jnp.int32)
    count = chosen.sum(axis=0)
    rank = jnp.cumsum(chosen, axis=0) - chosen
    tiles = (count + tm_e - 1) // tm_e
    tile_end = jnp.cumsum(tiles)
    row_start = (tile_end - tiles) * tm_e
    dest = jnp.take_along_axis(row_start[None, :] + rank, top_i, axis=1).astype(jnp.int32)
    token = jnp.broadcast_to(jnp.arange(T, dtype=jnp.int32)[:, None], (T, 2))
    src = jnp.zeros((n_tiles * tm_e,), jnp.int32).at[dest.reshape(-1)].set(token.reshape(-1))
    n_used = tile_end[-1]
    tile = jnp.arange(n_tiles, dtype=jnp.int32)
    tile_expert = (tile_end[None, :] <= jnp.minimum(tile, n_used - 1)[:, None]).sum(axis=1).astype(jnp.int32)
    return src, dest, tile_expert, n_used.reshape(1).astype(jnp.int32)


def _row_copy(src_hbm, src_row, dst_ref, dst_row, sem):
    return pltpu.make_async_copy(src_hbm.at[pl.ds(src_row, 1), :], dst_ref.at[pl.ds(dst_row, 1), :], sem)


def _moe_ffn_kernel(te_ref, nv_ref, src_ref, x_hbm, wg_ref, wu_ref, wd_ref, y_ref, xrows_ref, xb_ref, sem):
    tm_e = xrows_ref.shape[1]
    i = pl.program_id(0)
    f = pl.program_id(1)
    n_used = nv_ref[0]
    slot = lax.rem(i, 2)

    def start_gather(tile, slot):
        def body(r8, carry):
            for u in range(ROW_COPY_UNROLL):
                r = r8 * ROW_COPY_UNROLL + u
                _row_copy(x_hbm, src_ref[tile * tm_e + r], xrows_ref.at[slot], r, sem.at[slot]).start()
            return carry
        lax.fori_loop(0, tm_e // ROW_COPY_UNROLL, body, 0)

    @pl.when(f == 0)
    def _():
        @pl.when(i == 0)
        def _():
            start_gather(0, 0)

        @pl.when(i < n_used)
        def _():
            pltpu.make_async_copy(x_hbm.at[pl.ds(0, tm_e), :], xrows_ref.at[slot], sem.at[slot]).wait()
            xb_ref[...] = xrows_ref[slot].astype(BF16)

        @pl.when(i + 1 < n_used)
        def _():
            start_gather(i + 1, 1 - slot)

    @pl.when(i < n_used)
    def _():
        xb = xb_ref[...]
        gate = jnp.dot(xb, wg_ref[...], preferred_element_type=F32)
        up = jnp.dot(xb, wu_ref[...], preferred_element_type=F32)
        part = jnp.dot((_silu(gate) * up).astype(BF16), wd_ref[...], preferred_element_type=F32)

        @pl.when(f == 0)
        def _():
            y_ref[...] = part

        @pl.when(f > 0)
        def _():
            y_ref[...] += part

    @pl.when(jnp.logical_and(i >= n_used, f == 0))
    def _():
        y_ref[...] = jnp.zeros(y_ref.shape, F32)


def _moe_ffn(x, src, tile_expert, n_used, w_gu, w_down, tm_e, tf):
    T, D = x.shape
    E, F, _ = w_down.shape
    nf = F // tf
    n_tiles = tile_expert.shape[0]
    grid_spec = pltpu.PrefetchScalarGridSpec(
        num_scalar_prefetch=3,
        grid=(n_tiles, nf),
        in_specs=[pl.BlockSpec(memory_space=pl.ANY),
                  pl.BlockSpec((None, D, tf), lambda i, f, te, nv, src: (te[i], 0, f)),
                  pl.BlockSpec((None, D, tf), lambda i, f, te, nv, src: (te[i], 0, nf + f)),
                  pl.BlockSpec((None, tf, D), lambda i, f, te, nv, src: (te[i], f, 0))],
        out_specs=pl.BlockSpec((tm_e, D), lambda i, f, te, nv, src: (i, 0)),
        scratch_shapes=[pltpu.VMEM((2, tm_e, D), F32), pltpu.VMEM((tm_e, D), BF16), pltpu.SemaphoreType.DMA((2,))],
    )
    return pl.pallas_call(
        _moe_ffn_kernel,
        grid_spec=grid_spec,
        out_shape=jax.ShapeDtypeStruct((n_tiles * tm_e, D), F32),
        compiler_params=_cparams("arbitrary", "arbitrary"),
        name="moe_ffn",
    )(tile_expert, n_used, src, x, w_gu, w_gu, w_down)


def _moe_combine_kernel(d1_ref, d2_ref, x_ref, gate_ref, g_ref, b_ref, y_hbm, o_ref, yrows_ref, sem, *, alpha):
    tm = yrows_ref.shape[2]
    i = pl.program_id(0)
    slot = lax.rem(i, 2)

    def start_gather(tile, slot):
        def body(r8, carry):
            for u in range(ROW_COPY_UNROLL):
                r = r8 * ROW_COPY_UNROLL + u
                _row_copy(y_hbm, d1_ref[tile * tm + r], yrows_ref.at[slot, 0], r, sem.at[slot]).start()
                _row_copy(y_hbm, d2_ref[tile * tm + r], yrows_ref.at[slot, 1], r, sem.at[slot]).start()
            return carry
        lax.fori_loop(0, tm // ROW_COPY_UNROLL, body, 0)

    @pl.when(i == 0)
    def _():
        start_gather(0, 0)

    for k in range(2):
        pltpu.make_async_copy(y_hbm.at[pl.ds(0, tm), :], yrows_ref.at[slot, k], sem.at[slot]).wait()

    @pl.when(i + 1 < pl.num_programs(0))
    def _():
        start_gather(i + 1, 1 - slot)

    y = yrows_ref[slot, 0] * gate_ref[:, 0:1] + yrows_ref[slot, 1] * gate_ref[:, 1:2]
    o_ref[...] = _layer_norm(alpha * x_ref[...] + y, g_ref[...], b_ref[...])


def _moe_combine(x, y_rows, dest, top_g, g, b, alpha, tm):
    T, D = x.shape
    row = pl.BlockSpec((tm, D), lambda i, d1, d2: (i, 0))
    grid_spec = pltpu.PrefetchScalarGridSpec(
        num_scalar_prefetch=2,
        grid=(T // tm,),
        in_specs=[row, pl.BlockSpec((tm, 2), lambda i, d1, d2: (i, 0)),
                  pl.BlockSpec(g.shape, lambda i, d1, d2: (0, 0)), pl.BlockSpec(b.shape, lambda i, d1, d2: (0, 0)),
                  pl.BlockSpec(memory_space=pl.ANY)],
        out_specs=row,
        scratch_shapes=[pltpu.VMEM((2, 2, tm, D), F32), pltpu.SemaphoreType.DMA((2,))],
    )
    return pl.pallas_call(
        functools.partial(_moe_combine_kernel, alpha=alpha),
        grid_spec=grid_spec,
        out_shape=jax.ShapeDtypeStruct((T, D), F32),
        compiler_params=_cparams("arbitrary"),
        name="moe_combine",
    )(dest[:, 0], dest[:, 1], x, top_g, g, b, y_rows)


def _moe_kernel(x_ref, gate_ref, wg_ref, wu_ref, wd_ref, g_ref, b_ref, o_ref, acc_ref, *, alpha):
    e = pl.program_id(1)
    f = pl.program_id(2)
    x = x_ref[...]
    xb = x.astype(BF16)

    @pl.when(jnp.logical_and(e == 0, f == 0))
    def _():
        acc_ref[...] = alpha * x

    gates = gate_ref[...]
    lane = lax.broadcasted_iota(jnp.int32, gates.shape, 1)
    ge = jnp.sum(jnp.where(lane == e, gates, 0.0), axis=1, keepdims=True)
    gate = jnp.dot(xb, wg_ref[0], preferred_element_type=F32)
    up = jnp.dot(xb, wu_ref[0], preferred_element_type=F32)
    acc_ref[...] += jnp.dot((_silu(gate) * up).astype(BF16), wd_ref[0], preferred_element_type=F32) * ge

    @pl.when(jnp.logical_and(e == pl.num_programs(1) - 1, f == pl.num_programs(2) - 1))
    def _():
        o_ref[...] = _layer_norm(acc_ref[...], g_ref[...], b_ref[...])


def _moe(x, gates, w_gu, w_down, g, b, alpha, tm, tf):
    T, D = x.shape
    E, F, _ = w_down.shape
    nf = F // tf
    row = pl.BlockSpec((tm, D), lambda i, e, f: (i, 0))
    return pl.pallas_call(
        functools.partial(_moe_kernel, alpha=alpha),
        grid=(T // tm, E, nf),
        in_specs=[row, pl.BlockSpec((tm, E), lambda i, e, f: (i, 0)),
                  pl.BlockSpec((1, D, tf), lambda i, e, f: (e, 0, f)),
                  pl.BlockSpec((1, D, tf), lambda i, e, f: (e, 0, nf + f)),
                  pl.BlockSpec((1, tf, D), lambda i, e, f: (e, f, 0)), _full(g.shape), _full(b.shape)],
        out_specs=row,
        out_shape=jax.ShapeDtypeStruct((T, D), F32),
        scratch_shapes=[pltpu.VMEM((tm, D), F32)],
        compiler_params=_cparams("parallel", "arbitrary", "arbitrary"),
        name="moe",
    )(x, gates, w_gu, w_gu, w_down, g, b)


def _rope_tables(pos):
    half = ATT_QK_DIM // 2
    inv = ROPE_THETA ** (-jnp.arange(0, ATT_QK_DIM, 2, dtype=F32) / ATT_QK_DIM)
    ang = pos.astype(F32)[:, None] * inv[None, :]
    cos, sin = jnp.cos(ang), jnp.sin(ang)
    return jnp.concatenate([cos] * 4, axis=1), jnp.concatenate([-sin, sin, -sin, sin], axis=1)


def _row_tile(n, target):
    t = min(n, target)
    while n % t:
        t //= 2
    return t


def kernel(x_prompt, x_sample, cache_attn_k, cache_attn_v, cache_mem_k, cache_mem_v, state_conv, state_ssm,
           page_table, mem_prompt, ln_g, ln_b, w_in, conv_w, conv_b, dt_bias, a_log, d_skip, ssm_norm_w,
           lambda_params, attn_norm_w, w_out, w_cq, w_ckv, w_co, w_ffn_gu, w_ffn_down, w_router, w_exp_gu,
           w_exp_down):
    B, S, D = x_prompt.shape
    Bs = x_sample.shape[0]
    depth = w_in.shape[0]
    n_phys, page = cache_attn_k.shape[1], cache_attn_k.shape[2]
    n_pages = page_table.shape[1]
    past_len = n_pages * page
    n_mem = mem_prompt.shape[1]
    alpha = (2 * depth) ** 0.25
    assert S % SSD_CHUNK == 0 and x_sample.shape[1] == 1

    tm_p = _row_tile(S, 512)
    tm_s = _row_tile(Bs, 128)
    cos_p, sin_p = _rope_tables(jnp.arange(S, dtype=jnp.int32))
    cos_s, sin_s = _rope_tables(jnp.full((tm_s,), past_len, jnp.int32))

    hp = x_prompt.reshape(B * S, D)
    hs = x_sample.reshape(Bs, D)
    mem = mem_prompt.reshape(B * n_mem, D)
    cache_k = cache_attn_k.transpose(0, 1, 3, 4, 2)
    cache_v = cache_attn_v.reshape(depth, n_phys, page * ATT_HEADS, ATT_V_DIM)
    dh = D // MEM_HEADS
    mem_rows = lambda t: (t.reshape(depth, Bs, n_mem, MEM_HEADS, dh // LANES, LANES).swapaxes(3, 4)
                          .reshape(depth, Bs, n_mem * dh // LANES * MEM_HEADS, LANES))
    cmem_k, cmem_v = mem_rows(cache_mem_k), mem_rows(cache_mem_v)
    cstate = state_conv.reshape(depth, Bs, (CONV_WIDTH - 1) * CONV_DIM)
    pad_heads = lambda t: jnp.pad(t, [(0, 0)] * (t.ndim - 2) + [(0, 2 * ATT_HEADS - t.shape[-2]), (0, 0)])

    splits = (ATT_COLS, 2 * ATT_COLS, 3 * ATT_COLS, 3 * ATT_COLS + SSM_INNER, 3 * ATT_COLS + SSM_INNER + CONV_DIM)
    outs = {n: [] for n in ("cp", "sp", "ks", "vs", "cs")}
    kv_prompt = mem_kv_prompt = ssm_sample = None
    for l in range(depth):
        lam_init = 0.8 - 0.6 * math.exp(-0.3 * l)
        wl = w_in[l].astype(BF16)
        wq, wk, wv, wz, wxbc, wdt = (wl[:, a:b] for a, b in zip((0,) + splits, splits + (wl.shape[1],)))
        w_proj = (wq, wk, wv, wz, wxbc, wdt, wdt.T)
        sw = (conv_w[l], conv_b[l][None, :], dt_bias[l][None, :], dt_bias[l][:, None], a_log[l][None, :],
              a_log[l][:, None], d_skip[l][None, :], ssm_norm_w[l][None, :])
        lp = lambda_params[l]
        nw = attn_norm_w[l][None, :]
        wo = w_out[l].astype(BF16)
        wo_att, wo_ssm = wo[:ATT_COLS], wo[ATT_COLS:]
        wcq = w_cq[l].astype(BF16)
        wckv = w_ckv[l].astype(BF16)
        wco = w_co[l].astype(BF16)
        g0, g1, g2 = (ln_g[l, i][None, :] for i in range(3))
        b0, b1, b2 = (ln_b[l, i][None, :] for i in range(3))

        q, k_all, kb, v_all, vb, z, xbc, dt, dtT = _in_proj(hp, cos_p, sin_p, w_proj, tm_p, S // tm_p,
                                                            cache=(kv_prompt,))
        kv_prompt = (k_all, v_all)
        att = _attention(q, kb, vb, lp, nw, lam_init, B, S)
        y, conv_new, h_last = _ssd(xbc, z, dt, dtT, sw, B, S)
        hp = _proj_ln((att, y), (wo_att, wo_ssm), hp, g0, b0, alpha, tm_p)
        outs["cp"].append(conv_new)
        outs["sp"].append(h_last.reshape(B, SSM_HEADS, SSM_HEAD_DIM, SSM_STATE))
        mk_all, mv_all, mk_b, mv_b = _mem_kv(mem, wckv, mem_kv_prompt, _row_tile(B * n_mem, 256))
        mem_kv_prompt = (mk_all, mv_all)
        hp = _cross_attention(hp, wcq, mk_b, mv_b, wco, g1, b1, alpha, B, S, n_mem, tm_p)

        q, kf, _, vf, _, z, xbc, dt, _ = _in_proj(hs, cos_s, sin_s, w_proj, tm_s, 1)
        att = _decode_attention(page_table, l, q.astype(F32).reshape(Bs, 1, ATT_COLS), kf.reshape(Bs, 1, ATT_COLS),
                                pad_heads(vf.reshape(Bs, ATT_HEADS, ATT_V_DIM)), cache_k, cache_v, lp, nw, lam_init)
        att = att[:, :ATT_HEADS].reshape(Bs, ATT_COLS)
        y, conv_new, ssm_sample = _ssd_step(xbc, z, dt, cstate, state_ssm, l, sw, ssm_sample)
        hs = _proj_ln((att, y), (wo_att, wo_ssm), hs, g0, b0, alpha, tm_s)
        outs["ks"].append(kf.reshape(Bs, 1, 2 * ATT_HEADS, ATT_QK_DIM))
        outs["vs"].append(vf.reshape(Bs, 1, ATT_HEADS, ATT_V_DIM))
        outs["cs"].append(conv_new.reshape(Bs, CONV_WIDTH - 1, CONV_DIM))
        (qc,) = _matmul(hs, wcq, (F32,), tm_s, D)
        q_rows = pad_heads(qc.reshape(Bs, MEM_HEADS, dh // LANES, LANES).swapaxes(1, 2))
        oc = _cross_decode(q_rows.reshape(Bs, -1, LANES), cmem_k, cmem_v, l)
        oc = oc.reshape(Bs, dh // LANES, 2 * ATT_HEADS, LANES)[:, :, :MEM_HEADS].swapaxes(1, 2).reshape(Bs, D)
        hs = _proj_ln((oc,), (wco,), hs, g1, b1, alpha, tm_s)

        if l % 2 == 0:
            wgu = w_ffn_gu[l // 2].astype(BF16)
            wd = w_ffn_down[l // 2].astype(BF16)
            tf = wd.shape[0] // 2
            hp = _ffn(hp, wgu, wd, g2, b2, alpha, tm_p, tf)
            hs = _ffn(hs, wgu, wd, g2, b2, alpha, tm_s, tf)
        else:
            wr = w_router[l // 2].astype(BF16)
            wgu = w_exp_gu[l // 2].astype(BF16)
            wd = w_exp_down[l // 2].astype(BF16)
            tf = wd.shape[1] // 2
            _, top_i, top_g = _router(hp, wr, tm_p)
            tm_e = _row_tile(2 * B * S, MOE_ROW_TILE)
            src, dest, tile_expert, n_used = _moe_plan(top_i, tm_e, 2 * B * S // tm_e + N_EXPERTS)
            y_rows = _moe_ffn(hp, src, tile_expert, n_used, wgu, wd, tm_e, tf)
            hp = _moe_combine(hp, y_rows, dest, top_g, g2, b2, alpha, _row_tile(B * S, 256))
            hs = _moe(hs, _router(hs, wr, tm_s)[0], wgu, wd, g2, b2, alpha, tm_s, tf)

    st = {n: jnp.stack(v) for n, v in outs.items()}
    k_prompt = kv_prompt[0].reshape(depth, B, 2 * ATT_HEADS, ATT_QK_DIM, S).transpose(0, 1, 4, 2, 3)
    v_prompt = kv_prompt[1].reshape(depth, B, S, ATT_HEADS, ATT_V_DIM)
    mem_shape = (depth, B, n_mem, dh // LANES, MEM_HEADS, LANES)
    mem_k_prompt, mem_v_prompt = (t.reshape(mem_shape).swapaxes(3, 4).reshape(depth, B, n_mem, MEM_HEADS, dh)
                                  for t in mem_kv_prompt)
    return (hp.reshape(B, S, D), hs.reshape(Bs, 1, D), k_prompt, v_prompt, mem_k_prompt, mem_v_prompt, st["cp"],
            st["sp"], st["ks"], st["vs"], st["cs"], ssm_sample)
```

```python
import functools
import math

import jax
import jax.numpy as jnp
from jax import lax
from jax.experimental import pallas as pl
from jax.experimental.pallas import tpu as pltpu

F32 = jnp.float32
BF16 = jnp.bfloat16

ATT_HEADS = 4
ATT_V_DIM = 128
ATT_QK_DIM = 64
ATT_COLS = 512
SSM_INNER = 512
SSM_HEAD_DIM = 64
SSM_HEADS = 8
SSM_GROUPS = 2
SSM_STATE = 128
CONV_WIDTH = 4
CONV_DIM = 1024
SSD_CHUNK = 128
MEM_HEADS = 4
N_EXPERTS = 8
ROPE_THETA = 10000.0
LN_EPS = 1e-5
NORM_EPS = 1e-5
QK_SCALE = ATT_QK_DIM ** -0.5

LANES = 128
MOE_ROW_TILE = 512
ROW_COPY_UNROLL = 8
MOE_F_STEPS = 2
ATT_HEADS_PER_STEP = 2
VMEM_LIMIT_BYTES = 56 * 1024 * 1024
NT_DIMS = (((1,), (1,)), ((), ()))


def _cparams(*sem):
    return pltpu.CompilerParams(dimension_semantics=sem, vmem_limit_bytes=VMEM_LIMIT_BYTES)


def _silu(x):
    return x / (1.0 + jnp.exp(-x))


def _softplus(x):
    return jnp.maximum(x, 0.0) + jnp.log1p(jnp.exp(-jnp.abs(x)))


def _layer_norm(y, g, b):
    mu = jnp.mean(y, axis=-1, keepdims=True)
    yc = y - mu
    var = jnp.mean(yc * yc, axis=-1, keepdims=True)
    return yc * lax.rsqrt(var + LN_EPS) * g + b


def _full(shape):
    return pl.BlockSpec(shape, lambda *_: (0,) * len(shape))


def _in_proj_kernel(x_ref, cos_ref, sin_ref, wq_ref, wk_ref, wv_ref, wz_ref, wxbc_ref, wdt_ref, wdtT_ref, *rest,
                    cache_layout):
    q_ref, kf_ref, kb_ref, vf_ref, vb_ref, z_ref, xbc_ref, dt_ref, dtT_ref = rest[-9:]
    tm = x_ref.shape[0]
    xb = x_ref[...].astype(BF16)
    cos = jnp.concatenate([cos_ref[...]] * 4, axis=1)
    sin = jnp.concatenate([sin_ref[...]] * 4, axis=1)
    lane = lax.broadcasted_iota(jnp.int32, (tm, ATT_COLS), 1)
    first_half = (lane & (ATT_QK_DIM - 1)) < (ATT_QK_DIM // 2)

    def rope(t):
        partner = jnp.where(first_half, pltpu.roll(t, ATT_COLS - ATT_QK_DIM // 2, 1),
                            pltpu.roll(t, ATT_QK_DIM // 2, 1))
        return t * cos + partner * sin

    q = rope(jnp.dot(xb, wq_ref[...], preferred_element_type=F32))
    q_ref[...] = (q * QK_SCALE).astype(BF16)
    k = rope(jnp.dot(xb, wk_ref[...], preferred_element_type=F32))
    kb_ref[...] = k.astype(BF16)
    v = jnp.dot(xb, wv_ref[...], preferred_element_type=F32)
    vb_ref[...] = v.astype(BF16)
    if cache_layout:
        n_prev = kf_ref.shape[0] - 1
        if n_prev:
            kf_ref[0:n_prev] = rest[0][...]
            vf_ref[0:n_prev] = rest[1][...]
        kf_ref[n_prev] = k.T
        for h in range(ATT_HEADS):
            vf_ref[n_prev, pl.ds(h, tm, stride=ATT_HEADS), :] = v[:, h * ATT_V_DIM:(h + 1) * ATT_V_DIM]
    else:
        kf_ref[...] = k
        vf_ref[...] = v
    z_ref[...] = jnp.dot(xb, wz_ref[...], preferred_element_type=F32)
    xbc_ref[...] = jnp.dot(xb, wxbc_ref[...], preferred_element_type=F32)
    dt_ref[...] = jnp.dot(xb, wdt_ref[...], preferred_element_type=F32)
    dtT_ref[...] = lax.dot_general(wdtT_ref[...], xb, NT_DIMS, preferred_element_type=F32)


def _in_proj(x, cos_t, sin_t, w, tm, n_pos_blocks, cache=None):
    T, D = x.shape
    wq, wk, wv, wz, wxbc, wdt, wdtT = w
    row = lambda n: pl.BlockSpec((tm, n), lambda i: (i, 0))
    pos = pl.BlockSpec((tm, 128), lambda i: (i % n_pos_blocks, 0))
    kv_shape = jax.ShapeDtypeStruct((T, ATT_COLS), F32)
    k_shape, v_shape, k_spec, v_spec, prev, prev_specs = kv_shape, kv_shape, row(ATT_COLS), row(ATT_COLS), (), []
    if cache is not None:
        (prev_kv,) = cache
        S = n_pos_blocks * tm
        k_block = lambda n: pl.BlockSpec((n, None, ATT_COLS, tm), lambda i: (0, i // n_pos_blocks, 0, i % n_pos_blocks))
        v_block = lambda n: pl.BlockSpec((n, ATT_HEADS * tm, ATT_V_DIM), lambda i: (0, i, 0))
        n_prev = 0
        if prev_kv is not None:
            prev = tuple(prev_kv)
            n_prev = prev[0].shape[0]
            prev_specs = [k_block(n_prev), v_block(n_prev)]
        k_shape = jax.ShapeDtypeStruct((n_prev + 1, T // S, ATT_COLS, S), F32)
        v_shape = jax.ShapeDtypeStruct((n_prev + 1, ATT_HEADS * T, ATT_V_DIM), F32)
        k_spec, v_spec = k_block(n_prev + 1), v_block(n_prev + 1)
    out_shape = (
        jax.ShapeDtypeStruct((T, ATT_COLS), BF16),
        k_shape,
        jax.ShapeDtypeStruct((T, ATT_COLS), BF16),
        v_shape,
        jax.ShapeDtypeStruct((T, ATT_COLS), BF16),
        jax.ShapeDtypeStruct((T, SSM_INNER), F32),
        jax.ShapeDtypeStruct((T, CONV_DIM), F32),
        jax.ShapeDtypeStruct((T, SSM_HEADS), F32),
        jax.ShapeDtypeStruct((SSM_HEADS, T), F32),
    )
    out_specs = (row(ATT_COLS), k_spec, row(ATT_COLS), v_spec, row(ATT_COLS),
                 row(SSM_INNER), row(CONV_DIM), row(SSM_HEADS),
                 pl.BlockSpec((SSM_HEADS, tm), lambda i: (0, i)))
    return pl.pallas_call(
        functools.partial(_in_proj_kernel, cache_layout=cache is not None),
        grid=(T // tm,),
        in_specs=[row(D), pos, pos, _full(wq.shape), _full(wk.shape), _full(wv.shape), _full(wz.shape),
                  _full(wxbc.shape), _full(wdt.shape), _full(wdtT.shape)] + prev_specs,
        out_specs=out_specs,
        out_shape=out_shape,
        compiler_params=_cparams("parallel"),
        name="in_proj",
    )(x, cos_t, sin_t, wq, wk, wv, wz, wxbc, wdt, wdtT, *prev)


def _gated_group_norm(y, z, norm_w):
    y = y * _silu(z)
    half = SSM_INNER // SSM_GROUPS
    y2 = y * y
    ms0 = jnp.mean(y2[:, :half], axis=-1, keepdims=True)
    ms1 = jnp.mean(y2[:, half:], axis=-1, keepdims=True)
    lane = lax.broadcasted_iota(jnp.int32, y.shape, 1)
    scale = jnp.where(lane < half, lax.rsqrt(ms0 + NORM_EPS), lax.rsqrt(ms1 + NORM_EPS))
    return y * scale * norm_w


def _ssd_kernel(xbc_ref, z_ref, dt_ref, dtT_ref, convw_ref, convb_ref, dtb_ref, dtbT_ref, alog_ref, alogT_ref,
                dskip_ref, normw_ref, y_ref, conv_out_ref, state_out_ref, ext_ref, h_ref):
    Q = SSD_CHUNK
    c = pl.program_id(1)
    last = pl.num_programs(1) - 1

    @pl.when(c == 0)
    def _():
        ext_ref[0:8, :] = jnp.zeros((8, CONV_DIM), F32)
        h_ref[...] = jnp.zeros(h_ref.shape, F32)

    ext_ref[8:8 + Q, :] = xbc_ref[...]
    conv = convb_ref[...]
    for i in range(CONV_WIDTH):
        conv = conv + ext_ref[5 + i:5 + i + Q, :] * convw_ref[i:i + 1, :]
    ext_ref[0:8, :] = xbc_ref[Q - 8:Q, :]
    conv = _silu(conv)
    xs = conv[:, :SSM_INNER]
    bmat = conv[:, SSM_INNER:SSM_INNER + SSM_GROUPS * SSM_STATE]
    cmat = conv[:, SSM_INNER + SSM_GROUPS * SSM_STATE:]

    dt_col = _softplus(dt_ref[...] + dtb_ref[...])
    a_col = dt_col * (-jnp.exp(alog_ref[...]))
    a_row = _softplus(dtT_ref[...] + dtbT_ref[...]) * (-jnp.exp(alogT_ref[...]))
    ri = lax.broadcasted_iota(jnp.int32, (Q, Q), 0)
    ci = lax.broadcasted_iota(jnp.int32, (Q, Q), 1)
    causal = ci <= ri
    tril = causal.astype(F32)
    triu = (ri <= ci).astype(F32)
    acs_col = jnp.dot(tril, a_col, preferred_element_type=F32, precision=lax.Precision.HIGHEST)
    acs_row = jnp.dot(a_row, triu, preferred_element_type=F32, precision=lax.Precision.HIGHEST)

    lo = ci < SSM_HEAD_DIM
    top = ri < SSM_HEAD_DIM
    pairs = []
    for k in range(SSM_HEADS // 2):
        h0, h1 = 2 * k, 2 * k + 1
        g = k // (SSM_HEADS // 2 // SSM_GROUPS)
        b_g = bmat[:, g * SSM_STATE:(g + 1) * SSM_STATE].astype(BF16)
        c_g = cmat[:, g * SSM_STATE:(g + 1) * SSM_STATE].astype(BF16)
        cb = lax.dot_general(c_g, b_g, NT_DIMS, preferred_element_type=F32)
        xs_pair = xs[:, 2 * SSM_HEAD_DIM * k:2 * SSM_HEAD_DIM * (k + 1)]
        dtc = jnp.where(lo, dt_col[:, h0:h0 + 1], dt_col[:, h1:h1 + 1])
        xdt = xs_pair * dtc
        acol = jnp.where(lo, acs_col[:, h0:h0 + 1], acs_col[:, h1:h1 + 1])
        alast0 = acs_row[h0:h0 + 1, Q - 1:Q]
        alast1 = acs_row[h1:h1 + 1, Q - 1:Q]
        alast = jnp.where(lo[0:1, :], alast0, alast1)
        xw = xdt * jnp.exp(alast - acol)
        y_pair = jnp.where(lo, dskip_ref[0:1, h0:h0 + 1], dskip_ref[0:1, h1:h1 + 1]) * xs_pair
        for h, sel in ((h0, lo), (h1, jnp.logical_not(lo))):
            seg = acs_col[:, h:h + 1] - acs_row[h:h + 1, :]
            decay = jnp.exp(jnp.where(causal, seg, -jnp.inf))
            m = (cb * decay).astype(BF16)
            y_pair = y_pair + jnp.dot(m, jnp.where(sel, xdt, 0.0).astype(BF16), preferred_element_type=F32)
        states = jnp.dot(xw.T.astype(BF16), b_g, preferred_element_type=F32)
        h_prev = h_ref[k]
        y_off = lax.dot_general(c_g, h_prev.astype(BF16), NT_DIMS, preferred_element_type=F32)
        y_pair = y_pair + y_off * jnp.exp(acol)
        dec = jnp.where(top[:, 0:1], jnp.exp(alast0), jnp.exp(alast1))
        h_ref[k] = h_prev * dec + states
        pairs.append(y_pair)
    y = jnp.concatenate(pairs, axis=1)
    y_ref[...] = _gated_group_norm(y, z_ref[...], normw_ref[...]).astype(BF16)

    @pl.when(c == last)
    def _():
        conv_out_ref[0] = xbc_ref[Q - (CONV_WIDTH - 1):Q, :]
        state_out_ref[0] = h_ref[...]


def _ssd(xbc, z, dt, dtT, sw, B, S):
    convw, convb, dtb, dtbT, alog, alogT, dskip, normw = sw
    nc = S // SSD_CHUNK
    Q = SSD_CHUNK
    rowb = lambda n: pl.BlockSpec((Q, n), lambda b, c: (b * nc + c, 0))
    npair = SSM_HEADS // 2
    return pl.pallas_call(
        _ssd_kernel,
        grid=(B, nc),
        in_specs=[rowb(CONV_DIM), rowb(SSM_INNER), rowb(SSM_HEADS),
                  pl.BlockSpec((SSM_HEADS, Q), lambda b, c: (0, b * nc + c)),
                  _full(convw.shape), _full(convb.shape), _full(dtb.shape), _full(dtbT.shape),
                  _full(alog.shape), _full(alogT.shape), _full(dskip.shape), _full(normw.shape)],
        out_specs=(rowb(SSM_INNER),
                   pl.BlockSpec((1, CONV_WIDTH - 1, CONV_DIM), lambda b, c: (b, 0, 0)),
                   pl.BlockSpec((1, npair, 2 * SSM_HEAD_DIM, SSM_STATE), lambda b, c: (b, 0, 0, 0))),
        out_shape=(jax.ShapeDtypeStruct((B * S, SSM_INNER), BF16),
                   jax.ShapeDtypeStruct((B, CONV_WIDTH - 1, CONV_DIM), F32),
                   jax.ShapeDtypeStruct((B, npair, 2 * SSM_HEAD_DIM, SSM_STATE), F32)),
        scratch_shapes=[pltpu.VMEM((Q + 8, CONV_DIM), F32),
                        pltpu.VMEM((npair, 2 * SSM_HEAD_DIM, SSM_STATE), F32)],
        compiler_params=_cparams("parallel", "arbitrary"),
        name="ssd_scan",
    )(xbc, z, dt, dtT, convw, convb, dtb, dtbT, alog, alogT, dskip, normw)


def _ssd_step_kernel(xbc_ref, z_ref, dt_ref, cstate_ref, hstate_ref, convw_ref, convb_ref, dtb_ref, alog_ref,
                     dskip_ref, normw_ref, *rest):
    y_ref, cout_ref, hout_ref = rest[-3:]
    n_prev = hout_ref.shape[0] - 1
    if n_prev:
        hout_ref[0:n_prev] = rest[0][...]
    bt = xbc_ref.shape[0]
    P, N = SSM_HEAD_DIM, SSM_STATE
    new = xbc_ref[...]
    conv = convb_ref[...] + new * convw_ref[CONV_WIDTH - 1:CONV_WIDTH, :]
    for i in range(CONV_WIDTH - 1):
        conv = conv + cstate_ref[:, i * CONV_DIM:(i + 1) * CONV_DIM] * convw_ref[i:i + 1, :]
    cout_ref[:, 0:(CONV_WIDTH - 2) * CONV_DIM] = cstate_ref[:, CONV_DIM:(CONV_WIDTH - 1) * CONV_DIM]
    cout_ref[:, (CONV_WIDTH - 2) * CONV_DIM:] = new
    conv = _silu(conv)
    xs = conv[:, :SSM_INNER]
    bmat = conv[:, SSM_INNER:SSM_INNER + SSM_GROUPS * N]
    cmat = conv[:, SSM_INNER + SSM_GROUPS * N:]
    dt = _softplus(dt_ref[...] + dtb_ref[...])
    dec = jnp.exp(dt * (-jnp.exp(alog_ref[...])))
    H = SSM_HEADS
    HP = H * P
    exact = lax.Precision.HIGHEST
    eye_n = (lax.broadcasted_iota(jnp.int32, (N, N), 0) == lax.broadcasted_iota(jnp.int32, (N, N), 1)).astype(F32)
    x_t = jnp.concatenate(
        [lax.dot_general(eye_n, xs[:, c * N:(c + 1) * N], NT_DIMS, preferred_element_type=F32, precision=exact)
         for c in range(HP // N)], axis=0)
    rep = (lax.broadcasted_iota(jnp.int32, (HP, H), 0) // P == lax.broadcasted_iota(jnp.int32, (HP, H), 1)).astype(F32)
    per_head = jnp.concatenate([dt, dec, jnp.broadcast_to(dskip_ref[...], (8, H))], axis=0)
    cols = lax.dot_general(rep, per_head, NT_DIMS, preferred_element_type=F32, precision=exact)
    d_col = cols[:, 2 * bt:2 * bt + 1]
    first_group = lax.broadcasted_iota(jnp.int32, (HP, 1), 0) < HP // SSM_GROUPS
    lane_b = lax.broadcasted_iota(jnp.int32, (HP, bt), 1)
    y_t = jnp.zeros((HP, bt), F32)
    for b in range(bt):
        x_col = x_t[:, b:b + 1]
        b_sel = jnp.where(first_group, bmat[b:b + 1, 0:N], bmat[b:b + 1, N:2 * N])
        h_new = (hstate_ref[b].reshape(HP, N) * cols[:, bt + b:bt + b + 1]
                 + (x_col * cols[:, b:b + 1]) * b_sel)
        hout_ref[n_prev, b] = h_new.reshape(H, P, N)
        c2 = jnp.concatenate([cmat[b:b + 1, 0:N], cmat[b:b + 1, N:2 * N], jnp.zeros((6, N), F32)], axis=0)
        yb = lax.dot_general(h_new.astype(BF16), c2.astype(BF16), NT_DIMS, preferred_element_type=F32)
        y_col = jnp.where(first_group, yb[:, 0:1], yb[:, 1:2]) + d_col * x_col
        y_t = jnp.where(lane_b == b, y_col, y_t)
    eye_b = (lax.broadcasted_iota(jnp.int32, (bt, bt), 0) == lax.broadcasted_iota(jnp.int32, (bt, bt), 1)).astype(F32)
    y = lax.dot_general(eye_b, y_t, NT_DIMS, preferred_element_type=F32, precision=exact)
    y_ref[...] = _gated_group_norm(y, z_ref[...], normw_ref[...]).astype(BF16)


def _ssd_step(xbc, z, dt, cstate, hstate, layer, sw, h_prev=None, bt=8):
    convw, convb, dtb, _, alog, _, dskip, normw = sw
    Bs = xbc.shape[0]
    cw = (CONV_WIDTH - 1) * CONV_DIM
    rowb = lambda n: pl.BlockSpec((bt, n), lambda i: (i, 0))
    hshape = (bt, SSM_HEADS, SSM_HEAD_DIM, SSM_STATE)
    layers = lambda n: pl.BlockSpec((n,) + hshape, lambda i: (0, i, 0, 0, 0))
    prev = () if h_prev is None else (h_prev,)
    n_prev = 0 if h_prev is None else h_prev.shape[0]
    return pl.pallas_call(
        _ssd_step_kernel,
        grid=(Bs // bt,),
        in_specs=[rowb(CONV_DIM), rowb(SSM_INNER), rowb(SSM_HEADS),
                  pl.BlockSpec((None, bt, cw), lambda i: (layer, i, 0)),
                  pl.BlockSpec((None,) + hshape, lambda i: (layer, i, 0, 0, 0)),
                  _full(convw.shape), _full(convb.shape), _full(dtb.shape), _full(alog.shape),
                  _full(dskip.shape), _full(normw.shape)] + [layers(n_prev)] * len(prev),
        out_specs=(rowb(SSM_INNER), rowb(cw), layers(n_prev + 1)),
        out_shape=(jax.ShapeDtypeStruct((Bs, SSM_INNER), BF16),
                   jax.ShapeDtypeStruct((Bs, cw), F32),
                   jax.ShapeDtypeStruct((n_prev + 1,) + hstate.shape[1:], F32)),
        compiler_params=_cparams("parallel"),
        name="ssd_step",
    )(xbc, z, dt, cstate, hstate, convw, convb, dtb, alog, dskip, normw, *prev)


def _lambda_value(lp, lam_init):
    d1 = jnp.sum(lp[0:1, :] * lp[1:2, :], axis=1, keepdims=True)
    d2 = jnp.sum(lp[2:3, :] * lp[3:4, :], axis=1, keepdims=True)
    return jnp.exp(d1) - jnp.exp(d2) + lam_init


def _head_norm(o, nw, lam_init):
    return o * lax.rsqrt(jnp.mean(o * o, axis=-1, keepdims=True) + NORM_EPS) * nw * (1.0 - lam_init)


def _attn_kernel(lp_ref, nw_ref, q_ref, k_ref, v_ref, o_ref, m_ref, l_ref, acc_ref, *, lam_init):
    tq = q_ref.shape[0]
    n_heads = q_ref.shape[1] // ATT_V_DIM
    qi = pl.program_id(2)
    lane = lax.broadcasted_iota(jnp.int32, (tq, ATT_V_DIM), 1)
    q_maps = []
    for h in range(n_heads):
        q = q_ref[:, h * ATT_V_DIM:(h + 1) * ATT_V_DIM]
        zero = jnp.zeros_like(q)
        q_maps += [jnp.where(lane < ATT_QK_DIM, q, zero), jnp.where(lane >= ATT_QK_DIM, q, zero)]
    m_ref[...] = jnp.full(m_ref.shape, -jnp.inf, F32)
    l_ref[...] = jnp.zeros(l_ref.shape, F32)
    acc_ref[...] = jnp.zeros(acc_ref.shape, F32)
    below_diag = (lax.broadcasted_iota(jnp.int32, (tq, tq), 1) <= lax.broadcasted_iota(jnp.int32, (tq, tq), 0))

    def block(j, diagonal):
        start = pl.multiple_of(j * tq, tq)
        for i in range(2 * n_heads):
            cols = slice((i // 2) * ATT_V_DIM, (i // 2 + 1) * ATT_V_DIM)
            k = k_ref[pl.ds(start, tq), cols]
            v = v_ref[pl.ds(start, tq), cols]
            s = lax.dot_general(q_maps[i], k, NT_DIMS, preferred_element_type=F32)
            if diagonal:
                s = jnp.where(below_diag, s, -jnp.inf)
            chunks = [s[:, c * LANES:(c + 1) * LANES] for c in range(tq // LANES)]
            top = functools.reduce(jnp.maximum, chunks)
            m_old = m_ref[i]
            m_new = jnp.maximum(m_old, jnp.max(top, axis=1, keepdims=True))
            alpha = jnp.exp(m_old - m_new)
            ps = [jnp.exp(c - m_new) for c in chunks]
            l_ref[i] = alpha * l_ref[i] + functools.reduce(jnp.add, ps)
            p = jnp.concatenate(ps, axis=1).astype(BF16)
            acc_ref[i] = alpha * acc_ref[i] + jnp.dot(p, v, preferred_element_type=F32)
            m_ref[i] = m_new

    def full_block(j, carry):
        block(j, False)
        return carry

    lax.fori_loop(0, qi, full_block, 0)
    block(qi, True)
    lam = _lambda_value(lp_ref[...], lam_init)
    for h in range(n_heads):
        l1 = jnp.sum(l_ref[2 * h], axis=1, keepdims=True)
        l2 = jnp.sum(l_ref[2 * h + 1], axis=1, keepdims=True)
        o = acc_ref[2 * h] / l1 - lam * (acc_ref[2 * h + 1] / l2)
        o_ref[:, h * ATT_V_DIM:(h + 1) * ATT_V_DIM] = _head_norm(o, nw_ref[...], lam_init).astype(BF16)


def _attention(q, k, v, lp, nw, lam_init, B, S):
    tq = min(512, S)
    nq = S // tq
    hg = ATT_HEADS_PER_STEP
    qspec = pl.BlockSpec((tq, hg * ATT_V_DIM), lambda b, h, i: (b * nq + i, h))
    kspec = pl.BlockSpec((S, hg * ATT_V_DIM), lambda b, h, i: (b, h))
    return pl.pallas_call(
        functools.partial(_attn_kernel, lam_init=lam_init),
        grid=(B, ATT_HEADS // hg, nq),
        in_specs=[_full(lp.shape), _full(nw.shape), qspec, kspec, kspec],
        out_specs=qspec,
        out_shape=jax.ShapeDtypeStruct((B * S, ATT_COLS), BF16),
        scratch_shapes=[pltpu.VMEM((2 * hg, tq, LANES), F32), pltpu.VMEM((2 * hg, tq, LANES), F32),
                        pltpu.VMEM((2 * hg, tq, ATT_V_DIM), F32)],
        compiler_params=_cparams("parallel", "parallel", "arbitrary"),
        name="diff_attention",
    )(lp, nw, q, k, v)


def _decode_attn_kernel(pt_ref, lp_ref, nw_ref, q_ref, kn_ref, vn_ref, *refs, n_pages, lam_init):
    k_refs = refs[:n_pages]
    v_refs = refs[n_pages:2 * n_pages]
    o_ref = refs[2 * n_pages]
    R = 2 * ATT_HEADS
    page = k_refs[0].shape[2]
    r = lax.broadcasted_iota(jnp.int32, (R, ATT_COLS), 0)
    grp = lax.broadcasted_iota(jnp.int32, (R, ATT_COLS), 1) // ATT_QK_DIM
    target = jnp.where(r < ATT_HEADS, 2 * r, 2 * (r - ATT_HEADS) + 1)
    qf = jnp.where(grp == target, q_ref[0].astype(BF16).astype(F32), 0.0)
    qt = qf.astype(BF16)
    s = [jnp.dot(qt, k_ref[...].reshape(ATT_COLS, page).astype(BF16), preferred_element_type=F32)
         for k_ref in k_refs]
    s_new = jnp.sum(qf * kn_ref[0].astype(BF16).astype(F32), axis=1, keepdims=True)
    m = s_new
    for sj in s:
        m = jnp.maximum(m, jnp.max(sj, axis=1, keepdims=True))
    p = [jnp.exp(sj - m) for sj in s]
    p_new = jnp.exp(s_new - m)
    l = p_new
    for pj in p:
        l = l + jnp.sum(pj, axis=1, keepdims=True)
    inv_l = 1.0 / l
    lam = _lambda_value(lp_ref[...], lam_init)
    diff = lambda t: t - lam * pltpu.roll(t, ATT_HEADS, 0)
    a = [diff(pj * inv_l).astype(BF16) for pj in p]
    a_new = diff(jnp.broadcast_to(p_new * inv_l, (R, 128)))[:, 0:1].astype(BF16).astype(F32)
    row = lax.broadcasted_iota(jnp.int32, (R, ATT_V_DIM), 0)
    out = a_new * vn_ref[0].astype(BF16).astype(F32)
    for h in range(ATT_HEADS):
        acc = jnp.zeros((R, ATT_V_DIM), F32)
        for aj, v_ref in zip(a, v_refs):
            v_h = v_ref[pl.ds(h, page, stride=ATT_HEADS), :].astype(BF16)
            acc = acc + jnp.dot(aj, v_h, preferred_element_type=F32)
        out = out + jnp.where(row == h, acc, 0.0)
    o_ref[0] = _head_norm(out, nw_ref[...], lam_init)


def _decode_attention(pt, layer, q, k_new, v_new, cache_k, cache_v, lp, nw, lam_init):
    Bs, n_pages = pt.shape
    R = 2 * ATT_HEADS
    qspec = pl.BlockSpec((1, 1, ATT_COLS), lambda b, pt: (b, 0, 0))
    vspec = pl.BlockSpec((1, R, ATT_V_DIM), lambda b, pt: (b, 0, 0))
    kpages = [pl.BlockSpec((None, None) + cache_k.shape[2:], lambda b, pt, j=j: (layer, pt[b, j], 0, 0, 0))
              for j in range(n_pages)]
    vpages = [pl.BlockSpec((None, None) + cache_v.shape[2:], lambda b, pt, j=j: (layer, pt[b, j], 0, 0))
              for j in range(n_pages)]
    grid_spec = pltpu.PrefetchScalarGridSpec(
        num_scalar_prefetch=1,
        grid=(Bs,),
        in_specs=[pl.BlockSpec(lp.shape, lambda b, pt: (0, 0)), pl.BlockSpec(nw.shape, lambda b, pt: (0, 0)),
                  qspec, qspec, vspec] + kpages + vpages,
        out_specs=vspec,
    )
    return pl.pallas_call(
        functools.partial(_decode_attn_kernel, n_pages=n_pages, lam_init=lam_init),
        grid_spec=grid_spec,
        out_shape=jax.ShapeDtypeStruct((Bs, R, ATT_V_DIM), F32),
        compiler_params=_cparams("parallel"),
        name="decode_attention",
    )(pt, lp, nw, q, k_new, v_new, *([cache_k] * n_pages), *([cache_v] * n_pages))


def _proj_ln_kernel(*refs, n_in, alpha):
    a_refs = refs[:n_in]
    w_refs = refs[n_in:2 * n_in]
    x_ref, g_ref, b_ref, o_ref = refs[2 * n_in:]
    acc = alpha * x_ref[...]
    for a_ref, w_ref in zip(a_refs, w_refs):
        acc = acc + jnp.dot(a_ref[...].astype(BF16), w_ref[...], preferred_element_type=F32)
    o_ref[...] = _layer_norm(acc, g_ref[...], b_ref[...])


def _proj_ln(acts, ws, x, g, b, alpha, tm):
    T, D = x.shape
    row = lambda n: pl.BlockSpec((tm, n), lambda i: (i, 0))
    return pl.pallas_call(
        functools.partial(_proj_ln_kernel, n_in=len(acts), alpha=alpha),
        grid=(T // tm,),
        in_specs=[row(a.shape[1]) for a in acts] + [_full(w.shape) for w in ws] + [row(D), _full(g.shape), _full(b.shape)],
        out_specs=row(D),
        out_shape=jax.ShapeDtypeStruct((T, D), F32),
        compiler_params=_cparams("parallel"),
        name="proj_ln",
    )(*acts, *ws, x, g, b)


def _matmul_kernel(x_ref, w_ref, *o_refs):
    y = jnp.dot(x_ref[...].astype(BF16), w_ref[...], preferred_element_type=F32)
    for o_ref in o_refs:
        o_ref[...] = y.astype(o_ref.dtype)


def _matmul(x, w, out_dtypes, tm, tn):
    M, K = x.shape
    N = w.shape[1]
    ospec = pl.BlockSpec((tm, tn), lambda i, j: (i, j))
    return pl.pallas_call(
        _matmul_kernel,
        grid=(M // tm, N // tn),
        in_specs=[pl.BlockSpec((tm, K), lambda i, j: (i, 0)), pl.BlockSpec((K, tn), lambda i, j: (0, j))],
        out_specs=tuple(ospec for _ in out_dtypes),
        out_shape=tuple(jax.ShapeDtypeStruct((M, N), d) for d in out_dtypes),
        compiler_params=_cparams("parallel", "parallel"),
        name="matmul",
    )(x, w)


def _mem_kv_kernel(x_ref, w_ref, *rest):
    mk_ref, mv_ref, mkb_ref, mvb_ref = rest[-4:]
    tm = x_ref.shape[0]
    D = w_ref.shape[1] // 2
    dh = D // MEM_HEADS
    halves = dh // LANES
    y = jnp.dot(x_ref[...].astype(BF16), w_ref[...], preferred_element_type=F32)
    mkb_ref[...] = y[:, :D].astype(BF16)
    mvb_ref[...] = y[:, D:].astype(BF16)
    n_prev = mk_ref.shape[0] - 1
    if n_prev:
        mk_ref[0:n_prev] = rest[0][...]
        mv_ref[0:n_prev] = rest[1][...]
    for out_ref, base in ((mk_ref, 0), (mv_ref, D)):
        for h in range(MEM_HEADS):
            for c in range(halves):
                col = base + h * dh + c * LANES
                out_ref[n_prev, pl.ds(c * MEM_HEADS + h, tm, stride=halves * MEM_HEADS), :] = y[:, col:col + LANES]


def _mem_kv(mem, w, prev, tm):
    M, D = mem.shape
    rows_per_token = D // LANES
    rows = lambda n: pl.BlockSpec((n, tm * rows_per_token, LANES), lambda i: (0, i, 0))
    flat = pl.BlockSpec((tm, D), lambda i: (i, 0))
    prev = () if prev is None else tuple(prev)
    n_prev = prev[0].shape[0] if prev else 0
    rows_shape = jax.ShapeDtypeStruct((n_prev + 1, M * rows_per_token, LANES), F32)
    return pl.pallas_call(
        _mem_kv_kernel,
        grid=(M // tm,),
        in_specs=[flat, _full(w.shape)] + [rows(n_prev)] * len(prev),
        out_specs=(rows(n_prev + 1), rows(n_prev + 1), flat, flat),
        out_shape=(rows_shape, rows_shape, jax.ShapeDtypeStruct((M, D), BF16), jax.ShapeDtypeStruct((M, D), BF16)),
        compiler_params=_cparams("parallel"),
        name="mem_kv",
    )(mem, w, *prev)


def _softmax_rows(s):
    m = jnp.max(s, axis=1, keepdims=True)
    p = jnp.exp(s - m)
    return p / jnp.sum(p, axis=1, keepdims=True)


def _cross_kernel(x_ref, wq_ref, mk_ref, mv_ref, wo_ref, g_ref, b_ref, o_ref, *, alpha):
    x = x_ref[...]
    D = x.shape[1]
    dh = D // MEM_HEADS
    q = jnp.dot(x.astype(BF16), wq_ref[...], preferred_element_type=F32).astype(BF16)
    outs = []
    for h in range(MEM_HEADS):
        sl = slice(h * dh, (h + 1) * dh)
        s = lax.dot_general(q[:, sl], mk_ref[:, sl], NT_DIMS, preferred_element_type=F32) * (dh ** -0.5)
        outs.append(jnp.dot(_softmax_rows(s).astype(BF16), mv_ref[:, sl], preferred_element_type=F32).astype(BF16))
    o = jnp.concatenate(outs, axis=1)
    y = alpha * x + jnp.dot(o, wo_ref[...], preferred_element_type=F32)
    o_ref[...] = _layer_norm(y, g_ref[...], b_ref[...])


def _cross_attention(x, wq, mk, mv, wo, g, b, alpha, B, S, n_mem, tm):
    T, D = x.shape
    nt = S // tm
    row = pl.BlockSpec((tm, D), lambda bi, i: (bi * nt + i, 0))
    mem = pl.BlockSpec((n_mem, D), lambda bi, i: (bi, 0))
    return pl.pallas_call(
        functools.partial(_cross_kernel, alpha=alpha),
        grid=(B, nt),
        in_specs=[row, _full(wq.shape), mem, mem, _full(wo.shape), _full(g.shape), _full(b.shape)],
        out_specs=row,
        out_shape=jax.ShapeDtypeStruct((T, D), F32),
        compiler_params=_cparams("parallel", "parallel"),
        name="cross_attention",
    )(x, wq, mk, mv, wo, g, b)


def _cross_decode_kernel(q_ref, mk_ref, mv_ref, o_ref):
    R = q_ref.shape[1] // 2
    rows_kv = mk_ref.shape[1]
    dh = 2 * LANES
    own = (lax.broadcasted_iota(jnp.int32, (R, rows_kv), 1) & (R - 1)) == lax.broadcasted_iota(jnp.int32, (R, rows_kv), 0)
    for b in range(q_ref.shape[0]):
        part = lax.dot_general(q_ref[b].astype(BF16), mk_ref[b].astype(BF16), NT_DIMS,
                               preferred_element_type=F32)
        s = (part[:R] + pltpu.roll(part[R:], rows_kv - MEM_HEADS, 1)) * (dh ** -0.5)
        p = _softmax_rows(jnp.where(own, s, -jnp.inf))
        a = jnp.concatenate([p, pltpu.roll(p, MEM_HEADS, 1)], axis=0).astype(BF16)
        o_ref[b] = jnp.dot(a, mv_ref[b].astype(BF16), preferred_element_type=F32)


def _cross_decode(q, mem_k, mem_v, layer, bt=2):
    Bs, R2, _ = q.shape
    tok = pl.BlockSpec((bt, R2, LANES), lambda b: (b, 0, 0))
    mem = pl.BlockSpec((None, bt) + mem_k.shape[2:], lambda b: (layer, b, 0, 0))
    return pl.pallas_call(
        _cross_decode_kernel,
        grid=(Bs // bt,),
        in_specs=[tok, mem, mem],
        out_specs=tok,
        out_shape=jax.ShapeDtypeStruct((Bs, R2, LANES), F32),
        compiler_params=_cparams("parallel"),
        name="cross_decode",
    )(q, mem_k, mem_v)


def _ffn_kernel(x_ref, wg_ref, wu_ref, wd_ref, g_ref, b_ref, o_ref, acc_ref, *, alpha):
    f = pl.program_id(1)
    x = x_ref[...]
    xb = x.astype(BF16)

    @pl.when(f == 0)
    def _():
        acc_ref[...] = alpha * x

    gate = jnp.dot(xb, wg_ref[...], preferred_element_type=F32)
    up = jnp.dot(xb, wu_ref[...], preferred_element_type=F32)
    acc_ref[...] += jnp.dot((_silu(gate) * up).astype(BF16), wd_ref[...], preferred_element_type=F32)

    @pl.when(f == pl.num_programs(1) - 1)
    def _():
        o_ref[...] = _layer_norm(acc_ref[...], g_ref[...], b_ref[...])


def _ffn(x, w_gu, w_down, g, b, alpha, tm, tf):
    T, D = x.shape
    F = w_down.shape[0]
    nf = F // tf
    row = pl.BlockSpec((tm, D), lambda i, f: (i, 0))
    return pl.pallas_call(
        functools.partial(_ffn_kernel, alpha=alpha),
        grid=(T // tm, nf),
        in_specs=[row, pl.BlockSpec((D, tf), lambda i, f: (0, f)), pl.BlockSpec((D, tf), lambda i, f: (0, nf + f)),
                  pl.BlockSpec((tf, D), lambda i, f: (f, 0)), _full(g.shape), _full(b.shape)],
        out_specs=row,
        out_shape=jax.ShapeDtypeStruct((T, D), F32),
        scratch_shapes=[pltpu.VMEM((tm, D), F32)],
        compiler_params=_cparams("parallel", "arbitrary"),
        name="ffn",
    )(x, w_gu, w_gu, w_down, g, b)


def _router_kernel(x_ref, wr_ref, gate_ref, top_i_ref, top_g_ref):
    logits = jnp.dot(x_ref[...].astype(BF16), wr_ref[...], preferred_element_type=F32)
    lane = lax.broadcasted_iota(jnp.int32, logits.shape, 1).astype(F32)
    none = float(N_EXPERTS)
    m1 = jnp.max(logits, axis=1, keepdims=True)
    i1 = jnp.min(jnp.where(logits == m1, lane, none), axis=1, keepdims=True)
    rest = jnp.where(lane == i1, -jnp.inf, logits)
    m2 = jnp.max(rest, axis=1, keepdims=True)
    i2 = jnp.min(jnp.where(rest == m2, lane, none), axis=1, keepdims=True)
    e = jnp.exp(m2 - m1)
    den = 1.0 + e
    gate_ref[...] = jnp.where(lane == i1, 1.0 / den, 0.0) + jnp.where(lane == i2, e / den, 0.0)
    first = lax.broadcasted_iota(jnp.int32, top_i_ref.shape, 1) == 0
    top_i_ref[...] = jnp.where(first, i1, i2).astype(jnp.int32)
    top_g_ref[...] = jnp.where(first, 1.0 / den, e / den)


def _router(x, wr, tm):
    T, D = x.shape
    top = pl.BlockSpec((tm, 2), lambda i: (i, 0))
    return pl.pallas_call(
        _router_kernel,
        grid=(T // tm,),
        in_specs=[pl.BlockSpec((tm, D), lambda i: (i, 0)), _full(wr.shape)],
        out_specs=(pl.BlockSpec((tm, N_EXPERTS), lambda i: (i, 0)), top, top),
        out_shape=(jax.ShapeDtypeStruct((T, N_EXPERTS), F32), jax.ShapeDtypeStruct((T, 2), jnp.int32),
                   jax.ShapeDtypeStruct((T, 2), F32)),
        compiler_params=_cparams("parallel"),
        name="router",
    )(x, wr)


def _moe_plan(top_i, tm_e, n_tiles):
    T = top_i.shape[0]
    chosen = (top_i[:, :, None] == jnp.arange(N_EXPERTS, dtype=jnp.int32)[None, None, :]).any(axis=1)
    chosen = chosen.astype(jnp.int32)
    count = chosen.sum(axis=0)
    rank = jnp.cumsum(chosen, axis=0) - chosen
    tiles = (count + tm_e - 1) // tm_e
    tile_end = jnp.cumsum(tiles)
    row_start = (tile_end - tiles) * tm_e
    dest = jnp.take_along_axis(row_start[None, :] + rank, top_i, axis=1).astype(jnp.int32)
    token = jnp.broadcast_to(jnp.arange(T, dtype=jnp.int32)[:, None], (T, 2))
    src = jnp.zeros((n_tiles * tm_e,), jnp.int32).at[dest.reshape(-1)].set(token.reshape(-1))
    n_used = tile_end[-1]
    tile = jnp.arange(n_tiles, dtype=jnp.int32)
    tile_expert = (tile_end[None, :] <= jnp.minimum(tile, n_used - 1)[:, None]).sum(axis=1).astype(jnp.int32)
    return src, dest, tile_expert, n_used.reshape(1).astype(jnp.int32)


def _row_copy(src_hbm, src_row, dst_ref, dst_row, sem):
    return pltpu.make_async_copy(src_hbm.at[pl.ds(src_row, 1), :], dst_ref.at[pl.ds(dst_row, 1), :], sem)


def _moe_ffn_kernel(te_ref, nv_ref, src_ref, x_hbm, wg_ref, wu_ref, wd_ref, y_ref, xrows_ref, xb_ref, sem):
    tm_e = xrows_ref.shape[1]
    i = pl.program_id(0)
    f = pl.program_id(1)
    n_used = nv_ref[0]
    slot = lax.rem(i, 2)

    def start_gather(tile, slot):
        def body(r8, carry):
            for u in range(ROW_COPY_UNROLL):
                r = r8 * ROW_COPY_UNROLL + u
                _row_copy(x_hbm, src_ref[tile * tm_e + r], xrows_ref.at[slot], r, sem.at[slot]).start()
            return carry
        lax.fori_loop(0, tm_e // ROW_COPY_UNROLL, body, 0)

    @pl.when(f == 0)
    def _():
        @pl.when(i == 0)
        def _():
            start_gather(0, 0)

        @pl.when(i < n_used)
        def _():
            pltpu.make_async_copy(x_hbm.at[pl.ds(0, tm_e), :], xrows_ref.at[slot], sem.at[slot]).wait()
            xb_ref[...] = xrows_ref[slot].astype(BF16)

        @pl.when(i + 1 < n_used)
        def _():
            start_gather(i + 1, 1 - slot)

    @pl.when(i < n_used)
    def _():
        xb = xb_ref[...]
        gate = jnp.dot(xb, wg_ref[...], preferred_element_type=F32)
        up = jnp.dot(xb, wu_ref[...], preferred_element_type=F32)
        part = jnp.dot((_silu(gate) * up).astype(BF16), wd_ref[...], preferred_element_type=F32)

        @pl.when(f == 0)
        def _():
            y_ref[...] = part

        @pl.when(f > 0)
        def _():
            y_ref[...] += part

    @pl.when(jnp.logical_and(i >= n_used, f == 0))
    def _():
        y_ref[...] = jnp.zeros(y_ref.shape, F32)


def _moe_ffn(x, src, tile_expert, n_used, w_gu, w_down, tm_e, tf):
    T, D = x.shape
    E, F, _ = w_down.shape
    nf = F // tf
    n_tiles = tile_expert.shape[0]
    grid_spec = pltpu.PrefetchScalarGridSpec(
        num_scalar_prefetch=3,
        grid=(n_tiles, nf),
        in_specs=[pl.BlockSpec(memory_space=pl.ANY),
                  pl.BlockSpec((None, D, tf), lambda i, f, te, nv, src: (te[i], 0, f)),
                  pl.BlockSpec((None, D, tf), lambda i, f, te, nv, src: (te[i], 0, nf + f)),
                  pl.BlockSpec((None, tf, D), lambda i, f, te, nv, src: (te[i], f, 0))],
        out_specs=pl.BlockSpec((tm_e, D), lambda i, f, te, nv, src: (i, 0)),
        scratch_shapes=[pltpu.VMEM((2, tm_e, D), F32), pltpu.VMEM((tm_e, D), BF16), pltpu.SemaphoreType.DMA((2,))],
    )
    return pl.pallas_call(
        _moe_ffn_kernel,
        grid_spec=grid_spec,
        out_shape=jax.ShapeDtypeStruct((n_tiles * tm_e, D), F32),
        compiler_params=_cparams("arbitrary", "arbitrary"),
        name="moe_ffn",
    )(tile_expert, n_used, src, x, w_gu, w_gu, w_down)


def _moe_combine_kernel(d1_ref, d2_ref, x_ref, gate_ref, g_ref, b_ref, y_hbm, o_ref, yrows_ref, sem, *, alpha):
    tm = yrows_ref.shape[2]
    i = pl.program_id(0)
    slot = lax.rem(i, 2)

    def start_gather(tile, slot):
        def body(r8, carry):
            for u in range(ROW_COPY_UNROLL):
                r = r8 * ROW_COPY_UNROLL + u
                _row_copy(y_hbm, d1_ref[tile * tm + r], yrows_ref.at[slot, 0], r, sem.at[slot]).start()
                _row_copy(y_hbm, d2_ref[tile * tm + r], yrows_ref.at[slot, 1], r, sem.at[slot]).start()
            return carry
        lax.fori_loop(0, tm // ROW_COPY_UNROLL, body, 0)

    @pl.when(i == 0)
    def _():
        start_gather(0, 0)

    for k in range(2):
        pltpu.make_async_copy(y_hbm.at[pl.ds(0, tm), :], yrows_ref.at[slot, k], sem.at[slot]).wait()

    @pl.when(i + 1 < pl.num_programs(0))
    def _():
        start_gather(i + 1, 1 - slot)

    y = yrows_ref[slot, 0] * gate_ref[:, 0:1] + yrows_ref[slot, 1] * gate_ref[:, 1:2]
    o_ref[...] = _layer_norm(alpha * x_ref[...] + y, g_ref[...], b_ref[...])


def _moe_combine(x, y_rows, dest, top_g, g, b, alpha, tm):
    T, D = x.shape
    row = pl.BlockSpec((tm, D), lambda i, d1, d2: (i, 0))
    grid_spec = pltpu.PrefetchScalarGridSpec(
        num_scalar_prefetch=2,
        grid=(T // tm,),
        in_specs=[row, pl.BlockSpec((tm, 2), lambda i, d1, d2: (i, 0)),
                  pl.BlockSpec(g.shape, lambda i, d1, d2: (0, 0)), pl.BlockSpec(b.shape, lambda i, d1, d2: (0, 0)),
                  pl.BlockSpec(memory_space=pl.ANY)],
        out_specs=row,
        scratch_shapes=[pltpu.VMEM((2, 2, tm, D), F32), pltpu.SemaphoreType.DMA((2,))],
    )
    return pl.pallas_call(
        functools.partial(_moe_combine_kernel, alpha=alpha),
        grid_spec=grid_spec,
        out_shape=jax.ShapeDtypeStruct((T, D), F32),
        compiler_params=_cparams("arbitrary"),
        name="moe_combine",
    )(dest[:, 0], dest[:, 1], x, top_g, g, b, y_rows)


def _moe_kernel(x_ref, gate_ref, wg_ref, wu_ref, wd_ref, g_ref, b_ref, o_ref, acc_ref, *, alpha):
    e = pl.program_id(1)
    f = pl.program_id(2)
    x = x_ref[...]
    xb = x.astype(BF16)

    @pl.when(jnp.logical_and(e == 0, f == 0))
    def _():
        acc_ref[...] = alpha * x

    gates = gate_ref[...]
    lane = lax.broadcasted_iota(jnp.int32, gates.shape, 1)
    ge = jnp.sum(jnp.where(lane == e, gates, 0.0), axis=1, keepdims=True)
    gate = jnp.dot(xb, wg_ref[0], preferred_element_type=F32)
    up = jnp.dot(xb, wu_ref[0], preferred_element_type=F32)
    acc_ref[...] += jnp.dot((_silu(gate) * up).astype(BF16), wd_ref[0], preferred_element_type=F32) * ge

    @pl.when(jnp.logical_and(e == pl.num_programs(1) - 1, f == pl.num_programs(2) - 1))
    def _():
        o_ref[...] = _layer_norm(acc_ref[...], g_ref[...], b_ref[...])


def _moe(x, gates, w_gu, w_down, g, b, alpha, tm, tf):
    T, D = x.shape
    E, F, _ = w_down.shape
    nf = F // tf
    row = pl.BlockSpec((tm, D), lambda i, e, f: (i, 0))
    return pl.pallas_call(
        functools.partial(_moe_kernel, alpha=alpha),
        grid=(T // tm, E, nf),
        in_specs=[row, pl.BlockSpec((tm, E), lambda i, e, f: (i, 0)),
                  pl.BlockSpec((1, D, tf), lambda i, e, f: (e, 0, f)),
                  pl.BlockSpec((1, D, tf), lambda i, e, f: (e, 0, nf + f)),
                  pl.BlockSpec((1, tf, D), lambda i, e, f: (e, f, 0)), _full(g.shape), _full(b.shape)],
        out_specs=row,
        out_shape=jax.ShapeDtypeStruct((T, D), F32),
        scratch_shapes=[pltpu.VMEM((tm, D), F32)],
        compiler_params=_cparams("parallel", "arbitrary", "arbitrary"),
        name="moe",
    )(x, gates, w_gu, w_gu, w_down, g, b)


def _rope_tables(pos):
    half = ATT_QK_DIM // 2
    inv = ROPE_THETA ** (-jnp.arange(0, ATT_QK_DIM, 2, dtype=F32) / ATT_QK_DIM)
    ang = pos.astype(F32)[:, None] * inv[None, :]
    cos, sin = jnp.cos(ang), jnp.sin(ang)
    return jnp.concatenate([cos] * 4, axis=1), jnp.concatenate([-sin, sin, -sin, sin], axis=1)


def _row_tile(n, target):
    t = min(n, target)
    while n % t:
        t //= 2
    return t


def kernel(x_prompt, x_sample, cache_attn_k, cache_attn_v, cache_mem_k, cache_mem_v, state_conv, state_ssm,
           page_table, mem_prompt, ln_g, ln_b, w_in, conv_w, conv_b, dt_bias, a_log, d_skip, ssm_norm_w,
           lambda_params, attn_norm_w, w_out, w_cq, w_ckv, w_co, w_ffn_gu, w_ffn_down, w_router, w_exp_gu,
           w_exp_down):
    B, S, D = x_prompt.shape
    Bs = x_sample.shape[0]
    depth = w_in.shape[0]
    n_phys, page = cache_attn_k.shape[1], cache_attn_k.shape[2]
    n_pages = page_table.shape[1]
    past_len = n_pages * page
    n_mem = mem_prompt.shape[1]
    alpha = (2 * depth) ** 0.25
    assert S % SSD_CHUNK == 0 and x_sample.shape[1] == 1

    tm_p = _row_tile(S, 512)
    tm_s = _row_tile(Bs, 128)
    cos_p, sin_p = _rope_tables(jnp.arange(S, dtype=jnp.int32))
    cos_s, sin_s = _rope_tables(jnp.full((tm_s,), past_len, jnp.int32))

    hp = x_prompt.reshape(B * S, D)
    hs = x_sample.reshape(Bs, D)
    mem = mem_prompt.reshape(B * n_mem, D)
    cache_k = cache_attn_k.transpose(0, 1, 3, 4, 2)
    cache_v = cache_attn_v.reshape(depth, n_phys, page * ATT_HEADS, ATT_V_DIM)
    dh = D // MEM_HEADS
    mem_rows = lambda t: (t.reshape(depth, Bs, n_mem, MEM_HEADS, dh // LANES, LANES).swapaxes(3, 4)
                          .reshape(depth, Bs, n_mem * dh // LANES * MEM_HEADS, LANES))
    cmem_k, cmem_v = mem_rows(cache_mem_k), mem_rows(cache_mem_v)
    cstate = state_conv.reshape(depth, Bs, (CONV_WIDTH - 1) * CONV_DIM)
    pad_heads = lambda t: jnp.pad(t, [(0, 0)] * (t.ndim - 2) + [(0, 2 * ATT_HEADS - t.shape[-2]), (0, 0)])

    splits = (ATT_COLS, 2 * ATT_COLS, 3 * ATT_COLS, 3 * ATT_COLS + SSM_INNER, 3 * ATT_COLS + SSM_INNER + CONV_DIM)
    outs = {n: [] for n in ("cp", "sp", "ks", "vs", "cs")}
    kv_prompt = mem_kv_prompt = ssm_sample = None
    for l in range(depth):
        lam_init = 0.8 - 0.6 * math.exp(-0.3 * l)
        wl = w_in[l].astype(BF16)
        wq, wk, wv, wz, wxbc, wdt = (wl[:, a:b] for a, b in zip((0,) + splits, splits + (wl.shape[1],)))
        w_proj = (wq, wk, wv, wz, wxbc, wdt, wdt.T)
        sw = (conv_w[l], conv_b[l][None, :], dt_bias[l][None, :], dt_bias[l][:, None], a_log[l][None, :],
              a_log[l][:, None], d_skip[l][None, :], ssm_norm_w[l][None, :])
        lp = lambda_params[l]
        nw = attn_norm_w[l][None, :]
        wo = w_out[l].astype(BF16)
        wo_att, wo_ssm = wo[:ATT_COLS], wo[ATT_COLS:]
        wcq = w_cq[l].astype(BF16)
        wckv = w_ckv[l].astype(BF16)
        wco = w_co[l].astype(BF16)
        g0, g1, g2 = (ln_g[l, i][None, :] for i in range(3))
        b0, b1, b2 = (ln_b[l, i][None, :] for i in range(3))

        q, k_all, kb, v_all, vb, z, xbc, dt, dtT = _in_proj(hp, cos_p, sin_p, w_proj, tm_p, S // tm_p,
                                                            cache=(kv_prompt,))
        kv_prompt = (k_all, v_all)
        att = _attention(q, kb, vb, lp, nw, lam_init, B, S)
        y, conv_new, h_last = _ssd(xbc, z, dt, dtT, sw, B, S)
        hp = _proj_ln((att, y), (wo_att, wo_ssm), hp, g0, b0, alpha, tm_p)
        outs["cp"].append(conv_new)
        outs["sp"].append(h_last.reshape(B, SSM_HEADS, SSM_HEAD_DIM, SSM_STATE))
        mk_all, mv_all, mk_b, mv_b = _mem_kv(mem, wckv, mem_kv_prompt, _row_tile(B * n_mem, 256))
        mem_kv_prompt = (mk_all, mv_all)
        hp = _cross_attention(hp, wcq, mk_b, mv_b, wco, g1, b1, alpha, B, S, n_mem, tm_p)

        q, kf, _, vf, _, z, xbc, dt, _ = _in_proj(hs, cos_s, sin_s, w_proj, tm_s, 1)
        att = _decode_attention(page_table, l, q.astype(F32).reshape(Bs, 1, ATT_COLS), kf.reshape(Bs, 1, ATT_COLS),
                                pad_heads(vf.reshape(Bs, ATT_HEADS, ATT_V_DIM)), cache_k, cache_v, lp, nw, lam_init)
        att = att[:, :ATT_HEADS].reshape(Bs, ATT_COLS)
        y, conv_new, ssm_sample = _ssd_step(xbc, z, dt, cstate, state_ssm, l, sw, ssm_sample)
        hs = _proj_ln((att, y), (wo_att, wo_ssm), hs, g0, b0, alpha, tm_s)
        outs["ks"].append(kf.reshape(Bs, 1, 2 * ATT_HEADS, ATT_QK_DIM))
        outs["vs"].append(vf.reshape(Bs, 1, ATT_HEADS, ATT_V_DIM))
        outs["cs"].append(conv_new.reshape(Bs, CONV_WIDTH - 1, CONV_DIM))
        (qc,) = _matmul(hs, wcq, (F32,), tm_s, D)
        q_rows = pad_heads(qc.reshape(Bs, MEM_HEADS, dh // LANES, LANES).swapaxes(1, 2))
        oc = _cross_decode(q_rows.reshape(Bs, -1, LANES), cmem_k, cmem_v, l)
        oc = oc.reshape(Bs, dh // LANES, 2 * ATT_HEADS, LANES)[:, :, :MEM_HEADS].swapaxes(1, 2).reshape(Bs, D)
        hs = _proj_ln((oc,), (wco,), hs, g1, b1, alpha, tm_s)

        if l % 2 == 0:
            wgu = w_ffn_gu[l // 2].astype(BF16)
            wd = w_ffn_down[l // 2].astype(BF16)
            tf = wd.shape[0] // 2
            hp = _ffn(hp, wgu, wd, g2, b2, alpha, tm_p, tf)
            hs = _ffn(hs, wgu, wd, g2, b2, alpha, tm_s, tf)
        else:
            wr = w_router[l // 2].astype(BF16)
            wgu = w_exp_gu[l // 2].astype(BF16)
            wd = w_exp_down[l // 2].astype(BF16)
            tf = wd.shape[1] // MOE_F_STEPS
            _, top_i, top_g = _router(hp, wr, tm_p)
            tm_e = _row_tile(2 * B * S, MOE_ROW_TILE)
            src, dest, tile_expert, n_used = _moe_plan(top_i, tm_e, 2 * B * S // tm_e + N_EXPERTS)
            y_rows = _moe_ffn(hp, src, tile_expert, n_used, wgu, wd, tm_e, tf)
            hp = _moe_combine(hp, y_rows, dest, top_g, g2, b2, alpha, _row_tile(B * S, 256))
            hs = _moe(hs, _router(hs, wr, tm_s)[0], wgu, wd, g2, b2, alpha, tm_s, tf)

    st = {n: jnp.stack(v) for n, v in outs.items()}
    k_prompt = kv_prompt[0].reshape(depth, B, 2 * ATT_HEADS, ATT_QK_DIM, S).transpose(0, 1, 4, 2, 3)
    v_prompt = kv_prompt[1].reshape(depth, B, S, ATT_HEADS, ATT_V_DIM)
    mem_shape = (depth, B, n_mem, dh // LANES, MEM_HEADS, LANES)
    mem_k_prompt, mem_v_prompt = (t.reshape(mem_shape).swapaxes(3, 4).reshape(depth, B, n_mem, MEM_HEADS, dh)
                                  for t in mem_kv_prompt)
    return (hp.reshape(B, S, D), hs.reshape(Bs, 1, D), k_prompt, v_prompt, mem_k_prompt, mem_v_prompt, st["cp"],
            st["sp"], st["ks"], st["vs"], st["cs"], ssm_sample)
```

```python
import functools
import math

import jax
import jax.numpy as jnp
from jax import lax
from jax.experimental import pallas as pl
from jax.experimental.pallas import tpu as pltpu

F32 = jnp.float32
BF16 = jnp.bfloat16

ATT_HEADS = 4
ATT_V_DIM = 128
ATT_QK_DIM = 64
ATT_COLS = 512
SSM_INNER = 512
SSM_HEAD_DIM = 64
SSM_HEADS = 8
SSM_GROUPS = 2
SSM_STATE = 128
CONV_WIDTH = 4
CONV_DIM = 1024
SSD_CHUNK = 128
MEM_HEADS = 4
N_EXPERTS = 8
ROPE_THETA = 10000.0
LN_EPS = 1e-5
NORM_EPS = 1e-5
QK_SCALE = ATT_QK_DIM ** -0.5

LANES = 128
MOE_ROW_TILE = 512
ROW_COPY_UNROLL = 8
MOE_F_STEPS = 1
ATT_HEADS_PER_STEP = 2
VMEM_LIMIT_BYTES = 56 * 1024 * 1024
NT_DIMS = (((1,), (1,)), ((), ()))


def _cparams(*sem):
    return pltpu.CompilerParams(dimension_semantics=sem, vmem_limit_bytes=VMEM_LIMIT_BYTES)


def _silu(x):
    return x / (1.0 + jnp.exp(-x))


def _softplus(x):
    return jnp.maximum(x, 0.0) + jnp.log1p(jnp.exp(-jnp.abs(x)))


def _layer_norm(y, g, b):
    mu = jnp.mean(y, axis=-1, keepdims=True)
    yc = y - mu
    var = jnp.mean(yc * yc, axis=-1, keepdims=True)
    return yc * lax.rsqrt(var + LN_EPS) * g + b


def _full(shape):
    return pl.BlockSpec(shape, lambda *_: (0,) * len(shape))


def _in_proj_kernel(x_ref, cos_ref, sin_ref, wq_ref, wk_ref, wv_ref, wz_ref, wxbc_ref, wdt_ref, wdtT_ref, *rest,
                    cache_layout):
    q_ref, kf_ref, kb_ref, vf_ref, vb_ref, z_ref, xbc_ref, dt_ref, dtT_ref = rest[-9:]
    tm = x_ref.shape[0]
    xb = x_ref[...].astype(BF16)
    cos = jnp.concatenate([cos_ref[...]] * 4, axis=1)
    sin = jnp.concatenate([sin_ref[...]] * 4, axis=1)
    lane = lax.broadcasted_iota(jnp.int32, (tm, ATT_COLS), 1)
    first_half = (lane & (ATT_QK_DIM - 1)) < (ATT_QK_DIM // 2)

    def rope(t):
        partner = jnp.where(first_half, pltpu.roll(t, ATT_COLS - ATT_QK_DIM // 2, 1),
                            pltpu.roll(t, ATT_QK_DIM // 2, 1))
        return t * cos + partner * sin

    q = rope(jnp.dot(xb, wq_ref[...], preferred_element_type=F32))
    q_ref[...] = (q * QK_SCALE).astype(BF16)
    k = rope(jnp.dot(xb, wk_ref[...], preferred_element_type=F32))
    kb_ref[...] = k.astype(BF16)
    v = jnp.dot(xb, wv_ref[...], preferred_element_type=F32)
    vb_ref[...] = v.astype(BF16)
    if cache_layout:
        n_prev = kf_ref.shape[0] - 1
        if n_prev:
            kf_ref[0:n_prev] = rest[0][...]
            vf_ref[0:n_prev] = rest[1][...]
        kf_ref[n_prev] = k.T
        for h in range(ATT_HEADS):
            vf_ref[n_prev, pl.ds(h, tm, stride=ATT_HEADS), :] = v[:, h * ATT_V_DIM:(h + 1) * ATT_V_DIM]
    else:
        kf_ref[...] = k
        vf_ref[...] = v
    z_ref[...] = jnp.dot(xb, wz_ref[...], preferred_element_type=F32)
    xbc_ref[...] = jnp.dot(xb, wxbc_ref[...], preferred_element_type=F32)
    dt_ref[...] = jnp.dot(xb, wdt_ref[...], preferred_element_type=F32)
    dtT_ref[...] = lax.dot_general(wdtT_ref[...], xb, NT_DIMS, preferred_element_type=F32)


def _in_proj(x, cos_t, sin_t, w, tm, n_pos_blocks, cache=None):
    T, D = x.shape
    w_all, wdt, wdtT = w
    cols = lambda n, j: pl.BlockSpec((D, n), lambda i: (0, j))
    row = lambda n: pl.BlockSpec((tm, n), lambda i: (i, 0))
    pos = pl.BlockSpec((tm, 128), lambda i: (i % n_pos_blocks, 0))
    kv_shape = jax.ShapeDtypeStruct((T, ATT_COLS), F32)
    k_shape, v_shape, k_spec, v_spec, prev, prev_specs = kv_shape, kv_shape, row(ATT_COLS), row(ATT_COLS), (), []
    if cache is not None:
        (prev_kv,) = cache
        S = n_pos_blocks * tm
        k_block = lambda n: pl.BlockSpec((n, None, ATT_COLS, tm), lambda i: (0, i // n_pos_blocks, 0, i % n_pos_blocks))
        v_block = lambda n: pl.BlockSpec((n, ATT_HEADS * tm, ATT_V_DIM), lambda i: (0, i, 0))
        n_prev = 0
        if prev_kv is not None:
            prev = tuple(prev_kv)
            n_prev = prev[0].shape[0]
            prev_specs = [k_block(n_prev), v_block(n_prev)]
        k_shape = jax.ShapeDtypeStruct((n_prev + 1, T // S, ATT_COLS, S), F32)
        v_shape = jax.ShapeDtypeStruct((n_prev + 1, ATT_HEADS * T, ATT_V_DIM), F32)
        k_spec, v_spec = k_block(n_prev + 1), v_block(n_prev + 1)
    out_shape = (
        jax.ShapeDtypeStruct((T, ATT_COLS), BF16),
        k_shape,
        jax.ShapeDtypeStruct((T, ATT_COLS), BF16),
        v_shape,
        jax.ShapeDtypeStruct((T, ATT_COLS), BF16),
        jax.ShapeDtypeStruct((T, SSM_INNER), F32),
        jax.ShapeDtypeStruct((T, CONV_DIM), F32),
        jax.ShapeDtypeStruct((T, SSM_HEADS), F32),
        jax.ShapeDtypeStruct((SSM_HEADS, T), F32),
    )
    out_specs = (row(ATT_COLS), k_spec, row(ATT_COLS), v_spec, row(ATT_COLS),
                 row(SSM_INNER), row(CONV_DIM), row(SSM_HEADS),
                 pl.BlockSpec((SSM_HEADS, tm), lambda i: (0, i)))
    return pl.pallas_call(
        functools.partial(_in_proj_kernel, cache_layout=cache is not None),
        grid=(T // tm,),
        in_specs=[row(D), pos, pos, cols(ATT_COLS, 0), cols(ATT_COLS, 1), cols(ATT_COLS, 2), cols(SSM_INNER, 3),
                  cols(CONV_DIM, (3 * ATT_COLS + SSM_INNER) // CONV_DIM), _full(wdt.shape), _full(wdtT.shape)] + prev_specs,
        out_specs=out_specs,
        out_shape=out_shape,
        compiler_params=_cparams("parallel"),
        name="in_proj",
    )(x, cos_t, sin_t, w_all, w_all, w_all, w_all, w_all, wdt, wdtT, *prev)


def _gated_group_norm(y, z, norm_w):
    y = y * _silu(z)
    half = SSM_INNER // SSM_GROUPS
    y2 = y * y
    ms0 = jnp.mean(y2[:, :half], axis=-1, keepdims=True)
    ms1 = jnp.mean(y2[:, half:], axis=-1, keepdims=True)
    lane = lax.broadcasted_iota(jnp.int32, y.shape, 1)
    scale = jnp.where(lane < half, lax.rsqrt(ms0 + NORM_EPS), lax.rsqrt(ms1 + NORM_EPS))
    return y * scale * norm_w


def _ssd_kernel(xbc_ref, z_ref, dt_ref, dtT_ref, convw_ref, convb_ref, dtb_ref, dtbT_ref, alog_ref, alogT_ref,
                dskip_ref, normw_ref, y_ref, conv_out_ref, state_out_ref, ext_ref, h_ref):
    Q = SSD_CHUNK
    c = pl.program_id(1)
    last = pl.num_programs(1) - 1

    @pl.when(c == 0)
    def _():
        ext_ref[...] = jnp.zeros(ext_ref.shape, F32)
        h_ref[...] = jnp.zeros(h_ref.shape, F32)

    cur = xbc_ref[...]
    before = ext_ref[...]
    row8 = lax.broadcasted_iota(jnp.int32, (8, CONV_DIM), 0)
    conv = convb_ref[...] + cur * convw_ref[CONV_WIDTH - 1:CONV_WIDTH, :]
    for s in range(1, CONV_WIDTH):
        shifted = pltpu.roll(cur, s, 0)
        first = jnp.where(row8 < s, pltpu.roll(before, s, 0), shifted[0:8, :])
        shifted = jnp.concatenate([first, shifted[8:, :]], axis=0)
        conv = conv + shifted * convw_ref[CONV_WIDTH - 1 - s:CONV_WIDTH - s, :]
    ext_ref[...] = cur[Q - 8:Q, :]
    conv = _silu(conv)
    xs = conv[:, :SSM_INNER]
    bmat = conv[:, SSM_INNER:SSM_INNER + SSM_GROUPS * SSM_STATE]
    cmat = conv[:, SSM_INNER + SSM_GROUPS * SSM_STATE:]

    dt_col = _softplus(dt_ref[...] + dtb_ref[...])
    a_col = dt_col * (-jnp.exp(alog_ref[...]))
    a_row = _softplus(dtT_ref[...] + dtbT_ref[...]) * (-jnp.exp(alogT_ref[...]))
    ri = lax.broadcasted_iota(jnp.int32, (Q, Q), 0)
    ci = lax.broadcasted_iota(jnp.int32, (Q, Q), 1)
    causal = ci <= ri
    tril = causal.astype(F32)
    triu = (ri <= ci).astype(F32)
    acs_col = jnp.dot(tril, a_col, preferred_element_type=F32, precision=lax.Precision.HIGHEST)
    acs_row = jnp.dot(a_row, triu, preferred_element_type=F32, precision=lax.Precision.HIGHEST)

    lo = ci < SSM_HEAD_DIM
    top = ri < SSM_HEAD_DIM
    pairs = []
    b_groups = [bmat[:, g * SSM_STATE:(g + 1) * SSM_STATE].astype(BF16) for g in range(SSM_GROUPS)]
    c_groups = [cmat[:, g * SSM_STATE:(g + 1) * SSM_STATE].astype(BF16) for g in range(SSM_GROUPS)]
    cb_groups = [lax.dot_general(c_g, b_g, NT_DIMS, preferred_element_type=F32)
                 for c_g, b_g in zip(c_groups, b_groups)]
    for k in range(SSM_HEADS // 2):
        h0, h1 = 2 * k, 2 * k + 1
        g = k // (SSM_HEADS // 2 // SSM_GROUPS)
        b_g, c_g, cb = b_groups[g], c_groups[g], cb_groups[g]
        xs_pair = xs[:, 2 * SSM_HEAD_DIM * k:2 * SSM_HEAD_DIM * (k + 1)]
        dtc = jnp.where(lo, dt_col[:, h0:h0 + 1], dt_col[:, h1:h1 + 1])
        xdt = xs_pair * dtc
        acol = jnp.where(lo, acs_col[:, h0:h0 + 1], acs_col[:, h1:h1 + 1])
        alast0 = acs_row[h0:h0 + 1, Q - 1:Q]
        alast1 = acs_row[h1:h1 + 1, Q - 1:Q]
        alast = jnp.where(lo[0:1, :], alast0, alast1)
        xw = xdt * jnp.exp(alast - acol)
        y_pair = jnp.where(lo, dskip_ref[0:1, h0:h0 + 1], dskip_ref[0:1, h1:h1 + 1]) * xs_pair
        for h, sel in ((h0, lo), (h1, jnp.logical_not(lo))):
            seg = acs_col[:, h:h + 1] - acs_row[h:h + 1, :]
            decay = jnp.exp(jnp.where(causal, seg, -jnp.inf))
            m = (cb * decay).astype(BF16)
            y_pair = y_pair + jnp.dot(m, jnp.where(sel, xdt, 0.0).astype(BF16), preferred_element_type=F32)
        states = jnp.dot(xw.T.astype(BF16), b_g, preferred_element_type=F32)
        h_prev = h_ref[k]
        y_off = lax.dot_general(c_g, h_prev.astype(BF16), NT_DIMS, preferred_element_type=F32)
        y_pair = y_pair + y_off * jnp.exp(acol)
        dec = jnp.where(top[:, 0:1], jnp.exp(alast0), jnp.exp(alast1))
        h_ref[k] = h_prev * dec + states
        pairs.append(y_pair)
    y = jnp.concatenate(pairs, axis=1)
    y_ref[...] = _gated_group_norm(y, z_ref[...], normw_ref[...]).astype(BF16)

    @pl.when(c == last)
    def _():
        conv_out_ref[0] = xbc_ref[Q - (CONV_WIDTH - 1):Q, :]
        state_out_ref[0] = h_ref[...]


def _ssd(xbc, z, dt, dtT, sw, B, S):
    convw, convb, dtb, dtbT, alog, alogT, dskip, normw = sw
    nc = S // SSD_CHUNK
    Q = SSD_CHUNK
    rowb = lambda n: pl.BlockSpec((Q, n), lambda b, c: (b * nc + c, 0))
    npair = SSM_HEADS // 2
    return pl.pallas_call(
        _ssd_kernel,
        grid=(B, nc),
        in_specs=[rowb(CONV_DIM), rowb(SSM_INNER), rowb(SSM_HEADS),
                  pl.BlockSpec((SSM_HEADS, Q), lambda b, c: (0, b * nc + c)),
                  _full(convw.shape), _full(convb.shape), _full(dtb.shape), _full(dtbT.shape),
                  _full(alog.shape), _full(alogT.shape), _full(dskip.shape), _full(normw.shape)],
        out_specs=(rowb(SSM_INNER),
                   pl.BlockSpec((1, CONV_WIDTH - 1, CONV_DIM), lambda b, c: (b, 0, 0)),
                   pl.BlockSpec((1, npair, 2 * SSM_HEAD_DIM, SSM_STATE), lambda b, c: (b, 0, 0, 0))),
        out_shape=(jax.ShapeDtypeStruct((B * S, SSM_INNER), BF16),
                   jax.ShapeDtypeStruct((B, CONV_WIDTH - 1, CONV_DIM), F32),
                   jax.ShapeDtypeStruct((B, npair, 2 * SSM_HEAD_DIM, SSM_STATE), F32)),
        scratch_shapes=[pltpu.VMEM((8, CONV_DIM), F32),
                        pltpu.VMEM((npair, 2 * SSM_HEAD_DIM, SSM_STATE), F32)],
        compiler_params=_cparams("parallel", "arbitrary"),
        name="ssd_scan",
    )(xbc, z, dt, dtT, convw, convb, dtb, dtbT, alog, alogT, dskip, normw)


def _ssd_step_kernel(xbc_ref, z_ref, dt_ref, cstate_ref, hstate_ref, convw_ref, convb_ref, dtb_ref, alog_ref,
                     dskip_ref, normw_ref, *rest):
    y_ref, cout_ref, hout_ref = rest[-3:]
    n_prev = hout_ref.shape[0] - 1
    if n_prev:
        hout_ref[0:n_prev] = rest[0][...]
    bt = xbc_ref.shape[0]
    P, N = SSM_HEAD_DIM, SSM_STATE
    new = xbc_ref[...]
    conv = convb_ref[...] + new * convw_ref[CONV_WIDTH - 1:CONV_WIDTH, :]
    for i in range(CONV_WIDTH - 1):
        conv = conv + cstate_ref[:, i * CONV_DIM:(i + 1) * CONV_DIM] * convw_ref[i:i + 1, :]
    cout_ref[:, 0:(CONV_WIDTH - 2) * CONV_DIM] = cstate_ref[:, CONV_DIM:(CONV_WIDTH - 1) * CONV_DIM]
    cout_ref[:, (CONV_WIDTH - 2) * CONV_DIM:] = new
    conv = _silu(conv)
    xs = conv[:, :SSM_INNER]
    bmat = conv[:, SSM_INNER:SSM_INNER + SSM_GROUPS * N]
    cmat = conv[:, SSM_INNER + SSM_GROUPS * N:]
    dt = _softplus(dt_ref[...] + dtb_ref[...])
    dec = jnp.exp(dt * (-jnp.exp(alog_ref[...])))
    H = SSM_HEADS
    HP = H * P
    exact = lax.Precision.HIGHEST
    eye_n = (lax.broadcasted_iota(jnp.int32, (N, N), 0) == lax.broadcasted_iota(jnp.int32, (N, N), 1)).astype(F32)
    x_t = jnp.concatenate(
        [lax.dot_general(eye_n, xs[:, c * N:(c + 1) * N], NT_DIMS, preferred_element_type=F32, precision=exact)
         for c in range(HP // N)], axis=0)
    rep = (lax.broadcasted_iota(jnp.int32, (HP, H), 0) // P == lax.broadcasted_iota(jnp.int32, (HP, H), 1)).astype(F32)
    per_head = jnp.concatenate([dt, dec, jnp.broadcast_to(dskip_ref[...], (8, H))], axis=0)
    cols = lax.dot_general(rep, per_head, NT_DIMS, preferred_element_type=F32, precision=exact)
    d_col = cols[:, 2 * bt:2 * bt + 1]
    first_group = lax.broadcasted_iota(jnp.int32, (HP, 1), 0) < HP // SSM_GROUPS
    lane_b = lax.broadcasted_iota(jnp.int32, (HP, bt), 1)
    y_t = jnp.zeros((HP, bt), F32)
    for b in range(bt):
        x_col = x_t[:, b:b + 1]
        b_sel = jnp.where(first_group, bmat[b:b + 1, 0:N], bmat[b:b + 1, N:2 * N])
        h_new = (hstate_ref[b].reshape(HP, N) * cols[:, bt + b:bt + b + 1]
                 + (x_col * cols[:, b:b + 1]) * b_sel)
        hout_ref[n_prev, b] = h_new.reshape(H, P, N)
        c2 = jnp.concatenate([cmat[b:b + 1, 0:N], cmat[b:b + 1, N:2 * N], jnp.zeros((6, N), F32)], axis=0)
        yb = lax.dot_general(h_new.astype(BF16), c2.astype(BF16), NT_DIMS, preferred_element_type=F32)
        y_col = jnp.where(first_group, yb[:, 0:1], yb[:, 1:2]) + d_col * x_col
        y_t = jnp.where(lane_b == b, y_col, y_t)
    eye_b = (lax.broadcasted_iota(jnp.int32, (bt, bt), 0) == lax.broadcasted_iota(jnp.int32, (bt, bt), 1)).astype(F32)
    y = lax.dot_general(eye_b, y_t, NT_DIMS, preferred_element_type=F32, precision=exact)
    y_ref[...] = _gated_group_norm(y, z_ref[...], normw_ref[...]).astype(BF16)


def _ssd_step(xbc, z, dt, cstate, hstate, layer, sw, h_prev=None, bt=8):
    convw, convb, dtb, _, alog, _, dskip, normw = sw
    Bs = xbc.shape[0]
    cw = (CONV_WIDTH - 1) * CONV_DIM
    rowb = lambda n: pl.BlockSpec((bt, n), lambda i: (i, 0))
    hshape = (bt, SSM_HEADS, SSM_HEAD_DIM, SSM_STATE)
    layers = lambda n: pl.BlockSpec((n,) + hshape, lambda i: (0, i, 0, 0, 0))
    prev = () if h_prev is None else (h_prev,)
    n_prev = 0 if h_prev is None else h_prev.shape[0]
    return pl.pallas_call(
        _ssd_step_kernel,
        grid=(Bs // bt,),
        in_specs=[rowb(CONV_DIM), rowb(SSM_INNER), rowb(SSM_HEADS),
                  pl.BlockSpec((None, bt, cw), lambda i: (layer, i, 0)),
                  pl.BlockSpec((None,) + hshape, lambda i: (layer, i, 0, 0, 0)),
                  _full(convw.shape), _full(convb.shape), _full(dtb.shape), _full(alog.shape),
                  _full(dskip.shape), _full(normw.shape)] + [layers(n_prev)] * len(prev),
        out_specs=(rowb(SSM_INNER), rowb(cw), layers(n_prev + 1)),
        out_shape=(jax.ShapeDtypeStruct((Bs, SSM_INNER), BF16),
                   jax.ShapeDtypeStruct((Bs, cw), F32),
                   jax.ShapeDtypeStruct((n_prev + 1,) + hstate.shape[1:], F32)),
        compiler_params=_cparams("parallel"),
        name="ssd_step",
    )(xbc, z, dt, cstate, hstate, convw, convb, dtb, alog, dskip, normw, *prev)


def _lambda_value(lp, lam_init):
    d1 = jnp.sum(lp[0:1, :] * lp[1:2, :], axis=1, keepdims=True)
    d2 = jnp.sum(lp[2:3, :] * lp[3:4, :], axis=1, keepdims=True)
    return jnp.exp(d1) - jnp.exp(d2) + lam_init


def _head_norm(o, nw, lam_init):
    return o * lax.rsqrt(jnp.mean(o * o, axis=-1, keepdims=True) + NORM_EPS) * nw * (1.0 - lam_init)


def _attn_kernel(lp_ref, nw_ref, q_ref, k_ref, v_ref, o_ref, m_ref, l_ref, acc_ref, *, lam_init):
    tq = q_ref.shape[0]
    n_heads = q_ref.shape[1] // ATT_V_DIM
    qi = pl.program_id(2)
    lane = lax.broadcasted_iota(jnp.int32, (tq, ATT_V_DIM), 1)
    q_maps = []
    for h in range(n_heads):
        q = q_ref[:, h * ATT_V_DIM:(h + 1) * ATT_V_DIM]
        zero = jnp.zeros_like(q)
        q_maps += [jnp.where(lane < ATT_QK_DIM, q, zero), jnp.where(lane >= ATT_QK_DIM, q, zero)]
    m_ref[...] = jnp.full(m_ref.shape, -jnp.inf, F32)
    l_ref[...] = jnp.zeros(l_ref.shape, F32)
    acc_ref[...] = jnp.zeros(acc_ref.shape, F32)
    below_diag = (lax.broadcasted_iota(jnp.int32, (tq, tq), 1) <= lax.broadcasted_iota(jnp.int32, (tq, tq), 0))

    def block(j, diagonal):
        start = pl.multiple_of(j * tq, tq)
        for i in range(2 * n_heads):
            cols = slice((i // 2) * ATT_V_DIM, (i // 2 + 1) * ATT_V_DIM)
            k = k_ref[pl.ds(start, tq), cols]
            v = v_ref[pl.ds(start, tq), cols]
            s = lax.dot_general(q_maps[i], k, NT_DIMS, preferred_element_type=F32)
            if diagonal:
                s = jnp.where(below_diag, s, -jnp.inf)
            chunks = [s[:, c * LANES:(c + 1) * LANES] for c in range(tq // LANES)]
            top = functools.reduce(jnp.maximum, chunks)
            m_old = m_ref[i]
            m_new = jnp.maximum(m_old, jnp.max(top, axis=1, keepdims=True))
            alpha = jnp.exp(m_old - m_new)
            ps = [jnp.exp(c - m_new) for c in chunks]
            l_ref[i] = alpha * l_ref[i] + functools.reduce(jnp.add, ps)
            p = jnp.concatenate(ps, axis=1).astype(BF16)
            acc_ref[i] = alpha * acc_ref[i] + jnp.dot(p, v, preferred_element_type=F32)
            m_ref[i] = m_new

    def full_block(j, carry):
        block(j, False)
        return carry

    lax.fori_loop(0, qi, full_block, 0)
    block(qi, True)
    lam = _lambda_value(lp_ref[...], lam_init)
    for h in range(n_heads):
        l1 = jnp.sum(l_ref[2 * h], axis=1, keepdims=True)
        l2 = jnp.sum(l_ref[2 * h + 1], axis=1, keepdims=True)
        o = acc_ref[2 * h] / l1 - lam * (acc_ref[2 * h + 1] / l2)
        o_ref[:, h * ATT_V_DIM:(h + 1) * ATT_V_DIM] = _head_norm(o, nw_ref[...], lam_init).astype(BF16)


def _attention(q, k, v, lp, nw, lam_init, B, S):
    tq = min(512, S)
    nq = S // tq
    hg = ATT_HEADS_PER_STEP
    qspec = pl.BlockSpec((tq, hg * ATT_V_DIM), lambda b, h, i: (b * nq + i, h))
    kspec = pl.BlockSpec((S, hg * ATT_V_DIM), lambda b, h, i: (b, h))
    return pl.pallas_call(
        functools.partial(_attn_kernel, lam_init=lam_init),
        grid=(B, ATT_HEADS // hg, nq),
        in_specs=[_full(lp.shape), _full(nw.shape), qspec, kspec, kspec],
        out_specs=qspec,
        out_shape=jax.ShapeDtypeStruct((B * S, ATT_COLS), BF16),
        scratch_shapes=[pltpu.VMEM((2 * hg, tq, LANES), F32), pltpu.VMEM((2 * hg, tq, LANES), F32),
                        pltpu.VMEM((2 * hg, tq, ATT_V_DIM), F32)],
        compiler_params=_cparams("parallel", "parallel", "arbitrary"),
        name="diff_attention",
    )(lp, nw, q, k, v)


def _decode_attn_kernel(pt_ref, lp_ref, nw_ref, q_ref, kn_ref, vn_ref, *refs, n_pages, lam_init):
    k_refs = refs[:n_pages]
    v_refs = refs[n_pages:2 * n_pages]
    o_ref = refs[2 * n_pages]
    R = 2 * ATT_HEADS
    page = k_refs[0].shape[2]
    r = lax.broadcasted_iota(jnp.int32, (R, ATT_COLS), 0)
    grp = lax.broadcasted_iota(jnp.int32, (R, ATT_COLS), 1) // ATT_QK_DIM
    target = jnp.where(r < ATT_HEADS, 2 * r, 2 * (r - ATT_HEADS) + 1)
    qf = jnp.where(grp == target, q_ref[0].astype(BF16).astype(F32), 0.0)
    qt = qf.astype(BF16)
    s = [jnp.dot(qt, k_ref[...].reshape(ATT_COLS, page).astype(BF16), preferred_element_type=F32)
         for k_ref in k_refs]
    s_new = jnp.sum(qf * kn_ref[0].astype(BF16).astype(F32), axis=1, keepdims=True)
    m = s_new
    for sj in s:
        m = jnp.maximum(m, jnp.max(sj, axis=1, keepdims=True))
    p = [jnp.exp(sj - m) for sj in s]
    p_new = jnp.exp(s_new - m)
    l = p_new
    for pj in p:
        l = l + jnp.sum(pj, axis=1, keepdims=True)
    inv_l = 1.0 / l
    lam = _lambda_value(lp_ref[...], lam_init)
    diff = lambda t: t - lam * pltpu.roll(t, ATT_HEADS, 0)
    a = [diff(pj * inv_l).astype(BF16) for pj in p]
    a_new = diff(jnp.broadcast_to(p_new * inv_l, (R, 128)))[:, 0:1].astype(BF16).astype(F32)
    row = lax.broadcasted_iota(jnp.int32, (R, ATT_V_DIM), 0)
    out = a_new * vn_ref[0].astype(BF16).astype(F32)
    for h in range(ATT_HEADS):
        acc = jnp.zeros((R, ATT_V_DIM), F32)
        for aj, v_ref in zip(a, v_refs):
            v_h = v_ref[pl.ds(h, page, stride=ATT_HEADS), :].astype(BF16)
            acc = acc + jnp.dot(aj, v_h, preferred_element_type=F32)
        out = out + jnp.where(row == h, acc, 0.0)
    o_ref[0] = _head_norm(out, nw_ref[...], lam_init)


def _decode_attention(pt, layer, q, k_new, v_new, cache_k, cache_v, lp, nw, lam_init):
    Bs, n_pages = pt.shape
    R = 2 * ATT_HEADS
    qspec = pl.BlockSpec((1, 1, ATT_COLS), lambda b, pt: (b, 0, 0))
    vspec = pl.BlockSpec((1, R, ATT_V_DIM), lambda b, pt: (b, 0, 0))
    kpages = [pl.BlockSpec((None, None) + cache_k.shape[2:], lambda b, pt, j=j: (layer, pt[b, j], 0, 0, 0))
              for j in range(n_pages)]
    vpages = [pl.BlockSpec((None, None) + cache_v.shape[2:], lambda b, pt, j=j: (layer, pt[b, j], 0, 0))
              for j in range(n_pages)]
    grid_spec = pltpu.PrefetchScalarGridSpec(
        num_scalar_prefetch=1,
        grid=(Bs,),
        in_specs=[pl.BlockSpec(lp.shape, lambda b, pt: (0, 0)), pl.BlockSpec(nw.shape, lambda b, pt: (0, 0)),
                  qspec, qspec, vspec] + kpages + vpages,
        out_specs=vspec,
    )
    return pl.pallas_call(
        functools.partial(_decode_attn_kernel, n_pages=n_pages, lam_init=lam_init),
        grid_spec=grid_spec,
        out_shape=jax.ShapeDtypeStruct((Bs, R, ATT_V_DIM), F32),
        compiler_params=_cparams("parallel"),
        name="decode_attention",
    )(pt, lp, nw, q, k_new, v_new, *([cache_k] * n_pages), *([cache_v] * n_pages))


def _proj_ln_kernel(*refs, n_in, alpha):
    a_refs = refs[:n_in]
    w_refs = refs[n_in:2 * n_in]
    x_ref, g_ref, b_ref, o_ref = refs[2 * n_in:]
    acc = alpha * x_ref[...]
    for a_ref, w_ref in zip(a_refs, w_refs):
        acc = acc + jnp.dot(a_ref[...].astype(BF16), w_ref[...], preferred_element_type=F32)
    o_ref[...] = _layer_norm(acc, g_ref[...], b_ref[...])


def _proj_ln(acts, ws, x, g, b, alpha, tm):
    T, D = x.shape
    row = lambda n: pl.BlockSpec((tm, n), lambda i: (i, 0))
    return pl.pallas_call(
        functools.partial(_proj_ln_kernel, n_in=len(acts), alpha=alpha),
        grid=(T // tm,),
        in_specs=[row(a.shape[1]) for a in acts] + [_full(w.shape) for w in ws] + [row(D), _full(g.shape), _full(b.shape)],
        out_specs=row(D),
        out_shape=jax.ShapeDtypeStruct((T, D), F32),
        compiler_params=_cparams("parallel"),
        name="proj_ln",
    )(*acts, *ws, x, g, b)


def _matmul_kernel(x_ref, w_ref, *o_refs):
    y = jnp.dot(x_ref[...].astype(BF16), w_ref[...], preferred_element_type=F32)
    for o_ref in o_refs:
        o_ref[...] = y.astype(o_ref.dtype)


def _matmul(x, w, out_dtypes, tm, tn):
    M, K = x.shape
    N = w.shape[1]
    ospec = pl.BlockSpec((tm, tn), lambda i, j: (i, j))
    return pl.pallas_call(
        _matmul_kernel,
        grid=(M // tm, N // tn),
        in_specs=[pl.BlockSpec((tm, K), lambda i, j: (i, 0)), pl.BlockSpec((K, tn), lambda i, j: (0, j))],
        out_specs=tuple(ospec for _ in out_dtypes),
        out_shape=tuple(jax.ShapeDtypeStruct((M, N), d) for d in out_dtypes),
        compiler_params=_cparams("parallel", "parallel"),
        name="matmul",
    )(x, w)


def _mem_kv_kernel(x_ref, w_ref, *rest):
    mk_ref, mv_ref, mkb_ref, mvb_ref = rest[-4:]
    tm = x_ref.shape[0]
    D = w_ref.shape[1] // 2
    dh = D // MEM_HEADS
    halves = dh // LANES
    y = jnp.dot(x_ref[...].astype(BF16), w_ref[...], preferred_element_type=F32)
    mkb_ref[...] = y[:, :D].astype(BF16)
    mvb_ref[...] = y[:, D:].astype(BF16)
    n_prev = mk_ref.shape[0] - 1
    if n_prev:
        mk_ref[0:n_prev] = rest[0][...]
        mv_ref[0:n_prev] = rest[1][...]
    for out_ref, base in ((mk_ref, 0), (mv_ref, D)):
        for h in range(MEM_HEADS):
            for c in range(halves):
                col = base + h * dh + c * LANES
                out_ref[n_prev, pl.ds(c * MEM_HEADS + h, tm, stride=halves * MEM_HEADS), :] = y[:, col:col + LANES]


def _mem_kv(mem, w, prev, tm):
    M, D = mem.shape
    rows_per_token = D // LANES
    rows = lambda n: pl.BlockSpec((n, tm * rows_per_token, LANES), lambda i: (0, i, 0))
    flat = pl.BlockSpec((tm, D), lambda i: (i, 0))
    prev = () if prev is None else tuple(prev)
    n_prev = prev[0].shape[0] if prev else 0
    rows_shape = jax.ShapeDtypeStruct((n_prev + 1, M * rows_per_token, LANES), F32)
    return pl.pallas_call(
        _mem_kv_kernel,
        grid=(M // tm,),
        in_specs=[flat, _full(w.shape)] + [rows(n_prev)] * len(prev),
        out_specs=(rows(n_prev + 1), rows(n_prev + 1), flat, flat),
        out_shape=(rows_shape, rows_shape, jax.ShapeDtypeStruct((M, D), BF16), jax.ShapeDtypeStruct((M, D), BF16)),
        compiler_params=_cparams("parallel"),
        name="mem_kv",
    )(mem, w, *prev)


def _softmax_rows(s):
    m = jnp.max(s, axis=1, keepdims=True)
    p = jnp.exp(s - m)
    return p / jnp.sum(p, axis=1, keepdims=True)


def _cross_kernel(x_ref, wq_ref, mk_ref, mv_ref, wo_ref, g_ref, b_ref, o_ref, *, alpha):
    x = x_ref[...]
    D = x.shape[1]
    dh = D // MEM_HEADS
    q = jnp.dot(x.astype(BF16), wq_ref[...], preferred_element_type=F32).astype(BF16)
    outs = []
    for h in range(MEM_HEADS):
        sl = slice(h * dh, (h + 1) * dh)
        s = lax.dot_general(q[:, sl], mk_ref[:, sl], NT_DIMS, preferred_element_type=F32) * (dh ** -0.5)
        outs.append(jnp.dot(_softmax_rows(s).astype(BF16), mv_ref[:, sl], preferred_element_type=F32).astype(BF16))
    o = jnp.concatenate(outs, axis=1)
    y = alpha * x + jnp.dot(o, wo_ref[...], preferred_element_type=F32)
    o_ref[...] = _layer_norm(y, g_ref[...], b_ref[...])


def _cross_attention(x, wq, mk, mv, wo, g, b, alpha, B, S, n_mem, tm):
    T, D = x.shape
    nt = S // tm
    row = pl.BlockSpec((tm, D), lambda bi, i: (bi * nt + i, 0))
    mem = pl.BlockSpec((n_mem, D), lambda bi, i: (bi, 0))
    return pl.pallas_call(
        functools.partial(_cross_kernel, alpha=alpha),
        grid=(B, nt),
        in_specs=[row, _full(wq.shape), mem, mem, _full(wo.shape), _full(g.shape), _full(b.shape)],
        out_specs=row,
        out_shape=jax.ShapeDtypeStruct((T, D), F32),
        compiler_params=_cparams("parallel", "parallel"),
        name="cross_attention",
    )(x, wq, mk, mv, wo, g, b)


def _cross_decode_kernel(q_ref, mk_ref, mv_ref, o_ref):
    R = q_ref.shape[1] // 2
    rows_kv = mk_ref.shape[1]
    dh = 2 * LANES
    own = (lax.broadcasted_iota(jnp.int32, (R, rows_kv), 1) & (R - 1)) == lax.broadcasted_iota(jnp.int32, (R, rows_kv), 0)
    for b in range(q_ref.shape[0]):
        part = lax.dot_general(q_ref[b].astype(BF16), mk_ref[b].astype(BF16), NT_DIMS,
                               preferred_element_type=F32)
        s = (part[:R] + pltpu.roll(part[R:], rows_kv - MEM_HEADS, 1)) * (dh ** -0.5)
        p = _softmax_rows(jnp.where(own, s, -jnp.inf))
        a = jnp.concatenate([p, pltpu.roll(p, MEM_HEADS, 1)], axis=0).astype(BF16)
        o_ref[b] = jnp.dot(a, mv_ref[b].astype(BF16), preferred_element_type=F32)


def _cross_decode(q, mem_k, mem_v, layer, bt=4):
    Bs, R2, _ = q.shape
    tok = pl.BlockSpec((bt, R2, LANES), lambda b: (b, 0, 0))
    mem = pl.BlockSpec((None, bt) + mem_k.shape[2:], lambda b: (layer, b, 0, 0))
    return pl.pallas_call(
        _cross_decode_kernel,
        grid=(Bs // bt,),
        in_specs=[tok, mem, mem],
        out_specs=tok,
        out_shape=jax.ShapeDtypeStruct((Bs, R2, LANES), F32),
        compiler_params=_cparams("parallel"),
        name="cross_decode",
    )(q, mem_k, mem_v)


def _ffn_kernel(x_ref, wg_ref, wu_ref, wd_ref, g_ref, b_ref, o_ref, acc_ref, *, alpha):
    f = pl.program_id(1)
    x = x_ref[...]
    xb = x.astype(BF16)

    @pl.when(f == 0)
    def _():
        acc_ref[...] = alpha * x

    gate = jnp.dot(xb, wg_ref[...], preferred_element_type=F32)
    up = jnp.dot(xb, wu_ref[...], preferred_element_type=F32)
    acc_ref[...] += jnp.dot((_silu(gate) * up).astype(BF16), wd_ref[...], preferred_element_type=F32)

    @pl.when(f == pl.num_programs(1) - 1)
    def _():
        o_ref[...] = _layer_norm(acc_ref[...], g_ref[...], b_ref[...])


def _ffn(x, w_gu, w_down, g, b, alpha, tm, tf):
    T, D = x.shape
    F = w_down.shape[0]
    nf = F // tf
    row = pl.BlockSpec((tm, D), lambda i, f: (i, 0))
    mode = dict(pipeline_mode=pl.Buffered(1)) if nf == 1 else {}
    return pl.pallas_call(
        functools.partial(_ffn_kernel, alpha=alpha),
        grid=(T // tm, nf),
        in_specs=[row, pl.BlockSpec((D, tf), lambda i, f: (0, f), **mode),
                  pl.BlockSpec((D, tf), lambda i, f: (0, nf + f), **mode),
                  pl.BlockSpec((tf, D), lambda i, f: (f, 0), **mode), _full(g.shape), _full(b.shape)],
        out_specs=row,
        out_shape=jax.ShapeDtypeStruct((T, D), F32),
        scratch_shapes=[pltpu.VMEM((tm, D), F32)],
        compiler_params=_cparams("parallel", "arbitrary"),
        name="ffn",
    )(x, w_gu, w_gu, w_down, g, b)


def _router_kernel(x_ref, wr_ref, gate_ref, top_i_ref, top_g_ref):
    logits = jnp.dot(x_ref[...].astype(BF16), wr_ref[...], preferred_element_type=F32)
    lane = lax.broadcasted_iota(jnp.int32, logits.shape, 1).astype(F32)
    none = float(N_EXPERTS)
    m1 = jnp.max(logits, axis=1, keepdims=True)
    i1 = jnp.min(jnp.where(logits == m1, lane, none), axis=1, keepdims=True)
    rest = jnp.where(lane == i1, -jnp.inf, logits)
    m2 = jnp.max(rest, axis=1, keepdims=True)
    i2 = jnp.min(jnp.where(rest == m2, lane, none), axis=1, keepdims=True)
    e = jnp.exp(m2 - m1)
    den = 1.0 + e
    gate_ref[...] = jnp.where(lane == i1, 1.0 / den, 0.0) + jnp.where(lane == i2, e / den, 0.0)
    first = lax.broadcasted_iota(jnp.int32, top_i_ref.shape, 1) == 0
    top_i_ref[...] = jnp.where(first, i1, i2).astype(jnp.int32)
    top_g_ref[...] = jnp.where(first, 1.0 / den, e / den)


def _router(x, wr, tm):
    T, D = x.shape
    top = pl.BlockSpec((tm, 2), lambda i: (i, 0))
    return pl.pallas_call(
        _router_kernel,
        grid=(T // tm,),
        in_specs=[pl.BlockSpec((tm, D), lambda i: (i, 0)), _full(wr.shape)],
        out_specs=(pl.BlockSpec((tm, N_EXPERTS), lambda i: (i, 0)), top, top),
        out_shape=(jax.ShapeDtypeStruct((T, N_EXPERTS), F32), jax.ShapeDtypeStruct((T, 2), jnp.int32),
                   jax.ShapeDtypeStruct((T, 2), F32)),
        compiler_params=_cparams("parallel"),
        name="router",
    )(x, wr)


def _moe_plan(top_i, tm_e, n_tiles):
    T = top_i.shape[0]
    chosen = (top_i[:, :, None] == jnp.arange(N_EXPERTS, dtype=jnp.int32)[None, None, :]).any(axis=1)
    chosen = chosen.astype(jnp.int32)
    count = chosen.sum(axis=0)
    rank = jnp.cumsum(chosen, axis=0) - chosen
    tiles = (count + tm_e - 1) // tm_e
    tile_end = jnp.cumsum(tiles)
    row_start = (tile_end - tiles) * tm_e
    dest = jnp.take_along_axis(row_start[None, :] + rank, top_i, axis=1).astype(jnp.int32)
    token = jnp.broadcast_to(jnp.arange(T, dtype=jnp.int32)[:, None], (T, 2))
    src = jnp.zeros((n_tiles * tm_e,), jnp.int32).at[dest.reshape(-1)].set(token.reshape(-1))
    n_used = tile_end[-1]
    tile = jnp.arange(n_tiles, dtype=jnp.int32)
    tile_expert = (tile_end[None, :] <= jnp.minimum(tile, n_used - 1)[:, None]).sum(axis=1).astype(jnp.int32)
    return src, dest, tile_expert, n_used.reshape(1).astype(jnp.int32)


def _row_copy(src_hbm, src_row, dst_ref, dst_row, sem):
    return pltpu.make_async_copy(src_hbm.at[pl.ds(src_row, 1), :], dst_ref.at[pl.ds(dst_row, 1), :], sem)


def _moe_ffn_kernel(te_ref, nv_ref, src_ref, x_hbm, wg_ref, wu_ref, wd_ref, y_ref, xrows_ref, xb_ref, sem):
    tm_e = xrows_ref.shape[1]
    i = pl.program_id(0)
    f = pl.program_id(1)
    n_used = nv_ref[0]
    slot = lax.rem(i, 2)

    def start_gather(tile, slot):
        def body(r8, carry):
            for u in range(ROW_COPY_UNROLL):
                r = r8 * ROW_COPY_UNROLL + u
                _row_copy(x_hbm, src_ref[tile * tm_e + r], xrows_ref.at[slot], r, sem.at[slot]).start()
            return carry
        lax.fori_loop(0, tm_e // ROW_COPY_UNROLL, body, 0)

    @pl.when(f == 0)
    def _():
        @pl.when(i == 0)
        def _():
            start_gather(0, 0)

        @pl.when(i < n_used)
        def _():
            pltpu.make_async_copy(x_hbm.at[pl.ds(0, tm_e), :], xrows_ref.at[slot], sem.at[slot]).wait()
            xb_ref[...] = xrows_ref[slot].astype(BF16)

        @pl.when(i + 1 < n_used)
        def _():
            start_gather(i + 1, 1 - slot)

    @pl.when(i < n_used)
    def _():
        xb = xb_ref[...]
        gate = jnp.dot(xb, wg_ref[...], preferred_element_type=F32)
        up = jnp.dot(xb, wu_ref[...], preferred_element_type=F32)
        part = jnp.dot((_silu(gate) * up).astype(BF16), wd_ref[...], preferred_element_type=F32)

        @pl.when(f == 0)
        def _():
            y_ref[...] = part

        @pl.when(f > 0)
        def _():
            y_ref[...] += part

    @pl.when(jnp.logical_and(i >= n_used, f == 0))
    def _():
        y_ref[...] = jnp.zeros(y_ref.shape, F32)


def _moe_ffn(x, src, tile_expert, n_used, w_gu, w_down, tm_e, tf):
    T, D = x.shape
    E, F, _ = w_down.shape
    nf = F // tf
    n_tiles = tile_expert.shape[0]
    grid_spec = pltpu.PrefetchScalarGridSpec(
        num_scalar_prefetch=3,
        grid=(n_tiles, nf),
        in_specs=[pl.BlockSpec(memory_space=pl.ANY),
                  pl.BlockSpec((None, D, tf), lambda i, f, te, nv, src: (te[i], 0, f), pipeline_mode=pl.Buffered(1)),
                  pl.BlockSpec((None, D, tf), lambda i, f, te, nv, src: (te[i], 0, nf + f), pipeline_mode=pl.Buffered(1)),
                  pl.BlockSpec((None, tf, D), lambda i, f, te, nv, src: (te[i], f, 0), pipeline_mode=pl.Buffered(1))],
        out_specs=pl.BlockSpec((tm_e, D), lambda i, f, te, nv, src: (i, 0)),
        scratch_shapes=[pltpu.VMEM((2, tm_e, D), F32), pltpu.VMEM((tm_e, D), BF16), pltpu.SemaphoreType.DMA((2,))],
    )
    return pl.pallas_call(
        _moe_ffn_kernel,
        grid_spec=grid_spec,
        out_shape=jax.ShapeDtypeStruct((n_tiles * tm_e, D), F32),
        compiler_params=_cparams("arbitrary", "arbitrary"),
        name="moe_ffn",
    )(tile_expert, n_used, src, x, w_gu, w_gu, w_down)


def _moe_combine_kernel(d1_ref, d2_ref, x_ref, gate_ref, g_ref, b_ref, y_hbm, o_ref, yrows_ref, sem, *, alpha):
    tm = yrows_ref.shape[2]
    i = pl.program_id(0)
    slot = lax.rem(i, 2)

    def start_gather(tile, slot):
        def body(r8, carry):
            for u in range(ROW_COPY_UNROLL):
                r = r8 * ROW_COPY_UNROLL + u
                _row_copy(y_hbm, d1_ref[tile * tm + r], yrows_ref.at[slot, 0], r, sem.at[slot]).start()
                _row_copy(y_hbm, d2_ref[tile * tm + r], yrows_ref.at[slot, 1], r, sem.at[slot]).start()
            return carry
        lax.fori_loop(0, tm // ROW_COPY_UNROLL, body, 0)

    @pl.when(i == 0)
    def _():
        start_gather(0, 0)

    for k in range(2):
        pltpu.make_async_copy(y_hbm.at[pl.ds(0, tm), :], yrows_ref.at[slot, k], sem.at[slot]).wait()

    @pl.when(i + 1 < pl.num_programs(0))
    def _():
        start_gather(i + 1, 1 - slot)

    y = yrows_ref[slot, 0] * gate_ref[:, 0:1] + yrows_ref[slot, 1] * gate_ref[:, 1:2]
    o_ref[...] = _layer_norm(alpha * x_ref[...] + y, g_ref[...], b_ref[...])


def _moe_combine(x, y_rows, dest, top_g, g, b, alpha, tm):
    T, D = x.shape
    row = pl.BlockSpec((tm, D), lambda i, d1, d2: (i, 0))
    grid_spec = pltpu.PrefetchScalarGridSpec(
        num_scalar_prefetch=2,
        grid=(T // tm,),
        in_specs=[row, pl.BlockSpec((tm, 2), lambda i, d1, d2: (i, 0)),
                  pl.BlockSpec(g.shape, lambda i, d1, d2: (0, 0)), pl.BlockSpec(b.shape, lambda i, d1, d2: (0, 0)),
                  pl.BlockSpec(memory_space=pl.ANY)],
        out_specs=row,
        scratch_shapes=[pltpu.VMEM((2, 2, tm, D), F32), pltpu.SemaphoreType.DMA((2,))],
    )
    return pl.pallas_call(
        functools.partial(_moe_combine_kernel, alpha=alpha),
        grid_spec=grid_spec,
        out_shape=jax.ShapeDtypeStruct((T, D), F32),
        compiler_params=_cparams("arbitrary"),
        name="moe_combine",
    )(dest[:, 0], dest[:, 1], x, top_g, g, b, y_rows)


def _moe_kernel(x_ref, gate_ref, wg_ref, wu_ref, wd_ref, g_ref, b_ref, o_ref, acc_ref, *, alpha):
    e = pl.program_id(1)
    f = pl.program_id(2)
    x = x_ref[...]
    xb = x.astype(BF16)

    @pl.when(jnp.logical_and(e == 0, f == 0))
    def _():
        acc_ref[...] = alpha * x

    gates = gate_ref[...]
    lane = lax.broadcasted_iota(jnp.int32, gates.shape, 1)
    ge = jnp.sum(jnp.where(lane == e, gates, 0.0), axis=1, keepdims=True)
    gate = jnp.dot(xb, wg_ref[0], preferred_element_type=F32)
    up = jnp.dot(xb, wu_ref[0], preferred_element_type=F32)
    acc_ref[...] += jnp.dot((_silu(gate) * up).astype(BF16), wd_ref[0], preferred_element_type=F32) * ge

    @pl.when(jnp.logical_and(e == pl.num_programs(1) - 1, f == pl.num_programs(2) - 1))
    def _():
        o_ref[...] = _layer_norm(acc_ref[...], g_ref[...], b_ref[...])


def _moe(x, gates, w_gu, w_down, g, b, alpha, tm, tf):
    T, D = x.shape
    E, F, _ = w_down.shape
    nf = F // tf
    row = pl.BlockSpec((tm, D), lambda i, e, f: (i, 0))
    return pl.pallas_call(
        functools.partial(_moe_kernel, alpha=alpha),
        grid=(T // tm, E, nf),
        in_specs=[row, pl.BlockSpec((tm, E), lambda i, e, f: (i, 0)),
                  pl.BlockSpec((1, D, tf), lambda i, e, f: (e, 0, f)),
                  pl.BlockSpec((1, D, tf), lambda i, e, f: (e, 0, nf + f)),
                  pl.BlockSpec((1, tf, D), lambda i, e, f: (e, f, 0)), _full(g.shape), _full(b.shape)],
        out_specs=row,
        out_shape=jax.ShapeDtypeStruct((T, D), F32),
        scratch_shapes=[pltpu.VMEM((tm, D), F32)],
        compiler_params=_cparams("parallel", "arbitrary", "arbitrary"),
        name="moe",
    )(x, gates, w_gu, w_gu, w_down, g, b)


def _rope_tables(pos):
    half = ATT_QK_DIM // 2
    inv = ROPE_THETA ** (-jnp.arange(0, ATT_QK_DIM, 2, dtype=F32) / ATT_QK_DIM)
    ang = pos.astype(F32)[:, None] * inv[None, :]
    cos, sin = jnp.cos(ang), jnp.sin(ang)
    return jnp.concatenate([cos] * 4, axis=1), jnp.concatenate([-sin, sin, -sin, sin], axis=1)


def _row_tile(n, target):
    t = min(n, target)
    while n % t:
        t //= 2
    return t


def kernel(x_prompt, x_sample, cache_attn_k, cache_attn_v, cache_mem_k, cache_mem_v, state_conv, state_ssm,
           page_table, mem_prompt, ln_g, ln_b, w_in, conv_w, conv_b, dt_bias, a_log, d_skip, ssm_norm_w,
           lambda_params, attn_norm_w, w_out, w_cq, w_ckv, w_co, w_ffn_gu, w_ffn_down, w_router, w_exp_gu,
           w_exp_down):
    B, S, D = x_prompt.shape
    Bs = x_sample.shape[0]
    depth = w_in.shape[0]
    n_phys, page = cache_attn_k.shape[1], cache_attn_k.shape[2]
    n_pages = page_table.shape[1]
    past_len = n_pages * page
    n_mem = mem_prompt.shape[1]
    alpha = (2 * depth) ** 0.25
    assert S % SSD_CHUNK == 0 and x_sample.shape[1] == 1

    tm_p = _row_tile(S, 512)
    tm_s = _row_tile(Bs, 128)
    cos_p, sin_p = _rope_tables(jnp.arange(S, dtype=jnp.int32))
    cos_s, sin_s = _rope_tables(jnp.full((tm_s,), past_len, jnp.int32))

    hp = x_prompt.reshape(B * S, D)
    hs = x_sample.reshape(Bs, D)
    mem = mem_prompt.reshape(B * n_mem, D)
    cache_k = cache_attn_k.transpose(0, 1, 3, 4, 2)
    cache_v = cache_attn_v.reshape(depth, n_phys, page * ATT_HEADS, ATT_V_DIM)
    dh = D // MEM_HEADS
    mem_rows = lambda t: (t.reshape(depth, Bs, n_mem, MEM_HEADS, dh // LANES, LANES).swapaxes(3, 4)
                          .reshape(depth, Bs, n_mem * dh // LANES * MEM_HEADS, LANES))
    cmem_k, cmem_v = mem_rows(cache_mem_k), mem_rows(cache_mem_v)
    cstate = state_conv.reshape(depth, Bs, (CONV_WIDTH - 1) * CONV_DIM)
    pad_heads = lambda t: jnp.pad(t, [(0, 0)] * (t.ndim - 2) + [(0, 2 * ATT_HEADS - t.shape[-2]), (0, 0)])

    outs = {n: [] for n in ("cp", "sp", "ks", "vs", "cs")}
    kv_prompt = mem_kv_prompt = ssm_sample = None
    for l in range(depth):
        lam_init = 0.8 - 0.6 * math.exp(-0.3 * l)
        wl = w_in[l].astype(BF16)
        wdt = wl[:, 3 * ATT_COLS + SSM_INNER + CONV_DIM:]
        w_proj = (wl, wdt, wdt.T)
        sw = (conv_w[l], conv_b[l][None, :], dt_bias[l][None, :], dt_bias[l][:, None], a_log[l][None, :],
              a_log[l][:, None], d_skip[l][None, :], ssm_norm_w[l][None, :])
        lp = lambda_params[l]
        nw = attn_norm_w[l][None, :]
        wo = w_out[l].astype(BF16)
        wo_att, wo_ssm = wo[:ATT_COLS], wo[ATT_COLS:]
        wcq = w_cq[l].astype(BF16)
        wckv = w_ckv[l].astype(BF16)
        wco = w_co[l].astype(BF16)
        g0, g1, g2 = (ln_g[l, i][None, :] for i in range(3))
        b0, b1, b2 = (ln_b[l, i][None, :] for i in range(3))

        q, k_all, kb, v_all, vb, z, xbc, dt, dtT = _in_proj(hp, cos_p, sin_p, w_proj, tm_p, S // tm_p,
                                                            cache=(kv_prompt,))
        kv_prompt = (k_all, v_all)
        att = _attention(q, kb, vb, lp, nw, lam_init, B, S)
        y, conv_new, h_last = _ssd(xbc, z, dt, dtT, sw, B, S)
        hp = _proj_ln((att, y), (wo_att, wo_ssm), hp, g0, b0, alpha, tm_p)
        outs["cp"].append(conv_new)
        outs["sp"].append(h_last.reshape(B, SSM_HEADS, SSM_HEAD_DIM, SSM_STATE))
        mk_all, mv_all, mk_b, mv_b = _mem_kv(mem, wckv, mem_kv_prompt, _row_tile(B * n_mem, 256))
        mem_kv_prompt = (mk_all, mv_all)
        hp = _cross_attention(hp, wcq, mk_b, mv_b, wco, g1, b1, alpha, B, S, n_mem, tm_p)

        q, kf, _, vf, _, z, xbc, dt, _ = _in_proj(hs, cos_s, sin_s, w_proj, tm_s, 1)
        att = _decode_attention(page_table, l, q.astype(F32).reshape(Bs, 1, ATT_COLS), kf.reshape(Bs, 1, ATT_COLS),
                                pad_heads(vf.reshape(Bs, ATT_HEADS, ATT_V_DIM)), cache_k, cache_v, lp, nw, lam_init)
        att = att[:, :ATT_HEADS].reshape(Bs, ATT_COLS)
        y, conv_new, ssm_sample = _ssd_step(xbc, z, dt, cstate, state_ssm, l, sw, ssm_sample)
        hs = _proj_ln((att, y), (wo_att, wo_ssm), hs, g0, b0, alpha, tm_s)
        outs["ks"].append(kf.reshape(Bs, 1, 2 * ATT_HEADS, ATT_QK_DIM))
        outs["vs"].append(vf.reshape(Bs, 1, ATT_HEADS, ATT_V_DIM))
        outs["cs"].append(conv_new.reshape(Bs, CONV_WIDTH - 1, CONV_DIM))
        (qc,) = _matmul(hs, wcq, (F32,), tm_s, D)
        q_rows = pad_heads(qc.reshape(Bs, MEM_HEADS, dh // LANES, LANES).swapaxes(1, 2))
        oc = _cross_decode(q_rows.reshape(Bs, -1, LANES), cmem_k, cmem_v, l)
        oc = oc.reshape(Bs, dh // LANES, 2 * ATT_HEADS, LANES)[:, :, :MEM_HEADS].swapaxes(1, 2).reshape(Bs, D)
        hs = _proj_ln((oc,), (wco,), hs, g1, b1, alpha, tm_s)

        if l % 2 == 0:
            wgu = w_ffn_gu[l // 2].astype(BF16)
            wd = w_ffn_down[l // 2].astype(BF16)
            tf = wd.shape[0]
            hp = _ffn(hp, wgu, wd, g2, b2, alpha, tm_p, tf)
            hs = _ffn(hs, wgu, wd, g2, b2, alpha, tm_s, tf)
        else:
            wr = w_router[l // 2].astype(BF16)
            wgu = w_exp_gu[l // 2].astype(BF16)
            wd = w_exp_down[l // 2].astype(BF16)
            tf = wd.shape[1] // MOE_F_STEPS
            _, top_i, top_g = _router(hp, wr, tm_p)
            tm_e = _row_tile(2 * B * S, MOE_ROW_TILE)
            src, dest, tile_expert, n_used = _moe_plan(top_i, tm_e, 2 * B * S // tm_e + N_EXPERTS)
            y_rows = _moe_ffn(hp, src, tile_expert, n_used, wgu, wd, tm_e, tf)
            hp = _moe_combine(hp, y_rows, dest, top_g, g2, b2, alpha, tm_p)
            hs = _moe(hs, _router(hs, wr, tm_s)[0], wgu, wd, g2, b2, alpha, tm_s, tf)

    st = {n: jnp.stack(v) for n, v in outs.items()}
    k_prompt = kv_prompt[0].reshape(depth, B, 2 * ATT_HEADS, ATT_QK_DIM, S).transpose(0, 1, 4, 2, 3)
    v_prompt = kv_prompt[1].reshape(depth, B, S, ATT_HEADS, ATT_V_DIM)
    mem_shape = (depth, B, n_mem, dh // LANES, MEM_HEADS, LANES)
    mem_k_prompt, mem_v_prompt = (t.reshape(mem_shape).swapaxes(3, 4).reshape(depth, B, n_mem, MEM_HEADS, dh)
                                  for t in mem_kv_prompt)
    return (hp.reshape(B, S, D), hs.reshape(Bs, 1, D), k_prompt, v_prompt, mem_k_prompt, mem_v_prompt, st["cp"],
            st["sp"], st["ks"], st["vs"], st["cs"], ssm_sample)
```

```python
import functools
import math

import jax
import jax.numpy as jnp
from jax import lax
from jax.experimental import pallas as pl
from jax.experimental.pallas import tpu as pltpu

F32 = jnp.float32
BF16 = jnp.bfloat16

ATT_HEADS = 4
ATT_V_DIM = 128
ATT_QK_DIM = 64
ATT_COLS = 512
SSM_INNER = 512
SSM_HEAD_DIM = 64
SSM_HEADS = 8
SSM_GROUPS = 2
SSM_STATE = 128
CONV_WIDTH = 4
CONV_DIM = 1024
SSD_CHUNK = 128
MEM_HEADS = 4
N_EXPERTS = 8
ROPE_THETA = 10000.0
LN_EPS = 1e-5
NORM_EPS = 1e-5
QK_SCALE = ATT_QK_DIM ** -0.5

LANES = 128
MOE_ROW_TILE = 512
ROW_COPY_UNROLL = 8
MOE_F_STEPS = 1
ATT_HEADS_PER_STEP = 2
VMEM_LIMIT_BYTES = 56 * 1024 * 1024
NT_DIMS = (((1,), (1,)), ((), ()))


def _cparams(*sem):
    return pltpu.CompilerParams(dimension_semantics=sem, vmem_limit_bytes=VMEM_LIMIT_BYTES)


def _silu(x):
    return x / (1.0 + jnp.exp(-x))


def _softplus(x):
    return jnp.maximum(x, 0.0) + jnp.log1p(jnp.exp(-jnp.abs(x)))


def _layer_norm(y, g, b):
    mu = jnp.mean(y, axis=-1, keepdims=True)
    yc = y - mu
    var = jnp.mean(yc * yc, axis=-1, keepdims=True)
    return yc * lax.rsqrt(var + LN_EPS) * g + b


def _full(shape):
    return pl.BlockSpec(shape, lambda *_: (0,) * len(shape))


def _in_proj_kernel(x_ref, cos_ref, sin_ref, wq_ref, wk_ref, wv_ref, wz_ref, wxbc_ref, wdt_ref, wdtT_ref, *rest,
                    cache_layout):
    q_ref, kf_ref, kb_ref, vf_ref, vb_ref, z_ref, xbc_ref, dt_ref, dtT_ref = rest[-9:]
    tm = x_ref.shape[0]
    xb = x_ref[...].astype(BF16)
    cos = jnp.concatenate([cos_ref[...]] * 4, axis=1)
    sin = jnp.concatenate([sin_ref[...]] * 4, axis=1)
    lane = lax.broadcasted_iota(jnp.int32, (tm, ATT_COLS), 1)
    first_half = (lane & (ATT_QK_DIM - 1)) < (ATT_QK_DIM // 2)

    def rope(t):
        partner = jnp.where(first_half, pltpu.roll(t, ATT_COLS - ATT_QK_DIM // 2, 1),
                            pltpu.roll(t, ATT_QK_DIM // 2, 1))
        return t * cos + partner * sin

    q = rope(jnp.dot(xb, wq_ref[...], preferred_element_type=F32))
    q_ref[...] = (q * QK_SCALE).astype(BF16)
    k = rope(jnp.dot(xb, wk_ref[...], preferred_element_type=F32))
    kb_ref[...] = k.astype(BF16)
    v = jnp.dot(xb, wv_ref[...], preferred_element_type=F32)
    vb_ref[...] = v.astype(BF16)
    if cache_layout:
        n_prev = kf_ref.shape[0] - 1
        if n_prev:
            kf_ref[0:n_prev] = rest[0][...]
            vf_ref[0:n_prev] = rest[1][...]
        kf_ref[n_prev] = k.T
        for h in range(ATT_HEADS):
            vf_ref[n_prev, pl.ds(h, tm, stride=ATT_HEADS), :] = v[:, h * ATT_V_DIM:(h + 1) * ATT_V_DIM]
    else:
        kf_ref[...] = k
        vf_ref[...] = v
    z_ref[...] = jnp.dot(xb, wz_ref[...], preferred_element_type=F32)
    xbc_ref[...] = jnp.dot(xb, wxbc_ref[...], preferred_element_type=F32)
    dt_ref[...] = jnp.dot(xb, wdt_ref[...], preferred_element_type=F32)
    dtT_ref[...] = lax.dot_general(wdtT_ref[...], xb, NT_DIMS, preferred_element_type=F32)


def _in_proj(x, cos_t, sin_t, w, tm, n_pos_blocks, cache=None):
    T, D = x.shape
    w_all, wdt, wdtT = w
    cols = lambda n, j: pl.BlockSpec((D, n), lambda i: (0, j))
    row = lambda n: pl.BlockSpec((tm, n), lambda i: (i, 0))
    pos = pl.BlockSpec((tm, 128), lambda i: (i % n_pos_blocks, 0))
    kv_shape = jax.ShapeDtypeStruct((T, ATT_COLS), F32)
    k_shape, v_shape, k_spec, v_spec, prev, prev_specs = kv_shape, kv_shape, row(ATT_COLS), row(ATT_COLS), (), []
    if cache is not None:
        (prev_kv,) = cache
        S = n_pos_blocks * tm
        k_block = lambda n: pl.BlockSpec((n, None, ATT_COLS, tm), lambda i: (0, i // n_pos_blocks, 0, i % n_pos_blocks))
        v_block = lambda n: pl.BlockSpec((n, ATT_HEADS * tm, ATT_V_DIM), lambda i: (0, i, 0))
        n_prev = 0
        if prev_kv is not None:
            prev = tuple(prev_kv)
            n_prev = prev[0].shape[0]
            prev_specs = [k_block(n_prev), v_block(n_prev)]
        k_shape = jax.ShapeDtypeStruct((n_prev + 1, T // S, ATT_COLS, S), F32)
        v_shape = jax.ShapeDtypeStruct((n_prev + 1, ATT_HEADS * T, ATT_V_DIM), F32)
        k_spec, v_spec = k_block(n_prev + 1), v_block(n_prev + 1)
    out_shape = (
        jax.ShapeDtypeStruct((T, ATT_COLS), BF16),
        k_shape,
        jax.ShapeDtypeStruct((T, ATT_COLS), BF16),
        v_shape,
        jax.ShapeDtypeStruct((T, ATT_COLS), BF16),
        jax.ShapeDtypeStruct((T, SSM_INNER), F32),
        jax.ShapeDtypeStruct((T, CONV_DIM), F32),
        jax.ShapeDtypeStruct((T, SSM_HEADS), F32),
        jax.ShapeDtypeStruct((SSM_HEADS, T), F32),
    )
    out_specs = (row(ATT_COLS), k_spec, row(ATT_COLS), v_spec, row(ATT_COLS),
                 row(SSM_INNER), row(CONV_DIM), row(SSM_HEADS),
                 pl.BlockSpec((SSM_HEADS, tm), lambda i: (0, i)))
    return pl.pallas_call(
        functools.partial(_in_proj_kernel, cache_layout=cache is not None),
        grid=(T // tm,),
        in_specs=[row(D), pos, pos, cols(ATT_COLS, 0), cols(ATT_COLS, 1), cols(ATT_COLS, 2), cols(SSM_INNER, 3),
                  cols(CONV_DIM, (3 * ATT_COLS + SSM_INNER) // CONV_DIM), _full(wdt.shape), _full(wdtT.shape)] + prev_specs,
        out_specs=out_specs,
        out_shape=out_shape,
        compiler_params=_cparams("parallel"),
        name="in_proj",
    )(x, cos_t, sin_t, w_all, w_all, w_all, w_all, w_all, wdt, wdtT, *prev)


def _gated_group_norm(y, z, norm_w):
    y = y * _silu(z)
    half = SSM_INNER // SSM_GROUPS
    y2 = y * y
    ms0 = jnp.mean(y2[:, :half], axis=-1, keepdims=True)
    ms1 = jnp.mean(y2[:, half:], axis=-1, keepdims=True)
    lane = lax.broadcasted_iota(jnp.int32, y.shape, 1)
    scale = jnp.where(lane < half, lax.rsqrt(ms0 + NORM_EPS), lax.rsqrt(ms1 + NORM_EPS))
    return y * scale * norm_w


def _ssd_kernel(xbc_ref, z_ref, dt_ref, dtT_ref, convw_ref, convb_ref, dtb_ref, dtbT_ref, alog_ref, alogT_ref,
                dskip_ref, normw_ref, y_ref, conv_out_ref, state_out_ref, ext_ref, h_ref):
    Q = SSD_CHUNK
    c = pl.program_id(1)
    last = pl.num_programs(1) - 1

    @pl.when(c == 0)
    def _():
        ext_ref[...] = jnp.zeros(ext_ref.shape, F32)
        h_ref[...] = jnp.zeros(h_ref.shape, F32)

    cur = xbc_ref[...]
    before = ext_ref[...]
    row8 = lax.broadcasted_iota(jnp.int32, (8, CONV_DIM), 0)
    conv = convb_ref[...] + cur * convw_ref[CONV_WIDTH - 1:CONV_WIDTH, :]
    for s in range(1, CONV_WIDTH):
        shifted = pltpu.roll(cur, s, 0)
        first = jnp.where(row8 < s, pltpu.roll(before, s, 0), shifted[0:8, :])
        shifted = jnp.concatenate([first, shifted[8:, :]], axis=0)
        conv = conv + shifted * convw_ref[CONV_WIDTH - 1 - s:CONV_WIDTH - s, :]
    ext_ref[...] = cur[Q - 8:Q, :]
    conv = _silu(conv)
    xs = conv[:, :SSM_INNER]
    bmat = conv[:, SSM_INNER:SSM_INNER + SSM_GROUPS * SSM_STATE]
    cmat = conv[:, SSM_INNER + SSM_GROUPS * SSM_STATE:]

    dt_col = _softplus(dt_ref[...] + dtb_ref[...])
    a_col = dt_col * (-jnp.exp(alog_ref[...]))
    a_row = _softplus(dtT_ref[...] + dtbT_ref[...]) * (-jnp.exp(alogT_ref[...]))
    ri = lax.broadcasted_iota(jnp.int32, (Q, Q), 0)
    ci = lax.broadcasted_iota(jnp.int32, (Q, Q), 1)
    causal = ci <= ri
    tril = causal.astype(F32)
    triu = (ri <= ci).astype(F32)
    acs_col = jnp.dot(tril, a_col, preferred_element_type=F32, precision=lax.Precision.HIGHEST)
    acs_row = jnp.dot(a_row, triu, preferred_element_type=F32, precision=lax.Precision.HIGHEST)

    lo = ci < SSM_HEAD_DIM
    top = ri < SSM_HEAD_DIM
    pairs = []
    b_groups = [bmat[:, g * SSM_STATE:(g + 1) * SSM_STATE].astype(BF16) for g in range(SSM_GROUPS)]
    c_groups = [cmat[:, g * SSM_STATE:(g + 1) * SSM_STATE].astype(BF16) for g in range(SSM_GROUPS)]
    cb_groups = [lax.dot_general(c_g, b_g, NT_DIMS, preferred_element_type=F32)
                 for c_g, b_g in zip(c_groups, b_groups)]
    for k in range(SSM_HEADS // 2):
        h0, h1 = 2 * k, 2 * k + 1
        g = k // (SSM_HEADS // 2 // SSM_GROUPS)
        b_g, c_g, cb = b_groups[g], c_groups[g], cb_groups[g]
        xs_pair = xs[:, 2 * SSM_HEAD_DIM * k:2 * SSM_HEAD_DIM * (k + 1)]
        dtc = jnp.where(lo, dt_col[:, h0:h0 + 1], dt_col[:, h1:h1 + 1])
        xdt = xs_pair * dtc
        acol = jnp.where(lo, acs_col[:, h0:h0 + 1], acs_col[:, h1:h1 + 1])
        alast0 = acs_row[h0:h0 + 1, Q - 1:Q]
        alast1 = acs_row[h1:h1 + 1, Q - 1:Q]
        alast = jnp.where(lo[0:1, :], alast0, alast1)
        xw = xdt * jnp.exp(alast - acol)
        y_pair = jnp.where(lo, dskip_ref[0:1, h0:h0 + 1], dskip_ref[0:1, h1:h1 + 1]) * xs_pair
        for h, sel in ((h0, lo), (h1, jnp.logical_not(lo))):
            seg = acs_col[:, h:h + 1] - acs_row[h:h + 1, :]
            decay = jnp.exp(jnp.where(causal, seg, -jnp.inf))
            m = (cb * decay).astype(BF16)
            y_pair = y_pair + jnp.dot(m, jnp.where(sel, xdt, 0.0).astype(BF16), preferred_element_type=F32)
        states = jnp.dot(xw.T.astype(BF16), b_g, preferred_element_type=F32)
        h_prev = h_ref[k]
        y_off = lax.dot_general(c_g, h_prev.astype(BF16), NT_DIMS, preferred_element_type=F32)
        y_pair = y_pair + y_off * jnp.exp(acol)
        dec = jnp.where(top[:, 0:1], jnp.exp(alast0), jnp.exp(alast1))
        h_ref[k] = h_prev * dec + states
        pairs.append(y_pair)
    y = jnp.concatenate(pairs, axis=1)
    y_ref[...] = _gated_group_norm(y, z_ref[...], normw_ref[...]).astype(BF16)

    @pl.when(c == last)
    def _():
        conv_out_ref[0] = xbc_ref[Q - (CONV_WIDTH - 1):Q, :]
        state_out_ref[0] = h_ref[...]


def _ssd(xbc, z, dt, dtT, sw, B, S):
    convw, convb, dtb, dtbT, alog, alogT, dskip, normw = sw
    nc = S // SSD_CHUNK
    Q = SSD_CHUNK
    rowb = lambda n: pl.BlockSpec((Q, n), lambda b, c: (b * nc + c, 0))
    npair = SSM_HEADS // 2
    return pl.pallas_call(
        _ssd_kernel,
        grid=(B, nc),
        in_specs=[rowb(CONV_DIM), rowb(SSM_INNER), rowb(SSM_HEADS),
                  pl.BlockSpec((SSM_HEADS, Q), lambda b, c: (0, b * nc + c)),
                  _full(convw.shape), _full(convb.shape), _full(dtb.shape), _full(dtbT.shape),
                  _full(alog.shape), _full(alogT.shape), _full(dskip.shape), _full(normw.shape)],
        out_specs=(rowb(SSM_INNER),
                   pl.BlockSpec((1, CONV_WIDTH - 1, CONV_DIM), lambda b, c: (b, 0, 0)),
                   pl.BlockSpec((1, npair, 2 * SSM_HEAD_DIM, SSM_STATE), lambda b, c: (b, 0, 0, 0))),
        out_shape=(jax.ShapeDtypeStruct((B * S, SSM_INNER), BF16),
                   jax.ShapeDtypeStruct((B, CONV_WIDTH - 1, CONV_DIM), F32),
                   jax.ShapeDtypeStruct((B, npair, 2 * SSM_HEAD_DIM, SSM_STATE), F32)),
        scratch_shapes=[pltpu.VMEM((8, CONV_DIM), F32),
                        pltpu.VMEM((npair, 2 * SSM_HEAD_DIM, SSM_STATE), F32)],
        compiler_params=_cparams("parallel", "arbitrary"),
        name="ssd_scan",
    )(xbc, z, dt, dtT, convw, convb, dtb, dtbT, alog, alogT, dskip, normw)


def _ssd_step_kernel(xbc_ref, z_ref, dt_ref, cstate_ref, hstate_ref, convw_ref, convb_ref, dtb_ref, alog_ref,
                     dskip_ref, normw_ref, *rest):
    y_ref, cout_ref, hout_ref = rest[-3:]
    n_prev = hout_ref.shape[0] - 1
    if n_prev:
        hout_ref[0:n_prev] = rest[0][...]
    bt = xbc_ref.shape[0]
    P, N = SSM_HEAD_DIM, SSM_STATE
    new = xbc_ref[...]
    conv = convb_ref[...] + new * convw_ref[CONV_WIDTH - 1:CONV_WIDTH, :]
    for i in range(CONV_WIDTH - 1):
        conv = conv + cstate_ref[:, i * CONV_DIM:(i + 1) * CONV_DIM] * convw_ref[i:i + 1, :]
    cout_ref[:, 0:(CONV_WIDTH - 2) * CONV_DIM] = cstate_ref[:, CONV_DIM:(CONV_WIDTH - 1) * CONV_DIM]
    cout_ref[:, (CONV_WIDTH - 2) * CONV_DIM:] = new
    conv = _silu(conv)
    xs = conv[:, :SSM_INNER]
    bmat = conv[:, SSM_INNER:SSM_INNER + SSM_GROUPS * N]
    cmat = conv[:, SSM_INNER + SSM_GROUPS * N:]
    dt = _softplus(dt_ref[...] + dtb_ref[...])
    dec = jnp.exp(dt * (-jnp.exp(alog_ref[...])))
    H = SSM_HEADS
    HP = H * P
    exact = lax.Precision.HIGHEST
    eye_n = (lax.broadcasted_iota(jnp.int32, (N, N), 0) == lax.broadcasted_iota(jnp.int32, (N, N), 1)).astype(F32)
    x_t = jnp.concatenate(
        [lax.dot_general(eye_n, xs[:, c * N:(c + 1) * N], NT_DIMS, preferred_element_type=F32, precision=exact)
         for c in range(HP // N)], axis=0)
    rep = (lax.broadcasted_iota(jnp.int32, (HP, H), 0) // P == lax.broadcasted_iota(jnp.int32, (HP, H), 1)).astype(F32)
    per_head = jnp.concatenate([dt, dec, jnp.broadcast_to(dskip_ref[...], (8, H))], axis=0)
    cols = lax.dot_general(rep, per_head, NT_DIMS, preferred_element_type=F32, precision=exact)
    d_col = cols[:, 2 * bt:2 * bt + 1]
    first_group = lax.broadcasted_iota(jnp.int32, (HP, 1), 0) < HP // SSM_GROUPS
    lane_b = lax.broadcasted_iota(jnp.int32, (HP, bt), 1)
    y_t = jnp.zeros((HP, bt), F32)
    for b in range(bt):
        x_col = x_t[:, b:b + 1]
        b_sel = jnp.where(first_group, bmat[b:b + 1, 0:N], bmat[b:b + 1, N:2 * N])
        h_new = (hstate_ref[b].reshape(HP, N) * cols[:, bt + b:bt + b + 1]
                 + (x_col * cols[:, b:b + 1]) * b_sel)
        hout_ref[n_prev, b] = h_new.reshape(H, P, N)
        c2 = jnp.concatenate([cmat[b:b + 1, 0:N], cmat[b:b + 1, N:2 * N], jnp.zeros((6, N), F32)], axis=0)
        yb = lax.dot_general(h_new.astype(BF16), c2.astype(BF16), NT_DIMS, preferred_element_type=F32)
        y_col = jnp.where(first_group, yb[:, 0:1], yb[:, 1:2]) + d_col * x_col
        y_t = jnp.where(lane_b == b, y_col, y_t)
    eye_b = (lax.broadcasted_iota(jnp.int32, (bt, bt), 0) == lax.broadcasted_iota(jnp.int32, (bt, bt), 1)).astype(F32)
    y = lax.dot_general(eye_b, y_t, NT_DIMS, preferred_element_type=F32, precision=exact)
    y_ref[...] = _gated_group_norm(y, z_ref[...], normw_ref[...]).astype(BF16)


def _ssd_step(xbc, z, dt, cstate, hstate, layer, sw, h_prev=None, bt=8):
    convw, convb, dtb, _, alog, _, dskip, normw = sw
    Bs = xbc.shape[0]
    cw = (CONV_WIDTH - 1) * CONV_DIM
    rowb = lambda n: pl.BlockSpec((bt, n), lambda i: (i, 0))
    hshape = (bt, SSM_HEADS, SSM_HEAD_DIM, SSM_STATE)
    layers = lambda n: pl.BlockSpec((n,) + hshape, lambda i: (0, i, 0, 0, 0))
    prev = () if h_prev is None else (h_prev,)
    n_prev = 0 if h_prev is None else h_prev.shape[0]
    return pl.pallas_call(
        _ssd_step_kernel,
        grid=(Bs // bt,),
        in_specs=[rowb(CONV_DIM), rowb(SSM_INNER), rowb(SSM_HEADS),
                  pl.BlockSpec((None, bt, cw), lambda i: (layer, i, 0)),
                  pl.BlockSpec((None,) + hshape, lambda i: (layer, i, 0, 0, 0)),
                  _full(convw.shape), _full(convb.shape), _full(dtb.shape), _full(alog.shape),
                  _full(dskip.shape), _full(normw.shape)] + [layers(n_prev)] * len(prev),
        out_specs=(rowb(SSM_INNER), rowb(cw), layers(n_prev + 1)),
        out_shape=(jax.ShapeDtypeStruct((Bs, SSM_INNER), BF16),
                   jax.ShapeDtypeStruct((Bs, cw), F32),
                   jax.ShapeDtypeStruct((n_prev + 1,) + hstate.shape[1:], F32)),
        compiler_params=_cparams("parallel"),
        name="ssd_step",
    )(xbc, z, dt, cstate, hstate, convw, convb, dtb, alog, dskip, normw, *prev)


def _lambda_value(lp, lam_init):
    d1 = jnp.sum(lp[0:1, :] * lp[1:2, :], axis=1, keepdims=True)
    d2 = jnp.sum(lp[2:3, :] * lp[3:4, :], axis=1, keepdims=True)
    return jnp.exp(d1) - jnp.exp(d2) + lam_init


def _head_norm(o, nw, lam_init):
    return o * lax.rsqrt(jnp.mean(o * o, axis=-1, keepdims=True) + NORM_EPS) * nw * (1.0 - lam_init)


def _attn_kernel(lp_ref, nw_ref, q_ref, k_ref, v_ref, o_ref, m_ref, l_ref, acc_ref, *, lam_init):
    tq = q_ref.shape[0]
    n_heads = q_ref.shape[1] // ATT_V_DIM
    qi = pl.program_id(2)
    lane = lax.broadcasted_iota(jnp.int32, (tq, ATT_V_DIM), 1)
    q_maps = []
    for h in range(n_heads):
        q = q_ref[:, h * ATT_V_DIM:(h + 1) * ATT_V_DIM]
        zero = jnp.zeros_like(q)
        q_maps += [jnp.where(lane < ATT_QK_DIM, q, zero), jnp.where(lane >= ATT_QK_DIM, q, zero)]
    m_ref[...] = jnp.full(m_ref.shape, -jnp.inf, F32)
    l_ref[...] = jnp.zeros(l_ref.shape, F32)
    acc_ref[...] = jnp.zeros(acc_ref.shape, F32)
    below_diag = (lax.broadcasted_iota(jnp.int32, (tq, tq), 1) <= lax.broadcasted_iota(jnp.int32, (tq, tq), 0))

    def block(j, diagonal):
        start = pl.multiple_of(j * tq, tq)
        for i in range(2 * n_heads):
            cols = slice((i // 2) * ATT_V_DIM, (i // 2 + 1) * ATT_V_DIM)
            k = k_ref[pl.ds(start, tq), cols]
            v = v_ref[pl.ds(start, tq), cols]
            s = lax.dot_general(q_maps[i], k, NT_DIMS, preferred_element_type=F32)
            if diagonal:
                s = jnp.where(below_diag, s, -jnp.inf)
            chunks = [s[:, c * LANES:(c + 1) * LANES] for c in range(tq // LANES)]
            top = functools.reduce(jnp.maximum, chunks)
            m_old = m_ref[i]
            m_new = jnp.maximum(m_old, jnp.max(top, axis=1, keepdims=True))
            alpha = jnp.exp(m_old - m_new)
            ps = [jnp.exp(c - m_new) for c in chunks]
            l_ref[i] = alpha * l_ref[i] + functools.reduce(jnp.add, ps)
            p = jnp.concatenate(ps, axis=1).astype(BF16)
            acc_ref[i] = alpha * acc_ref[i] + jnp.dot(p, v, preferred_element_type=F32)
            m_ref[i] = m_new

    def full_block(j, carry):
        block(j, False)
        return carry

    lax.fori_loop(0, qi, full_block, 0)
    block(qi, True)
    lam = _lambda_value(lp_ref[...], lam_init)
    for h in range(n_heads):
        l1 = jnp.sum(l_ref[2 * h], axis=1, keepdims=True)
        l2 = jnp.sum(l_ref[2 * h + 1], axis=1, keepdims=True)
        o = acc_ref[2 * h] / l1 - lam * (acc_ref[2 * h + 1] / l2)
        o_ref[:, h * ATT_V_DIM:(h + 1) * ATT_V_DIM] = _head_norm(o, nw_ref[...], lam_init).astype(BF16)


def _attention(q, k, v, lp, nw, lam_init, B, S):
    tq = min(512, S)
    nq = S // tq
    hg = ATT_HEADS_PER_STEP
    qspec = pl.BlockSpec((tq, hg * ATT_V_DIM), lambda b, h, i: (b * nq + i, h))
    kspec = pl.BlockSpec((S, hg * ATT_V_DIM), lambda b, h, i: (b, h))
    return pl.pallas_call(
        functools.partial(_attn_kernel, lam_init=lam_init),
        grid=(B, ATT_HEADS // hg, nq),
        in_specs=[_full(lp.shape), _full(nw.shape), qspec, kspec, kspec],
        out_specs=qspec,
        out_shape=jax.ShapeDtypeStruct((B * S, ATT_COLS), BF16),
        scratch_shapes=[pltpu.VMEM((2 * hg, tq, LANES), F32), pltpu.VMEM((2 * hg, tq, LANES), F32),
                        pltpu.VMEM((2 * hg, tq, ATT_V_DIM), F32)],
        compiler_params=_cparams("parallel", "parallel", "arbitrary"),
        name="diff_attention",
    )(lp, nw, q, k, v)


def _decode_attn_kernel(pt_ref, lp_ref, nw_ref, q_ref, kn_ref, vn_ref, *refs, n_pages, lam_init):
    o_ref = refs[-1]
    for sq in range(q_ref.shape[0]):
        pages = refs[2 * n_pages * sq:2 * n_pages * (sq + 1)]
        _decode_one(lp_ref, nw_ref, q_ref.at[sq], kn_ref.at[sq], vn_ref.at[sq], pages[:n_pages], pages[n_pages:],
                    o_ref.at[sq], lam_init)


def _decode_one(lp_ref, nw_ref, q_ref, kn_ref, vn_ref, k_refs, v_refs, o_ref, lam_init):
    R = 2 * ATT_HEADS
    page = k_refs[0].shape[2]
    r = lax.broadcasted_iota(jnp.int32, (R, ATT_COLS), 0)
    grp = lax.broadcasted_iota(jnp.int32, (R, ATT_COLS), 1) // ATT_QK_DIM
    target = jnp.where(r < ATT_HEADS, 2 * r, 2 * (r - ATT_HEADS) + 1)
    qf = jnp.where(grp == target, q_ref[...].astype(BF16).astype(F32), 0.0)
    qt = qf.astype(BF16)
    s = [jnp.dot(qt, k_ref[...].reshape(ATT_COLS, page).astype(BF16), preferred_element_type=F32)
         for k_ref in k_refs]
    s_new = jnp.sum(qf * kn_ref[...].astype(BF16).astype(F32), axis=1, keepdims=True)
    m = s_new
    for sj in s:
        m = jnp.maximum(m, jnp.max(sj, axis=1, keepdims=True))
    p = [jnp.exp(sj - m) for sj in s]
    p_new = jnp.exp(s_new - m)
    l = p_new
    for pj in p:
        l = l + jnp.sum(pj, axis=1, keepdims=True)
    inv_l = 1.0 / l
    lam = _lambda_value(lp_ref[...], lam_init)
    diff = lambda t: t - lam * pltpu.roll(t, ATT_HEADS, 0)
    a = [diff(pj * inv_l).astype(BF16) for pj in p]
    a_new = diff(jnp.broadcast_to(p_new * inv_l, (R, 128)))[:, 0:1].astype(BF16).astype(F32)
    row = lax.broadcasted_iota(jnp.int32, (R, ATT_V_DIM), 0)
    out = a_new * vn_ref[...].astype(BF16).astype(F32)
    for h in range(ATT_HEADS):
        acc = jnp.zeros((R, ATT_V_DIM), F32)
        for aj, v_ref in zip(a, v_refs):
            v_h = v_ref[pl.ds(h, page, stride=ATT_HEADS), :].astype(BF16)
            acc = acc + jnp.dot(aj, v_h, preferred_element_type=F32)
        out = out + jnp.where(row == h, acc, 0.0)
    o_ref[...] = _head_norm(out, nw_ref[...], lam_init)


def _decode_attention(pt, layer, q, k_new, v_new, cache_k, cache_v, lp, nw, lam_init, bt=2):
    Bs, n_pages = pt.shape
    R = 2 * ATT_HEADS
    qspec = pl.BlockSpec((bt, 1, ATT_COLS), lambda b, pt: (b, 0, 0))
    vspec = pl.BlockSpec((bt, R, ATT_V_DIM), lambda b, pt: (b, 0, 0))
    pages = []
    for sq in range(bt):
        pages += [pl.BlockSpec((None, None) + cache_k.shape[2:],
                               lambda b, pt, j=j, sq=sq: (layer, pt[b * bt + sq, j], 0, 0, 0)) for j in range(n_pages)]
        pages += [pl.BlockSpec((None, None) + cache_v.shape[2:],
                               lambda b, pt, j=j, sq=sq: (layer, pt[b * bt + sq, j], 0, 0)) for j in range(n_pages)]
    grid_spec = pltpu.PrefetchScalarGridSpec(
        num_scalar_prefetch=1,
        grid=(Bs // bt,),
        in_specs=[pl.BlockSpec(lp.shape, lambda b, pt: (0, 0)), pl.BlockSpec(nw.shape, lambda b, pt: (0, 0)),
                  qspec, qspec, vspec] + pages,
        out_specs=vspec,
    )
    return pl.pallas_call(
        functools.partial(_decode_attn_kernel, n_pages=n_pages, lam_init=lam_init),
        grid_spec=grid_spec,
        out_shape=jax.ShapeDtypeStruct((Bs, R, ATT_V_DIM), F32),
        compiler_params=_cparams("parallel"),
        name="decode_attention",
    )(pt, lp, nw, q, k_new, v_new, *(([cache_k] * n_pages + [cache_v] * n_pages) * bt))


def _proj_ln_kernel(*refs, n_in, alpha):
    a_refs = refs[:n_in]
    w_refs = refs[n_in:2 * n_in]
    x_ref, g_ref, b_ref, o_ref = refs[2 * n_in:]
    acc = alpha * x_ref[...]
    for a_ref, w_ref in zip(a_refs, w_refs):
        acc = acc + jnp.dot(a_ref[...].astype(BF16), w_ref[...], preferred_element_type=F32)
    o_ref[...] = _layer_norm(acc, g_ref[...], b_ref[...])


def _proj_ln(acts, ws, x, g, b, alpha, tm):
    T, D = x.shape
    row = lambda n: pl.BlockSpec((tm, n), lambda i: (i, 0))
    return pl.pallas_call(
        functools.partial(_proj_ln_kernel, n_in=len(acts), alpha=alpha),
        grid=(T // tm,),
        in_specs=[row(a.shape[1]) for a in acts] + [_full(w.shape) for w in ws] + [row(D), _full(g.shape), _full(b.shape)],
        out_specs=row(D),
        out_shape=jax.ShapeDtypeStruct((T, D), F32),
        compiler_params=_cparams("parallel"),
        name="proj_ln",
    )(*acts, *ws, x, g, b)


def _matmul_kernel(x_ref, w_ref, *o_refs):
    y = jnp.dot(x_ref[...].astype(BF16), w_ref[...], preferred_element_type=F32)
    for o_ref in o_refs:
        o_ref[...] = y.astype(o_ref.dtype)


def _matmul(x, w, out_dtypes, tm, tn):
    M, K = x.shape
    N = w.shape[1]
    ospec = pl.BlockSpec((tm, tn), lambda i, j: (i, j))
    return pl.pallas_call(
        _matmul_kernel,
        grid=(M // tm, N // tn),
        in_specs=[pl.BlockSpec((tm, K), lambda i, j: (i, 0)), pl.BlockSpec((K, tn), lambda i, j: (0, j))],
        out_specs=tuple(ospec for _ in out_dtypes),
        out_shape=tuple(jax.ShapeDtypeStruct((M, N), d) for d in out_dtypes),
        compiler_params=_cparams("parallel", "parallel"),
        name="matmul",
    )(x, w)


def _mem_kv_kernel(x_ref, w_ref, *rest):
    mk_ref, mv_ref, mkb_ref, mvb_ref = rest[-4:]
    tm = x_ref.shape[0]
    D = w_ref.shape[1] // 2
    dh = D // MEM_HEADS
    halves = dh // LANES
    y = jnp.dot(x_ref[...].astype(BF16), w_ref[...], preferred_element_type=F32)
    mkb_ref[...] = y[:, :D].astype(BF16)
    mvb_ref[...] = y[:, D:].astype(BF16)
    n_prev = mk_ref.shape[0] - 1
    if n_prev:
        mk_ref[0:n_prev] = rest[0][...]
        mv_ref[0:n_prev] = rest[1][...]
    for out_ref, base in ((mk_ref, 0), (mv_ref, D)):
        for h in range(MEM_HEADS):
            for c in range(halves):
                col = base + h * dh + c * LANES
                out_ref[n_prev, pl.ds(c * MEM_HEADS + h, tm, stride=halves * MEM_HEADS), :] = y[:, col:col + LANES]


def _mem_kv(mem, w, prev, tm):
    M, D = mem.shape
    rows_per_token = D // LANES
    rows = lambda n: pl.BlockSpec((n, tm * rows_per_token, LANES), lambda i: (0, i, 0))
    flat = pl.BlockSpec((tm, D), lambda i: (i, 0))
    prev = () if prev is None else tuple(prev)
    n_prev = prev[0].shape[0] if prev else 0
    rows_shape = jax.ShapeDtypeStruct((n_prev + 1, M * rows_per_token, LANES), F32)
    return pl.pallas_call(
        _mem_kv_kernel,
        grid=(M // tm,),
        in_specs=[flat, _full(w.shape)] + [rows(n_prev)] * len(prev),
        out_specs=(rows(n_prev + 1), rows(n_prev + 1), flat, flat),
        out_shape=(rows_shape, rows_shape, jax.ShapeDtypeStruct((M, D), BF16), jax.ShapeDtypeStruct((M, D), BF16)),
        compiler_params=_cparams("parallel"),
        name="mem_kv",
    )(mem, w, *prev)


def _softmax_rows(s):
    m = jnp.max(s, axis=1, keepdims=True)
    p = jnp.exp(s - m)
    return p / jnp.sum(p, axis=1, keepdims=True)


def _cross_kernel(x_ref, wq_ref, mk_ref, mv_ref, wo_ref, g_ref, b_ref, o_ref, *, alpha):
    x = x_ref[...]
    D = x.shape[1]
    dh = D // MEM_HEADS
    q = jnp.dot(x.astype(BF16), wq_ref[...], preferred_element_type=F32).astype(BF16)
    outs = []
    for h in range(MEM_HEADS):
        sl = slice(h * dh, (h + 1) * dh)
        s = lax.dot_general(q[:, sl], mk_ref[:, sl], NT_DIMS, preferred_element_type=F32) * (dh ** -0.5)
        outs.append(jnp.dot(_softmax_rows(s).astype(BF16), mv_ref[:, sl], preferred_element_type=F32).astype(BF16))
    o = jnp.concatenate(outs, axis=1)
    y = alpha * x + jnp.dot(o, wo_ref[...], preferred_element_type=F32)
    o_ref[...] = _layer_norm(y, g_ref[...], b_ref[...])


def _cross_attention(x, wq, mk, mv, wo, g, b, alpha, B, S, n_mem, tm):
    T, D = x.shape
    nt = S // tm
    row = pl.BlockSpec((tm, D), lambda bi, i: (bi * nt + i, 0))
    mem = pl.BlockSpec((n_mem, D), lambda bi, i: (bi, 0))
    return pl.pallas_call(
        functools.partial(_cross_kernel, alpha=alpha),
        grid=(B, nt),
        in_specs=[row, _full(wq.shape), mem, mem, _full(wo.shape), _full(g.shape), _full(b.shape)],
        out_specs=row,
        out_shape=jax.ShapeDtypeStruct((T, D), F32),
        compiler_params=_cparams("parallel", "parallel"),
        name="cross_attention",
    )(x, wq, mk, mv, wo, g, b)


def _cross_decode_kernel(q_ref, mk_ref, mv_ref, o_ref):
    R = q_ref.shape[1] // 2
    rows_kv = mk_ref.shape[1]
    dh = 2 * LANES
    own = (lax.broadcasted_iota(jnp.int32, (R, rows_kv), 1) & (R - 1)) == lax.broadcasted_iota(jnp.int32, (R, rows_kv), 0)
    for b in range(q_ref.shape[0]):
        part = lax.dot_general(q_ref[b].astype(BF16), mk_ref[b].astype(BF16), NT_DIMS,
                               preferred_element_type=F32)
        s = (part[:R] + pltpu.roll(part[R:], rows_kv - MEM_HEADS, 1)) * (dh ** -0.5)
        p = _softmax_rows(jnp.where(own, s, -jnp.inf))
        a = jnp.concatenate([p, pltpu.roll(p, MEM_HEADS, 1)], axis=0).astype(BF16)
        o_ref[b] = jnp.dot(a, mv_ref[b].astype(BF16), preferred_element_type=F32)


def _cross_decode(q, mem_k, mem_v, layer, bt=4):
    Bs, R2, _ = q.shape
    tok = pl.BlockSpec((bt, R2, LANES), lambda b: (b, 0, 0))
    mem = pl.BlockSpec((None, bt) + mem_k.shape[2:], lambda b: (layer, b, 0, 0))
    return pl.pallas_call(
        _cross_decode_kernel,
        grid=(Bs // bt,),
        in_specs=[tok, mem, mem],
        out_specs=tok,
        out_shape=jax.ShapeDtypeStruct((Bs, R2, LANES), F32),
        compiler_params=_cparams("parallel"),
        name="cross_decode",
    )(q, mem_k, mem_v)


def _ffn_kernel(x_ref, wg_ref, wu_ref, wd_ref, g_ref, b_ref, o_ref, acc_ref, *, alpha):
    f = pl.program_id(1)
    x = x_ref[...]
    xb = x.astype(BF16)

    @pl.when(f == 0)
    def _():
        acc_ref[...] = alpha * x

    gate = jnp.dot(xb, wg_ref[...], preferred_element_type=F32)
    up = jnp.dot(xb, wu_ref[...], preferred_element_type=F32)
    acc_ref[...] += jnp.dot((_silu(gate) * up).astype(BF16), wd_ref[...], preferred_element_type=F32)

    @pl.when(f == pl.num_programs(1) - 1)
    def _():
        o_ref[...] = _layer_norm(acc_ref[...], g_ref[...], b_ref[...])


def _ffn(x, w_gu, w_down, g, b, alpha, tm, tf):
    T, D = x.shape
    F = w_down.shape[0]
    nf = F // tf
    row = pl.BlockSpec((tm, D), lambda i, f: (i, 0))
    mode = dict(pipeline_mode=pl.Buffered(1)) if nf == 1 else {}
    return pl.pallas_call(
        functools.partial(_ffn_kernel, alpha=alpha),
        grid=(T // tm, nf),
        in_specs=[row, pl.BlockSpec((D, tf), lambda i, f: (0, f), **mode),
                  pl.BlockSpec((D, tf), lambda i, f: (0, nf + f), **mode),
                  pl.BlockSpec((tf, D), lambda i, f: (f, 0), **mode), _full(g.shape), _full(b.shape)],
        out_specs=row,
        out_shape=jax.ShapeDtypeStruct((T, D), F32),
        scratch_shapes=[pltpu.VMEM((tm, D), F32)],
        compiler_params=_cparams("parallel", "arbitrary"),
        name="ffn",
    )(x, w_gu, w_gu, w_down, g, b)


def _router_kernel(x_ref, wr_ref, gate_ref, top_i_ref, top_g_ref):
    logits = jnp.dot(x_ref[...].astype(BF16), wr_ref[...], preferred_element_type=F32)
    lane = lax.broadcasted_iota(jnp.int32, logits.shape, 1).astype(F32)
    none = float(N_EXPERTS)
    m1 = jnp.max(logits, axis=1, keepdims=True)
    i1 = jnp.min(jnp.where(logits == m1, lane, none), axis=1, keepdims=True)
    rest = jnp.where(lane == i1, -jnp.inf, logits)
    m2 = jnp.max(rest, axis=1, keepdims=True)
    i2 = jnp.min(jnp.where(rest == m2, lane, none), axis=1, keepdims=True)
    e = jnp.exp(m2 - m1)
    den = 1.0 + e
    gate_ref[...] = jnp.where(lane == i1, 1.0 / den, 0.0) + jnp.where(lane == i2, e / den, 0.0)
    first = lax.broadcasted_iota(jnp.int32, top_i_ref.shape, 1) == 0
    top_i_ref[...] = jnp.where(first, i1, i2).astype(jnp.int32)
    top_g_ref[...] = jnp.where(first, 1.0 / den, e / den)


def _router(x, wr, tm):
    T, D = x.shape
    top = pl.BlockSpec((tm, 2), lambda i: (i, 0))
    return pl.pallas_call(
        _router_kernel,
        grid=(T // tm,),
        in_specs=[pl.BlockSpec((tm, D), lambda i: (i, 0)), _full(wr.shape)],
        out_specs=(pl.BlockSpec((tm, N_EXPERTS), lambda i: (i, 0)), top, top),
        out_shape=(jax.ShapeDtypeStruct((T, N_EXPERTS), F32), jax.ShapeDtypeStruct((T, 2), jnp.int32),
                   jax.ShapeDtypeStruct((T, 2), F32)),
        compiler_params=_cparams("parallel"),
        name="router",
    )(x, wr)


def _moe_plan(top_i, tm_e, n_tiles):
    T = top_i.shape[0]
    chosen = (top_i[:, :, None] == jnp.arange(N_EXPERTS, dtype=jnp.int32)[None, None, :]).any(axis=1)
    chosen = chosen.astype(jnp.int32)
    count = chosen.sum(axis=0)
    rank = jnp.cumsum(chosen, axis=0) - chosen
    tiles = (count + tm_e - 1) // tm_e
    tile_end = jnp.cumsum(tiles)
    row_start = (tile_end - tiles) * tm_e
    dest = jnp.take_along_axis(row_start[None, :] + rank, top_i, axis=1).astype(jnp.int32)
    token = jnp.broadcast_to(jnp.arange(T, dtype=jnp.int32)[:, None], (T, 2))
    src = jnp.zeros((n_tiles * tm_e,), jnp.int32).at[dest.reshape(-1)].set(token.reshape(-1))
    n_used = tile_end[-1]
    tile = jnp.arange(n_tiles, dtype=jnp.int32)
    tile_expert = (tile_end[None, :] <= jnp.minimum(tile, n_used - 1)[:, None]).sum(axis=1).astype(jnp.int32)
    return src, dest, tile_expert, n_used.reshape(1).astype(jnp.int32)


def _row_copy(src_hbm, src_row, dst_ref, dst_row, sem):
    return pltpu.make_async_copy(src_hbm.at[pl.ds(src_row, 1), :], dst_ref.at[pl.ds(dst_row, 1), :], sem)


def _moe_ffn_kernel(te_ref, nv_ref, src_ref, x_hbm, wg_ref, wu_ref, wd_ref, y_ref, xrows_ref, xb_ref, sem):
    tm_e = xrows_ref.shape[1]
    i = pl.program_id(0)
    f = pl.program_id(1)
    n_used = nv_ref[0]
    slot = lax.rem(i, 2)

    def start_gather(tile, slot):
        def body(r8, carry):
            for u in range(ROW_COPY_UNROLL):
                r = r8 * ROW_COPY_UNROLL + u
                _row_copy(x_hbm, src_ref[tile * tm_e + r], xrows_ref.at[slot], r, sem.at[slot]).start()
            return carry
        lax.fori_loop(0, tm_e // ROW_COPY_UNROLL, body, 0)

    @pl.when(f == 0)
    def _():
        @pl.when(i == 0)
        def _():
            start_gather(0, 0)

        @pl.when(i < n_used)
        def _():
            pltpu.make_async_copy(x_hbm.at[pl.ds(0, tm_e), :], xrows_ref.at[slot], sem.at[slot]).wait()
            xb_ref[...] = xrows_ref[slot].astype(BF16)

        @pl.when(i + 1 < n_used)
        def _():
            start_gather(i + 1, 1 - slot)

    @pl.when(i < n_used)
    def _():
        xb = xb_ref[...]
        gate = jnp.dot(xb, wg_ref[...], preferred_element_type=F32)
        up = jnp.dot(xb, wu_ref[...], preferred_element_type=F32)
        part = jnp.dot((_silu(gate) * up).astype(BF16), wd_ref[...], preferred_element_type=F32)

        @pl.when(f == 0)
        def _():
            y_ref[...] = part

        @pl.when(f > 0)
        def _():
            y_ref[...] += part

    @pl.when(jnp.logical_and(i >= n_used, f == 0))
    def _():
        y_ref[...] = jnp.zeros(y_ref.shape, F32)


def _moe_ffn(x, src, tile_expert, n_used, w_gu, w_down, tm_e, tf):
    T, D = x.shape
    E, F, _ = w_down.shape
    nf = F // tf
    n_tiles = tile_expert.shape[0]
    grid_spec = pltpu.PrefetchScalarGridSpec(
        num_scalar_prefetch=3,
        grid=(n_tiles, nf),
        in_specs=[pl.BlockSpec(memory_space=pl.ANY),
                  pl.BlockSpec((None, D, tf), lambda i, f, te, nv, src: (te[i], 0, f)),
                  pl.BlockSpec((None, D, tf), lambda i, f, te, nv, src: (te[i], 0, nf + f)),
                  pl.BlockSpec((None, tf, D), lambda i, f, te, nv, src: (te[i], f, 0))],
        out_specs=pl.BlockSpec((tm_e, D), lambda i, f, te, nv, src: (i, 0)),
        scratch_shapes=[pltpu.VMEM((2, tm_e, D), F32), pltpu.VMEM((tm_e, D), BF16), pltpu.SemaphoreType.DMA((2,))],
    )
    return pl.pallas_call(
        _moe_ffn_kernel,
        grid_spec=grid_spec,
        out_shape=jax.ShapeDtypeStruct((n_tiles * tm_e, D), F32),
        compiler_params=_cparams("arbitrary", "arbitrary"),
        name="moe_ffn",
    )(tile_expert, n_used, src, x, w_gu, w_gu, w_down)


def _moe_combine_kernel(d1_ref, d2_ref, x_ref, gate_ref, g_ref, b_ref, y_hbm, o_ref, yrows_ref, sem, *, alpha):
    tm = yrows_ref.shape[2]
    i = pl.program_id(0)
    slot = lax.rem(i, 2)

    def start_gather(tile, slot):
        def body(r8, carry):
            for u in range(ROW_COPY_UNROLL):
                r = r8 * ROW_COPY_UNROLL + u
                _row_copy(y_hbm, d1_ref[tile * tm + r], yrows_ref.at[slot, 0], r, sem.at[slot]).start()
                _row_copy(y_hbm, d2_ref[tile * tm + r], yrows_ref.at[slot, 1], r, sem.at[slot]).start()
            return carry
        lax.fori_loop(0, tm // ROW_COPY_UNROLL, body, 0)

    @pl.when(i == 0)
    def _():
        start_gather(0, 0)

    for k in range(2):
        pltpu.make_async_copy(y_hbm.at[pl.ds(0, tm), :], yrows_ref.at[slot, k], sem.at[slot]).wait()

    @pl.when(i + 1 < pl.num_programs(0))
    def _():
        start_gather(i + 1, 1 - slot)

    y = yrows_ref[slot, 0] * gate_ref[:, 0:1] + yrows_ref[slot, 1] * gate_ref[:, 1:2]
    o_ref[...] = _layer_norm(alpha * x_ref[...] + y, g_ref[...], b_ref[...])


def _moe_combine(x, y_rows, dest, top_g, g, b, alpha, tm):
    T, D = x.shape
    row = pl.BlockSpec((tm, D), lambda i, d1, d2: (i, 0))
    grid_spec = pltpu.PrefetchScalarGridSpec(
        num_scalar_prefetch=2,
        grid=(T // tm,),
        in_specs=[row, pl.BlockSpec((tm, 2), lambda i, d1, d2: (i, 0)),
                  pl.BlockSpec(g.shape, lambda i, d1, d2: (0, 0)), pl.BlockSpec(b.shape, lambda i, d1, d2: (0, 0)),
                  pl.BlockSpec(memory_space=pl.ANY)],
        out_specs=row,
        scratch_shapes=[pltpu.VMEM((2, 2, tm, D), F32), pltpu.SemaphoreType.DMA((2,))],
    )
    return pl.pallas_call(
        functools.partial(_moe_combine_kernel, alpha=alpha),
        grid_spec=grid_spec,
        out_shape=jax.ShapeDtypeStruct((T, D), F32),
        compiler_params=_cparams("arbitrary"),
        name="moe_combine",
    )(dest[:, 0], dest[:, 1], x, top_g, g, b, y_rows)


def _moe_kernel(x_ref, gate_ref, wg_ref, wu_ref, wd_ref, g_ref, b_ref, o_ref, acc_ref, *, alpha):
    e = pl.program_id(1)
    f = pl.program_id(2)
    x = x_ref[...]
    xb = x.astype(BF16)

    @pl.when(jnp.logical_and(e == 0, f == 0))
    def _():
        acc_ref[...] = alpha * x

    gates = gate_ref[...]
    lane = lax.broadcasted_iota(jnp.int32, gates.shape, 1)
    ge = jnp.sum(jnp.where(lane == e, gates, 0.0), axis=1, keepdims=True)
    gate = jnp.dot(xb, wg_ref[0], preferred_element_type=F32)
    up = jnp.dot(xb, wu_ref[0], preferred_element_type=F32)
    acc_ref[...] += jnp.dot((_silu(gate) * up).astype(BF16), wd_ref[0], preferred_element_type=F32) * ge

    @pl.when(jnp.logical_and(e == pl.num_programs(1) - 1, f == pl.num_programs(2) - 1))
    def _():
        o_ref[...] = _layer_norm(acc_ref[...], g_ref[...], b_ref[...])


def _moe(x, gates, w_gu, w_down, g, b, alpha, tm, tf):
    T, D = x.shape
    E, F, _ = w_down.shape
    nf = F // tf
    row = pl.BlockSpec((tm, D), lambda i, e, f: (i, 0))
    return pl.pallas_call(
        functools.partial(_moe_kernel, alpha=alpha),
        grid=(T // tm, E, nf),
        in_specs=[row, pl.BlockSpec((tm, E), lambda i, e, f: (i, 0)),
                  pl.BlockSpec((1, D, tf), lambda i, e, f: (e, 0, f)),
                  pl.BlockSpec((1, D, tf), lambda i, e, f: (e, 0, nf + f)),
                  pl.BlockSpec((1, tf, D), lambda i, e, f: (e, f, 0)), _full(g.shape), _full(b.shape)],
        out_specs=row,
        out_shape=jax.ShapeDtypeStruct((T, D), F32),
        scratch_shapes=[pltpu.VMEM((tm, D), F32)],
        compiler_params=_cparams("parallel", "arbitrary", "arbitrary"),
        name="moe",
    )(x, gates, w_gu, w_gu, w_down, g, b)


def _rope_tables(pos):
    half = ATT_QK_DIM // 2
    inv = ROPE_THETA ** (-jnp.arange(0, ATT_QK_DIM, 2, dtype=F32) / ATT_QK_DIM)
    ang = pos.astype(F32)[:, None] * inv[None, :]
    cos, sin = jnp.cos(ang), jnp.sin(ang)
    return jnp.concatenate([cos] * 4, axis=1), jnp.concatenate([-sin, sin, -sin, sin], axis=1)


def _row_tile(n, target):
    t = min(n, target)
    while n % t:
        t //= 2
    return t


def kernel(x_prompt, x_sample, cache_attn_k, cache_attn_v, cache_mem_k, cache_mem_v, state_conv, state_ssm,
           page_table, mem_prompt, ln_g, ln_b, w_in, conv_w, conv_b, dt_bias, a_log, d_skip, ssm_norm_w,
           lambda_params, attn_norm_w, w_out, w_cq, w_ckv, w_co, w_ffn_gu, w_ffn_down, w_router, w_exp_gu,
           w_exp_down):
    B, S, D = x_prompt.shape
    Bs = x_sample.shape[0]
    depth = w_in.shape[0]
    n_phys, page = cache_attn_k.shape[1], cache_attn_k.shape[2]
    n_pages = page_table.shape[1]
    past_len = n_pages * page
    n_mem = mem_prompt.shape[1]
    alpha = (2 * depth) ** 0.25
    assert S % SSD_CHUNK == 0 and x_sample.shape[1] == 1

    tm_p = _row_tile(S, 512)
    tm_s = _row_tile(Bs, 128)
    cos_p, sin_p = _rope_tables(jnp.arange(S, dtype=jnp.int32))
    cos_s, sin_s = _rope_tables(jnp.full((tm_s,), past_len, jnp.int32))

    hp = x_prompt.reshape(B * S, D)
    hs = x_sample.reshape(Bs, D)
    mem = mem_prompt.reshape(B * n_mem, D)
    cache_k = cache_attn_k.transpose(0, 1, 3, 4, 2)
    cache_v = cache_attn_v.reshape(depth, n_phys, page * ATT_HEADS, ATT_V_DIM)
    dh = D // MEM_HEADS
    mem_rows = lambda t: (t.reshape(depth, Bs, n_mem, MEM_HEADS, dh // LANES, LANES).swapaxes(3, 4)
                          .reshape(depth, Bs, n_mem * dh // LANES * MEM_HEADS, LANES))
    cmem_k, cmem_v = mem_rows(cache_mem_k), mem_rows(cache_mem_v)
    cstate = state_conv.reshape(depth, Bs, (CONV_WIDTH - 1) * CONV_DIM)
    pad_heads = lambda t: jnp.pad(t, [(0, 0)] * (t.ndim - 2) + [(0, 2 * ATT_HEADS - t.shape[-2]), (0, 0)])

    outs = {n: [] for n in ("cp", "sp", "ks", "vs", "cs")}
    kv_prompt = mem_kv_prompt = ssm_sample = None
    for l in range(depth):
        lam_init = 0.8 - 0.6 * math.exp(-0.3 * l)
        wl = w_in[l].astype(BF16)
        wdt = wl[:, 3 * ATT_COLS + SSM_INNER + CONV_DIM:]
        w_proj = (wl, wdt, wdt.T)
        sw = (conv_w[l], conv_b[l][None, :], dt_bias[l][None, :], dt_bias[l][:, None], a_log[l][None, :],
              a_log[l][:, None], d_skip[l][None, :], ssm_norm_w[l][None, :])
        lp = lambda_params[l]
        nw = attn_norm_w[l][None, :]
        wo = w_out[l].astype(BF16)
        wo_att, wo_ssm = wo[:ATT_COLS], wo[ATT_COLS:]
        wcq = w_cq[l].astype(BF16)
        wckv = w_ckv[l].astype(BF16)
        wco = w_co[l].astype(BF16)
        g0, g1, g2 = (ln_g[l, i][None, :] for i in range(3))
        b0, b1, b2 = (ln_b[l, i][None, :] for i in range(3))

        q, k_all, kb, v_all, vb, z, xbc, dt, dtT = _in_proj(hp, cos_p, sin_p, w_proj, tm_p, S // tm_p,
                                                            cache=(kv_prompt,))
        kv_prompt = (k_all, v_all)
        att = _attention(q, kb, vb, lp, nw, lam_init, B, S)
        y, conv_new, h_last = _ssd(xbc, z, dt, dtT, sw, B, S)
        hp = _proj_ln((att, y), (wo_att, wo_ssm), hp, g0, b0, alpha, _row_tile(S, 1024))
        outs["cp"].append(conv_new)
        outs["sp"].append(h_last.reshape(B, SSM_HEADS, SSM_HEAD_DIM, SSM_STATE))
        mk_all, mv_all, mk_b, mv_b = _mem_kv(mem, wckv, mem_kv_prompt, _row_tile(B * n_mem, 256))
        mem_kv_prompt = (mk_all, mv_all)
        hp = _cross_attention(hp, wcq, mk_b, mv_b, wco, g1, b1, alpha, B, S, n_mem, _row_tile(S, 1024))

        q, kf, _, vf, _, z, xbc, dt, _ = _in_proj(hs, cos_s, sin_s, w_proj, tm_s, 1)
        att = _decode_attention(page_table, l, q.astype(F32).reshape(Bs, 1, ATT_COLS), kf.reshape(Bs, 1, ATT_COLS),
                                pad_heads(vf.reshape(Bs, ATT_HEADS, ATT_V_DIM)), cache_k, cache_v, lp, nw, lam_init)
        att = att[:, :ATT_HEADS].reshape(Bs, ATT_COLS)
        y, conv_new, ssm_sample = _ssd_step(xbc, z, dt, cstate, state_ssm, l, sw, ssm_sample)
        hs = _proj_ln((att, y), (wo_att, wo_ssm), hs, g0, b0, alpha, tm_s)
        outs["ks"].append(kf.reshape(Bs, 1, 2 * ATT_HEADS, ATT_QK_DIM))
        outs["vs"].append(vf.reshape(Bs, 1, ATT_HEADS, ATT_V_DIM))
        outs["cs"].append(conv_new.reshape(Bs, CONV_WIDTH - 1, CONV_DIM))
        (qc,) = _matmul(hs, wcq, (F32,), tm_s, D)
        q_rows = pad_heads(qc.reshape(Bs, MEM_HEADS, dh // LANES, LANES).swapaxes(1, 2))
        oc = _cross_decode(q_rows.reshape(Bs, -1, LANES), cmem_k, cmem_v, l)
        oc = oc.reshape(Bs, dh // LANES, 2 * ATT_HEADS, LANES)[:, :, :MEM_HEADS].swapaxes(1, 2).reshape(Bs, D)
        hs = _proj_ln((oc,), (wco,), hs, g1, b1, alpha, tm_s)

        if l % 2 == 0:
            wgu = w_ffn_gu[l // 2].astype(BF16)
            wd = w_ffn_down[l // 2].astype(BF16)
            tf = wd.shape[0]
            hp = _ffn(hp, wgu, wd, g2, b2, alpha, tm_p, tf)
            hs = _ffn(hs, wgu, wd, g2, b2, alpha, tm_s, tf)
        else:
            wr = w_router[l // 2].astype(BF16)
            wgu = w_exp_gu[l // 2].astype(BF16)
            wd = w_exp_down[l // 2].astype(BF16)
            tf = wd.shape[1] // MOE_F_STEPS
            _, top_i, top_g = _router(hp, wr, tm_p)
            tm_e = _row_tile(2 * B * S, MOE_ROW_TILE)
            src, dest, tile_expert, n_used = _moe_plan(top_i, tm_e, 2 * B * S // tm_e + N_EXPERTS)
            y_rows = _moe_ffn(hp, src, tile_expert, n_used, wgu, wd, tm_e, tf)
            hp = _moe_combine(hp, y_rows, dest, top_g, g2, b2, alpha, tm_p)
            hs = _moe(hs, _router(hs, wr, tm_s)[0], wgu, wd, g2, b2, alpha, tm_s, tf)

    st = {n: jnp.stack(v) for n, v in outs.items()}
    k_prompt = kv_prompt[0].reshape(depth, B, 2 * ATT_HEADS, ATT_QK_DIM, S).transpose(0, 1, 4, 2, 3)
    v_prompt = kv_prompt[1].reshape(depth, B, S, ATT_HEADS, ATT_V_DIM)
    mem_shape = (depth, B, n_mem, dh // LANES, MEM_HEADS, LANES)
    mem_k_prompt, mem_v_prompt = (t.reshape(mem_shape).swapaxes(3, 4).reshape(depth, B, n_mem, MEM_HEADS, dh)
                                  for t in mem_kv_prompt)
    return (hp.reshape(B, S, D), hs.reshape(Bs, 1, D), k_prompt, v_prompt, mem_k_prompt, mem_v_prompt, st["cp"],
            st["sp"], st["ks"], st["vs"], st["cs"], ssm_sample)
```

```python
import functools
import math

import jax
import jax.numpy as jnp
from jax import lax
from jax.experimental import pallas as pl
from jax.experimental.pallas import tpu as pltpu

F32 = jnp.float32
BF16 = jnp.bfloat16

ATT_HEADS = 4
ATT_V_DIM = 128
ATT_QK_DIM = 64
ATT_COLS = 512
SSM_INNER = 512
SSM_HEAD_DIM = 64
SSM_HEADS = 8
SSM_GROUPS = 2
SSM_STATE = 128
CONV_WIDTH = 4
CONV_DIM = 1024
SSD_CHUNK = 128
MEM_HEADS = 4
N_EXPERTS = 8
ROPE_THETA = 10000.0
LN_EPS = 1e-5
NORM_EPS = 1e-5
QK_SCALE = ATT_QK_DIM ** -0.5

LANES = 128
MOE_ROW_TILE = 512
ROW_COPY_UNROLL = 8
MOE_F_STEPS = 1
ATT_HEADS_PER_STEP = 4
VMEM_LIMIT_BYTES = 56 * 1024 * 1024
NT_DIMS = (((1,), (1,)), ((), ()))


def _cparams(*sem):
    return pltpu.CompilerParams(dimension_semantics=sem, vmem_limit_bytes=VMEM_LIMIT_BYTES)


def _silu(x):
    return x / (1.0 + jnp.exp(-x))


def _softplus(x):
    return jnp.maximum(x, 0.0) + jnp.log1p(jnp.exp(-jnp.abs(x)))


def _layer_norm(y, g, b):
    mu = jnp.mean(y, axis=-1, keepdims=True)
    yc = y - mu
    var = jnp.mean(yc * yc, axis=-1, keepdims=True)
    return yc * lax.rsqrt(var + LN_EPS) * g + b


def _full(shape):
    return pl.BlockSpec(shape, lambda *_: (0,) * len(shape))


def _in_proj_kernel(x_ref, cos_ref, sin_ref, wq_ref, wk_ref, wv_ref, wz_ref, wxbc_ref, wdt_ref, wdtT_ref, *rest,
                    cache_layout):
    q_ref, kf_ref, kb_ref, vf_ref, vb_ref, z_ref, xbc_ref, dt_ref, dtT_ref = rest[-9:]
    tm = x_ref.shape[0]
    xb = x_ref[...].astype(BF16)
    cos = jnp.concatenate([cos_ref[...]] * 4, axis=1)
    sin = jnp.concatenate([sin_ref[...]] * 4, axis=1)
    lane = lax.broadcasted_iota(jnp.int32, (tm, ATT_COLS), 1)
    first_half = (lane & (ATT_QK_DIM - 1)) < (ATT_QK_DIM // 2)

    def rope(t):
        partner = jnp.where(first_half, pltpu.roll(t, ATT_COLS - ATT_QK_DIM // 2, 1),
                            pltpu.roll(t, ATT_QK_DIM // 2, 1))
        return t * cos + partner * sin

    q = rope(jnp.dot(xb, wq_ref[...], preferred_element_type=F32))
    q_ref[...] = (q * QK_SCALE).astype(BF16)
    k = rope(jnp.dot(xb, wk_ref[...], preferred_element_type=F32))
    kb_ref[...] = k.astype(BF16)
    v = jnp.dot(xb, wv_ref[...], preferred_element_type=F32)
    vb_ref[...] = v.astype(BF16)
    if cache_layout:
        n_prev = kf_ref.shape[0] - 1
        if n_prev:
            kf_ref[0:n_prev] = rest[0][...]
            vf_ref[0:n_prev] = rest[1][...]
        kf_ref[n_prev] = k.T
        for h in range(ATT_HEADS):
            vf_ref[n_prev, pl.ds(h, tm, stride=ATT_HEADS), :] = v[:, h * ATT_V_DIM:(h + 1) * ATT_V_DIM]
    else:
        kf_ref[...] = k
        vf_ref[...] = v
    z_ref[...] = jnp.dot(xb, wz_ref[...], preferred_element_type=F32)
    xbc_ref[...] = jnp.dot(xb, wxbc_ref[...], preferred_element_type=F32)
    dt_ref[...] = jnp.dot(xb, wdt_ref[...], preferred_element_type=F32)
    dtT_ref[...] = lax.dot_general(wdtT_ref[...], xb, NT_DIMS, preferred_element_type=F32)


def _in_proj(x, cos_t, sin_t, w, tm, n_pos_blocks, cache=None):
    T, D = x.shape
    w_all, wdt, wdtT = w
    cols = lambda n, j: pl.BlockSpec((D, n), lambda i: (0, j))
    row = lambda n: pl.BlockSpec((tm, n), lambda i: (i, 0))
    pos = pl.BlockSpec((tm, 128), lambda i: (i % n_pos_blocks, 0))
    kv_shape = jax.ShapeDtypeStruct((T, ATT_COLS), F32)
    k_shape, v_shape, k_spec, v_spec, prev, prev_specs = kv_shape, kv_shape, row(ATT_COLS), row(ATT_COLS), (), []
    if cache is not None:
        (prev_kv,) = cache
        S = n_pos_blocks * tm
        k_block = lambda n: pl.BlockSpec((n, None, ATT_COLS, tm), lambda i: (0, i // n_pos_blocks, 0, i % n_pos_blocks))
        v_block = lambda n: pl.BlockSpec((n, ATT_HEADS * tm, ATT_V_DIM), lambda i: (0, i, 0))
        n_prev = 0
        if prev_kv is not None:
            prev = tuple(prev_kv)
            n_prev = prev[0].shape[0]
            prev_specs = [k_block(n_prev), v_block(n_prev)]
        k_shape = jax.ShapeDtypeStruct((n_prev + 1, T // S, ATT_COLS, S), F32)
        v_shape = jax.ShapeDtypeStruct((n_prev + 1, ATT_HEADS * T, ATT_V_DIM), F32)
        k_spec, v_spec = k_block(n_prev + 1), v_block(n_prev + 1)
    out_shape = (
        jax.ShapeDtypeStruct((T, ATT_COLS), BF16),
        k_shape,
        jax.ShapeDtypeStruct((T, ATT_COLS), BF16),
        v_shape,
        jax.ShapeDtypeStruct((T, ATT_COLS), BF16),
        jax.ShapeDtypeStruct((T, SSM_INNER), F32),
        jax.ShapeDtypeStruct((T, CONV_DIM), F32),
        jax.ShapeDtypeStruct((T, SSM_HEADS), F32),
        jax.ShapeDtypeStruct((SSM_HEADS, T), F32),
    )
    out_specs = (row(ATT_COLS), k_spec, row(ATT_COLS), v_spec, row(ATT_COLS),
                 row(SSM_INNER), row(CONV_DIM), row(SSM_HEADS),
                 pl.BlockSpec((SSM_HEADS, tm), lambda i: (0, i)))
    return pl.pallas_call(
        functools.partial(_in_proj_kernel, cache_layout=cache is not None),
        grid=(T // tm,),
        in_specs=[row(D), pos, pos, cols(ATT_COLS, 0), cols(ATT_COLS, 1), cols(ATT_COLS, 2), cols(SSM_INNER, 3),
                  cols(CONV_DIM, (3 * ATT_COLS + SSM_INNER) // CONV_DIM), _full(wdt.shape), _full(wdtT.shape)] + prev_specs,
        out_specs=out_specs,
        out_shape=out_shape,
        compiler_params=_cparams("parallel"),
        name="in_proj",
    )(x, cos_t, sin_t, w_all, w_all, w_all, w_all, w_all, wdt, wdtT, *prev)


def _gated_group_norm(y, z, norm_w):
    y = y * _silu(z)
    half = SSM_INNER // SSM_GROUPS
    y2 = y * y
    ms0 = jnp.mean(y2[:, :half], axis=-1, keepdims=True)
    ms1 = jnp.mean(y2[:, half:], axis=-1, keepdims=True)
    lane = lax.broadcasted_iota(jnp.int32, y.shape, 1)
    scale = jnp.where(lane < half, lax.rsqrt(ms0 + NORM_EPS), lax.rsqrt(ms1 + NORM_EPS))
    return y * scale * norm_w


def _ssd_kernel(xbc_ref, z_ref, dt_ref, dtT_ref, convw_ref, convb_ref, dtb_ref, dtbT_ref, alog_ref, alogT_ref,
                dskip_ref, normw_ref, y_ref, conv_out_ref, state_out_ref, ext_ref, h_ref):
    Q = SSD_CHUNK
    c = pl.program_id(1)
    last = pl.num_programs(1) - 1

    @pl.when(c == 0)
    def _():
        ext_ref[...] = jnp.zeros(ext_ref.shape, F32)
        h_ref[...] = jnp.zeros(h_ref.shape, F32)

    cur = xbc_ref[...]
    before = ext_ref[...]
    row8 = lax.broadcasted_iota(jnp.int32, (8, CONV_DIM), 0)
    conv = convb_ref[...] + cur * convw_ref[CONV_WIDTH - 1:CONV_WIDTH, :]
    for s in range(1, CONV_WIDTH):
        shifted = pltpu.roll(cur, s, 0)
        first = jnp.where(row8 < s, pltpu.roll(before, s, 0), shifted[0:8, :])
        shifted = jnp.concatenate([first, shifted[8:, :]], axis=0)
        conv = conv + shifted * convw_ref[CONV_WIDTH - 1 - s:CONV_WIDTH - s, :]
    ext_ref[...] = cur[Q - 8:Q, :]
    conv = _silu(conv)
    xs = conv[:, :SSM_INNER]
    bmat = conv[:, SSM_INNER:SSM_INNER + SSM_GROUPS * SSM_STATE]
    cmat = conv[:, SSM_INNER + SSM_GROUPS * SSM_STATE:]

    dt_col = _softplus(dt_ref[...] + dtb_ref[...])
    a_col = dt_col * (-jnp.exp(alog_ref[...]))
    a_row = _softplus(dtT_ref[...] + dtbT_ref[...]) * (-jnp.exp(alogT_ref[...]))
    ri = lax.broadcasted_iota(jnp.int32, (Q, Q), 0)
    ci = lax.broadcasted_iota(jnp.int32, (Q, Q), 1)
    causal = ci <= ri
    tril = causal.astype(F32)
    triu = (ri <= ci).astype(F32)
    acs_col = jnp.dot(tril, a_col, preferred_element_type=F32, precision=lax.Precision.HIGHEST)
    acs_row = jnp.dot(a_row, triu, preferred_element_type=F32, precision=lax.Precision.HIGHEST)

    lo = ci < SSM_HEAD_DIM
    top = ri < SSM_HEAD_DIM
    pairs = []
    b_groups = [bmat[:, g * SSM_STATE:(g + 1) * SSM_STATE].astype(BF16) for g in range(SSM_GROUPS)]
    c_groups = [cmat[:, g * SSM_STATE:(g + 1) * SSM_STATE].astype(BF16) for g in range(SSM_GROUPS)]
    cb_groups = [lax.dot_general(c_g, b_g, NT_DIMS, preferred_element_type=F32)
                 for c_g, b_g in zip(c_groups, b_groups)]
    for k in range(SSM_HEADS // 2):
        h0, h1 = 2 * k, 2 * k + 1
        g = k // (SSM_HEADS // 2 // SSM_GROUPS)
        b_g, c_g, cb = b_groups[g], c_groups[g], cb_groups[g]
        xs_pair = xs[:, 2 * SSM_HEAD_DIM * k:2 * SSM_HEAD_DIM * (k + 1)]
        dtc = jnp.where(lo, dt_col[:, h0:h0 + 1], dt_col[:, h1:h1 + 1])
        xdt = xs_pair * dtc
        acol = jnp.where(lo, acs_col[:, h0:h0 + 1], acs_col[:, h1:h1 + 1])
        alast0 = acs_row[h0:h0 + 1, Q - 1:Q]
        alast1 = acs_row[h1:h1 + 1, Q - 1:Q]
        alast = jnp.where(lo[0:1, :], alast0, alast1)
        xw = xdt * jnp.exp(alast - acol)
        y_pair = jnp.where(lo, dskip_ref[0:1, h0:h0 + 1], dskip_ref[0:1, h1:h1 + 1]) * xs_pair
        for h, sel in ((h0, lo), (h1, jnp.logical_not(lo))):
            seg = acs_col[:, h:h + 1] - acs_row[h:h + 1, :]
            decay = jnp.exp(jnp.where(causal, seg, -jnp.inf))
            m = (cb * decay).astype(BF16)
            y_pair = y_pair + jnp.dot(m, jnp.where(sel, xdt, 0.0).astype(BF16), preferred_element_type=F32)
        states = jnp.dot(xw.T.astype(BF16), b_g, preferred_element_type=F32)
        h_prev = h_ref[k]
        y_off = lax.dot_general(c_g, h_prev.astype(BF16), NT_DIMS, preferred_element_type=F32)
        y_pair = y_pair + y_off * jnp.exp(acol)
        dec = jnp.where(top[:, 0:1], jnp.exp(alast0), jnp.exp(alast1))
        h_ref[k] = h_prev * dec + states
        pairs.append(y_pair)
    y = jnp.concatenate(pairs, axis=1)
    y_ref[...] = _gated_group_norm(y, z_ref[...], normw_ref[...]).astype(BF16)

    @pl.when(c == last)
    def _():
        conv_out_ref[0] = xbc_ref[Q - (CONV_WIDTH - 1):Q, :]
        state_out_ref[0] = h_ref[...]


def _ssd(xbc, z, dt, dtT, sw, B, S):
    convw, convb, dtb, dtbT, alog, alogT, dskip, normw = sw
    nc = S // SSD_CHUNK
    Q = SSD_CHUNK
    rowb = lambda n: pl.BlockSpec((Q, n), lambda b, c: (b * nc + c, 0))
    npair = SSM_HEADS // 2
    return pl.pallas_call(
        _ssd_kernel,
        grid=(B, nc),
        in_specs=[rowb(CONV_DIM), rowb(SSM_INNER), rowb(SSM_HEADS),
                  pl.BlockSpec((SSM_HEADS, Q), lambda b, c: (0, b * nc + c)),
                  _full(convw.shape), _full(convb.shape), _full(dtb.shape), _full(dtbT.shape),
                  _full(alog.shape), _full(alogT.shape), _full(dskip.shape), _full(normw.shape)],
        out_specs=(rowb(SSM_INNER),
                   pl.BlockSpec((1, CONV_WIDTH - 1, CONV_DIM), lambda b, c: (b, 0, 0)),
                   pl.BlockSpec((1, npair, 2 * SSM_HEAD_DIM, SSM_STATE), lambda b, c: (b, 0, 0, 0))),
        out_shape=(jax.ShapeDtypeStruct((B * S, SSM_INNER), BF16),
                   jax.ShapeDtypeStruct((B, CONV_WIDTH - 1, CONV_DIM), F32),
                   jax.ShapeDtypeStruct((B, npair, 2 * SSM_HEAD_DIM, SSM_STATE), F32)),
        scratch_shapes=[pltpu.VMEM((8, CONV_DIM), F32),
                        pltpu.VMEM((npair, 2 * SSM_HEAD_DIM, SSM_STATE), F32)],
        compiler_params=_cparams("parallel", "arbitrary"),
        name="ssd_scan",
    )(xbc, z, dt, dtT, convw, convb, dtb, dtbT, alog, alogT, dskip, normw)


def _ssd_step_kernel(xbc_ref, z_ref, dt_ref, cstate_ref, hstate_ref, convw_ref, convb_ref, dtb_ref, alog_ref,
                     dskip_ref, normw_ref, *rest):
    y_ref, cout_ref, hout_ref = rest[-3:]
    n_prev = hout_ref.shape[0] - 1
    if n_prev:
        hout_ref[0:n_prev] = rest[0][...]
    bt = xbc_ref.shape[0]
    P, N = SSM_HEAD_DIM, SSM_STATE
    new = xbc_ref[...]
    conv = convb_ref[...] + new * convw_ref[CONV_WIDTH - 1:CONV_WIDTH, :]
    for i in range(CONV_WIDTH - 1):
        conv = conv + cstate_ref[:, i * CONV_DIM:(i + 1) * CONV_DIM] * convw_ref[i:i + 1, :]
    cout_ref[:, 0:(CONV_WIDTH - 2) * CONV_DIM] = cstate_ref[:, CONV_DIM:(CONV_WIDTH - 1) * CONV_DIM]
    cout_ref[:, (CONV_WIDTH - 2) * CONV_DIM:] = new
    conv = _silu(conv)
    xs = conv[:, :SSM_INNER]
    bmat = conv[:, SSM_INNER:SSM_INNER + SSM_GROUPS * N]
    cmat = conv[:, SSM_INNER + SSM_GROUPS * N:]
    dt = _softplus(dt_ref[...] + dtb_ref[...])
    dec = jnp.exp(dt * (-jnp.exp(alog_ref[...])))
    H = SSM_HEADS
    HP = H * P
    exact = lax.Precision.HIGHEST
    eye_n = (lax.broadcasted_iota(jnp.int32, (N, N), 0) == lax.broadcasted_iota(jnp.int32, (N, N), 1)).astype(F32)
    x_t = jnp.concatenate(
        [lax.dot_general(eye_n, xs[:, c * N:(c + 1) * N], NT_DIMS, preferred_element_type=F32, precision=exact)
         for c in range(HP // N)], axis=0)
    rep = (lax.broadcasted_iota(jnp.int32, (HP, H), 0) // P == lax.broadcasted_iota(jnp.int32, (HP, H), 1)).astype(F32)
    per_head = jnp.concatenate([dt, dec, jnp.broadcast_to(dskip_ref[...], (8, H))], axis=0)
    cols = lax.dot_general(rep, per_head, NT_DIMS, preferred_element_type=F32, precision=exact)
    d_col = cols[:, 2 * bt:2 * bt + 1]
    first_group = lax.broadcasted_iota(jnp.int32, (HP, 1), 0) < HP // SSM_GROUPS
    lane_b = lax.broadcasted_iota(jnp.int32, (HP, bt), 1)
    y_t = jnp.zeros((HP, bt), F32)
    for b in range(bt):
        x_col = x_t[:, b:b + 1]
        b_sel = jnp.where(first_group, bmat[b:b + 1, 0:N], bmat[b:b + 1, N:2 * N])
        h_new = (hstate_ref[b].reshape(HP, N) * cols[:, bt + b:bt + b + 1]
                 + (x_col * cols[:, b:b + 1]) * b_sel)
        hout_ref[n_prev, b] = h_new.reshape(H, P, N)
        c2 = jnp.concatenate([cmat[b:b + 1, 0:N], cmat[b:b + 1, N:2 * N], jnp.zeros((6, N), F32)], axis=0)
        yb = lax.dot_general(h_new.astype(BF16), c2.astype(BF16), NT_DIMS, preferred_element_type=F32)
        y_col = jnp.where(first_group, yb[:, 0:1], yb[:, 1:2]) + d_col * x_col
        y_t = jnp.where(lane_b == b, y_col, y_t)
    eye_b = (lax.broadcasted_iota(jnp.int32, (bt, bt), 0) == lax.broadcasted_iota(jnp.int32, (bt, bt), 1)).astype(F32)
    y = lax.dot_general(eye_b, y_t, NT_DIMS, preferred_element_type=F32, precision=exact)
    y_ref[...] = _gated_group_norm(y, z_ref[...], normw_ref[...]).astype(BF16)


def _ssd_step(xbc, z, dt, cstate, hstate, layer, sw, h_prev=None, bt=8):
    convw, convb, dtb, _, alog, _, dskip, normw = sw
    Bs = xbc.shape[0]
    cw = (CONV_WIDTH - 1) * CONV_DIM
    rowb = lambda n: pl.BlockSpec((bt, n), lambda i: (i, 0))
    hshape = (bt, SSM_HEADS, SSM_HEAD_DIM, SSM_STATE)
    layers = lambda n: pl.BlockSpec((n,) + hshape, lambda i: (0, i, 0, 0, 0))
    prev = () if h_prev is None else (h_prev,)
    n_prev = 0 if h_prev is None else h_prev.shape[0]
    return pl.pallas_call(
        _ssd_step_kernel,
        grid=(Bs // bt,),
        in_specs=[rowb(CONV_DIM), rowb(SSM_INNER), rowb(SSM_HEADS),
                  pl.BlockSpec((None, bt, cw), lambda i: (layer, i, 0)),
                  pl.BlockSpec((None,) + hshape, lambda i: (layer, i, 0, 0, 0)),
                  _full(convw.shape), _full(convb.shape), _full(dtb.shape), _full(alog.shape),
                  _full(dskip.shape), _full(normw.shape)] + [layers(n_prev)] * len(prev),
        out_specs=(rowb(SSM_INNER), rowb(cw), layers(n_prev + 1)),
        out_shape=(jax.ShapeDtypeStruct((Bs, SSM_INNER), BF16),
                   jax.ShapeDtypeStruct((Bs, cw), F32),
                   jax.ShapeDtypeStruct((n_prev + 1,) + hstate.shape[1:], F32)),
        compiler_params=_cparams("parallel"),
        name="ssd_step",
    )(xbc, z, dt, cstate, hstate, convw, convb, dtb, alog, dskip, normw, *prev)


def _lambda_value(lp, lam_init):
    d1 = jnp.sum(lp[0:1, :] * lp[1:2, :], axis=1, keepdims=True)
    d2 = jnp.sum(lp[2:3, :] * lp[3:4, :], axis=1, keepdims=True)
    return jnp.exp(d1) - jnp.exp(d2) + lam_init


def _head_norm(o, nw, lam_init):
    return o * lax.rsqrt(jnp.mean(o * o, axis=-1, keepdims=True) + NORM_EPS) * nw * (1.0 - lam_init)


def _attn_kernel(lp_ref, nw_ref, q_ref, k_ref, v_ref, o_ref, m_ref, l_ref, acc_ref, *, lam_init):
    tq = q_ref.shape[0]
    n_heads = q_ref.shape[1] // ATT_V_DIM
    qi = pl.program_id(2)
    lane = lax.broadcasted_iota(jnp.int32, (tq, ATT_V_DIM), 1)
    q_maps = []
    for h in range(n_heads):
        q = q_ref[:, h * ATT_V_DIM:(h + 1) * ATT_V_DIM]
        zero = jnp.zeros_like(q)
        q_maps += [jnp.where(lane < ATT_QK_DIM, q, zero), jnp.where(lane >= ATT_QK_DIM, q, zero)]
    m_ref[...] = jnp.full(m_ref.shape, -jnp.inf, F32)
    l_ref[...] = jnp.zeros(l_ref.shape, F32)
    acc_ref[...] = jnp.zeros(acc_ref.shape, F32)
    below_diag = (lax.broadcasted_iota(jnp.int32, (tq, tq), 1) <= lax.broadcasted_iota(jnp.int32, (tq, tq), 0))

    def block(j, diagonal):
        start = pl.multiple_of(j * tq, tq)
        for i in range(2 * n_heads):
            cols = slice((i // 2) * ATT_V_DIM, (i // 2 + 1) * ATT_V_DIM)
            k = k_ref[pl.ds(start, tq), cols]
            v = v_ref[pl.ds(start, tq), cols]
            s = lax.dot_general(q_maps[i], k, NT_DIMS, preferred_element_type=F32)
            if diagonal:
                s = jnp.where(below_diag, s, -jnp.inf)
            chunks = [s[:, c * LANES:(c + 1) * LANES] for c in range(tq // LANES)]
            top = functools.reduce(jnp.maximum, chunks)
            m_old = m_ref[i]
            m_new = jnp.maximum(m_old, jnp.max(top, axis=1, keepdims=True))
            alpha = jnp.exp(m_old - m_new)
            ps = [jnp.exp(c - m_new) for c in chunks]
            l_ref[i] = alpha * l_ref[i] + functools.reduce(jnp.add, ps)
            p = jnp.concatenate(ps, axis=1).astype(BF16)
            acc_ref[i] = alpha * acc_ref[i] + jnp.dot(p, v, preferred_element_type=F32)
            m_ref[i] = m_new

    def full_block(j, carry):
        block(j, False)
        return carry

    lax.fori_loop(0, qi, full_block, 0)
    block(qi, True)
    lam = _lambda_value(lp_ref[...], lam_init)
    for h in range(n_heads):
        l1 = jnp.sum(l_ref[2 * h], axis=1, keepdims=True)
        l2 = jnp.sum(l_ref[2 * h + 1], axis=1, keepdims=True)
        o = acc_ref[2 * h] / l1 - lam * (acc_ref[2 * h + 1] / l2)
        o_ref[:, h * ATT_V_DIM:(h + 1) * ATT_V_DIM] = _head_norm(o, nw_ref[...], lam_init).astype(BF16)


def _attention(q, k, v, lp, nw, lam_init, B, S):
    tq = min(512, S)
    nq = S // tq
    hg = ATT_HEADS_PER_STEP
    qspec = pl.BlockSpec((tq, hg * ATT_V_DIM), lambda b, h, i: (b * nq + i, h))
    kspec = pl.BlockSpec((S, hg * ATT_V_DIM), lambda b, h, i: (b, h))
    return pl.pallas_call(
        functools.partial(_attn_kernel, lam_init=lam_init),
        grid=(B, ATT_HEADS // hg, nq),
        in_specs=[_full(lp.shape), _full(nw.shape), qspec, kspec, kspec],
        out_specs=qspec,
        out_shape=jax.ShapeDtypeStruct((B * S, ATT_COLS), BF16),
        scratch_shapes=[pltpu.VMEM((2 * hg, tq, LANES), F32), pltpu.VMEM((2 * hg, tq, LANES), F32),
                        pltpu.VMEM((2 * hg, tq, ATT_V_DIM), F32)],
        compiler_params=_cparams("parallel", "parallel", "arbitrary"),
        name="diff_attention",
    )(lp, nw, q, k, v)


def _decode_attn_kernel(pt_ref, lp_ref, nw_ref, q_ref, kn_ref, vn_ref, *refs, n_pages, lam_init):
    o_ref = refs[-1]
    for sq in range(q_ref.shape[0]):
        pages = refs[2 * n_pages * sq:2 * n_pages * (sq + 1)]
        _decode_one(lp_ref, nw_ref, q_ref.at[sq], kn_ref.at[sq], vn_ref.at[sq], pages[:n_pages], pages[n_pages:],
                    o_ref.at[sq], lam_init)


def _decode_one(lp_ref, nw_ref, q_ref, kn_ref, vn_ref, k_refs, v_refs, o_ref, lam_init):
    R = 2 * ATT_HEADS
    page = k_refs[0].shape[2]
    r = lax.broadcasted_iota(jnp.int32, (R, ATT_COLS), 0)
    grp = lax.broadcasted_iota(jnp.int32, (R, ATT_COLS), 1) // ATT_QK_DIM
    target = jnp.where(r < ATT_HEADS, 2 * r, 2 * (r - ATT_HEADS) + 1)
    qf = jnp.where(grp == target, q_ref[...].astype(BF16).astype(F32), 0.0)
    qt = qf.astype(BF16)
    s = [jnp.dot(qt, k_ref[...].reshape(ATT_COLS, page).astype(BF16), preferred_element_type=F32)
         for k_ref in k_refs]
    s_new = jnp.sum(qf * kn_ref[...].astype(BF16).astype(F32), axis=1, keepdims=True)
    m = s_new
    for sj in s:
        m = jnp.maximum(m, jnp.max(sj, axis=1, keepdims=True))
    p = [jnp.exp(sj - m) for sj in s]
    p_new = jnp.exp(s_new - m)
    l = p_new
    for pj in p:
        l = l + jnp.sum(pj, axis=1, keepdims=True)
    inv_l = 1.0 / l
    lam = _lambda_value(lp_ref[...], lam_init)
    diff = lambda t: t - lam * pltpu.roll(t, ATT_HEADS, 0)
    a = [diff(pj * inv_l).astype(BF16) for pj in p]
    a_new = diff(jnp.broadcast_to(p_new * inv_l, (R, 128)))[:, 0:1].astype(BF16).astype(F32)
    row = lax.broadcasted_iota(jnp.int32, (R, ATT_V_DIM), 0)
    out = a_new * vn_ref[...].astype(BF16).astype(F32)
    for h in range(ATT_HEADS):
        acc = jnp.zeros((R, ATT_V_DIM), F32)
        for aj, v_ref in zip(a, v_refs):
            v_h = v_ref[pl.ds(h, page, stride=ATT_HEADS), :].astype(BF16)
            acc = acc + jnp.dot(aj, v_h, preferred_element_type=F32)
        out = out + jnp.where(row == h, acc, 0.0)
    o_ref[...] = _head_norm(out, nw_ref[...], lam_init)


def _decode_attention(pt, layer, q, k_new, v_new, cache_k, cache_v, lp, nw, lam_init, bt=2):
    Bs, n_pages = pt.shape
    R = 2 * ATT_HEADS
    qspec = pl.BlockSpec((bt, 1, ATT_COLS), lambda b, pt: (b, 0, 0))
    vspec = pl.BlockSpec((bt, R, ATT_V_DIM), lambda b, pt: (b, 0, 0))
    pages = []
    for sq in range(bt):
        pages += [pl.BlockSpec((None, None) + cache_k.shape[2:],
                               lambda b, pt, j=j, sq=sq: (layer, pt[b * bt + sq, j], 0, 0, 0)) for j in range(n_pages)]
        pages += [pl.BlockSpec((None, None) + cache_v.shape[2:],
                               lambda b, pt, j=j, sq=sq: (layer, pt[b * bt + sq, j], 0, 0)) for j in range(n_pages)]
    grid_spec = pltpu.PrefetchScalarGridSpec(
        num_scalar_prefetch=1,
        grid=(Bs // bt,),
        in_specs=[pl.BlockSpec(lp.shape, lambda b, pt: (0, 0)), pl.BlockSpec(nw.shape, lambda b, pt: (0, 0)),
                  qspec, qspec, vspec] + pages,
        out_specs=vspec,
    )
    return pl.pallas_call(
        functools.partial(_decode_attn_kernel, n_pages=n_pages, lam_init=lam_init),
        grid_spec=grid_spec,
        out_shape=jax.ShapeDtypeStruct((Bs, R, ATT_V_DIM), F32),
        compiler_params=_cparams("parallel"),
        name="decode_attention",
    )(pt, lp, nw, q, k_new, v_new, *(([cache_k] * n_pages + [cache_v] * n_pages) * bt))


def _proj_ln_kernel(*refs, n_in, alpha):
    a_refs = refs[:n_in]
    w_refs = refs[n_in:2 * n_in]
    x_ref, g_ref, b_ref, o_ref = refs[2 * n_in:]
    acc = alpha * x_ref[...]
    for a_ref, w_ref in zip(a_refs, w_refs):
        acc = acc + jnp.dot(a_ref[...].astype(BF16), w_ref[...], preferred_element_type=F32)
    o_ref[...] = _layer_norm(acc, g_ref[...], b_ref[...])


def _proj_ln(acts, ws, x, g, b, alpha, tm):
    T, D = x.shape
    row = lambda n: pl.BlockSpec((tm, n), lambda i: (i, 0))
    return pl.pallas_call(
        functools.partial(_proj_ln_kernel, n_in=len(acts), alpha=alpha),
        grid=(T // tm,),
        in_specs=[row(a.shape[1]) for a in acts] + [_full(w.shape) for w in ws] + [row(D), _full(g.shape), _full(b.shape)],
        out_specs=row(D),
        out_shape=jax.ShapeDtypeStruct((T, D), F32),
        compiler_params=_cparams("parallel"),
        name="proj_ln",
    )(*acts, *ws, x, g, b)


def _matmul_kernel(x_ref, w_ref, *o_refs):
    y = jnp.dot(x_ref[...].astype(BF16), w_ref[...], preferred_element_type=F32)
    for o_ref in o_refs:
        o_ref[...] = y.astype(o_ref.dtype)


def _matmul(x, w, out_dtypes, tm, tn):
    M, K = x.shape
    N = w.shape[1]
    ospec = pl.BlockSpec((tm, tn), lambda i, j: (i, j))
    return pl.pallas_call(
        _matmul_kernel,
        grid=(M // tm, N // tn),
        in_specs=[pl.BlockSpec((tm, K), lambda i, j: (i, 0)), pl.BlockSpec((K, tn), lambda i, j: (0, j))],
        out_specs=tuple(ospec for _ in out_dtypes),
        out_shape=tuple(jax.ShapeDtypeStruct((M, N), d) for d in out_dtypes),
        compiler_params=_cparams("parallel", "parallel"),
        name="matmul",
    )(x, w)


def _mem_kv_kernel(x_ref, w_ref, *rest):
    mk_ref, mv_ref, mkb_ref, mvb_ref = rest[-4:]
    tm = x_ref.shape[0]
    D = w_ref.shape[1] // 2
    dh = D // MEM_HEADS
    halves = dh // LANES
    y = jnp.dot(x_ref[...].astype(BF16), w_ref[...], preferred_element_type=F32)
    mkb_ref[...] = y[:, :D].astype(BF16)
    mvb_ref[...] = y[:, D:].astype(BF16)
    n_prev = mk_ref.shape[0] - 1
    if n_prev:
        mk_ref[0:n_prev] = rest[0][...]
        mv_ref[0:n_prev] = rest[1][...]
    for out_ref, base in ((mk_ref, 0), (mv_ref, D)):
        for h in range(MEM_HEADS):
            for c in range(halves):
                col = base + h * dh + c * LANES
                out_ref[n_prev, pl.ds(c * MEM_HEADS + h, tm, stride=halves * MEM_HEADS), :] = y[:, col:col + LANES]


def _mem_kv(mem, w, prev, tm):
    M, D = mem.shape
    rows_per_token = D // LANES
    rows = lambda n: pl.BlockSpec((n, tm * rows_per_token, LANES), lambda i: (0, i, 0))
    flat = pl.BlockSpec((tm, D), lambda i: (i, 0))
    prev = () if prev is None else tuple(prev)
    n_prev = prev[0].shape[0] if prev else 0
    rows_shape = jax.ShapeDtypeStruct((n_prev + 1, M * rows_per_token, LANES), F32)
    return pl.pallas_call(
        _mem_kv_kernel,
        grid=(M // tm,),
        in_specs=[flat, _full(w.shape)] + [rows(n_prev)] * len(prev),
        out_specs=(rows(n_prev + 1), rows(n_prev + 1), flat, flat),
        out_shape=(rows_shape, rows_shape, jax.ShapeDtypeStruct((M, D), BF16), jax.ShapeDtypeStruct((M, D), BF16)),
        compiler_params=_cparams("parallel"),
        name="mem_kv",
    )(mem, w, *prev)


def _softmax_rows(s):
    m = jnp.max(s, axis=1, keepdims=True)
    p = jnp.exp(s - m)
    return p / jnp.sum(p, axis=1, keepdims=True)


def _cross_kernel(x_ref, wq_ref, mk_ref, mv_ref, wo_ref, g_ref, b_ref, o_ref, *, alpha):
    x = x_ref[...]
    D = x.shape[1]
    dh = D // MEM_HEADS
    q = jnp.dot(x.astype(BF16), wq_ref[...], preferred_element_type=F32).astype(BF16)
    outs = []
    for h in range(MEM_HEADS):
        sl = slice(h * dh, (h + 1) * dh)
        s = lax.dot_general(q[:, sl], mk_ref[:, sl], NT_DIMS, preferred_element_type=F32) * (dh ** -0.5)
        outs.append(jnp.dot(_softmax_rows(s).astype(BF16), mv_ref[:, sl], preferred_element_type=F32).astype(BF16))
    o = jnp.concatenate(outs, axis=1)
    y = alpha * x + jnp.dot(o, wo_ref[...], preferred_element_type=F32)
    o_ref[...] = _layer_norm(y, g_ref[...], b_ref[...])


def _cross_attention(x, wq, mk, mv, wo, g, b, alpha, B, S, n_mem, tm):
    T, D = x.shape
    nt = S // tm
    row = pl.BlockSpec((tm, D), lambda bi, i: (bi * nt + i, 0))
    mem = pl.BlockSpec((n_mem, D), lambda bi, i: (bi, 0))
    return pl.pallas_call(
        functools.partial(_cross_kernel, alpha=alpha),
        grid=(B, nt),
        in_specs=[row, _full(wq.shape), mem, mem, _full(wo.shape), _full(g.shape), _full(b.shape)],
        out_specs=row,
        out_shape=jax.ShapeDtypeStruct((T, D), F32),
        compiler_params=_cparams("parallel", "parallel"),
        name="cross_attention",
    )(x, wq, mk, mv, wo, g, b)


def _cross_decode_kernel(q_ref, mk_ref, mv_ref, o_ref):
    R = q_ref.shape[1] // 2
    rows_kv = mk_ref.shape[1]
    dh = 2 * LANES
    own = (lax.broadcasted_iota(jnp.int32, (R, rows_kv), 1) & (R - 1)) == lax.broadcasted_iota(jnp.int32, (R, rows_kv), 0)
    for b in range(q_ref.shape[0]):
        part = lax.dot_general(q_ref[b].astype(BF16), mk_ref[b].astype(BF16), NT_DIMS,
                               preferred_element_type=F32)
        s = (part[:R] + pltpu.roll(part[R:], rows_kv - MEM_HEADS, 1)) * (dh ** -0.5)
        p = _softmax_rows(jnp.where(own, s, -jnp.inf))
        a = jnp.concatenate([p, pltpu.roll(p, MEM_HEADS, 1)], axis=0).astype(BF16)
        o_ref[b] = jnp.dot(a, mv_ref[b].astype(BF16), preferred_element_type=F32)


def _cross_decode(q, mem_k, mem_v, layer, bt=8):
    Bs, R2, _ = q.shape
    tok = pl.BlockSpec((bt, R2, LANES), lambda b: (b, 0, 0))
    mem = pl.BlockSpec((None, bt) + mem_k.shape[2:], lambda b: (layer, b, 0, 0))
    return pl.pallas_call(
        _cross_decode_kernel,
        grid=(Bs // bt,),
        in_specs=[tok, mem, mem],
        out_specs=tok,
        out_shape=jax.ShapeDtypeStruct((Bs, R2, LANES), F32),
        compiler_params=_cparams("parallel"),
        name="cross_decode",
    )(q, mem_k, mem_v)


def _ffn_kernel(x_ref, wg_ref, wu_ref, wd_ref, g_ref, b_ref, o_ref, acc_ref, *, alpha):
    f = pl.program_id(1)
    x = x_ref[...]
    xb = x.astype(BF16)

    @pl.when(f == 0)
    def _():
        acc_ref[...] = alpha * x

    gate = jnp.dot(xb, wg_ref[...], preferred_element_type=F32)
    up = jnp.dot(xb, wu_ref[...], preferred_element_type=F32)
    acc_ref[...] += jnp.dot((_silu(gate) * up).astype(BF16), wd_ref[...], preferred_element_type=F32)

    @pl.when(f == pl.num_programs(1) - 1)
    def _():
        o_ref[...] = _layer_norm(acc_ref[...], g_ref[...], b_ref[...])


def _ffn(x, w_gu, w_down, g, b, alpha, tm, tf):
    T, D = x.shape
    F = w_down.shape[0]
    nf = F // tf
    row = pl.BlockSpec((tm, D), lambda i, f: (i, 0))
    mode = dict(pipeline_mode=pl.Buffered(1)) if nf == 1 else {}
    return pl.pallas_call(
        functools.partial(_ffn_kernel, alpha=alpha),
        grid=(T // tm, nf),
        in_specs=[row, pl.BlockSpec((D, tf), lambda i, f: (0, f), **mode),
                  pl.BlockSpec((D, tf), lambda i, f: (0, nf + f), **mode),
                  pl.BlockSpec((tf, D), lambda i, f: (f, 0), **mode), _full(g.shape), _full(b.shape)],
        out_specs=row,
        out_shape=jax.ShapeDtypeStruct((T, D), F32),
        scratch_shapes=[pltpu.VMEM((tm, D), F32)],
        compiler_params=_cparams("parallel", "arbitrary"),
        name="ffn",
    )(x, w_gu, w_gu, w_down, g, b)


def _router_kernel(x_ref, wr_ref, gate_ref, top_i_ref, top_g_ref):
    logits = jnp.dot(x_ref[...].astype(BF16), wr_ref[...], preferred_element_type=F32)
    lane = lax.broadcasted_iota(jnp.int32, logits.shape, 1).astype(F32)
    none = float(N_EXPERTS)
    m1 = jnp.max(logits, axis=1, keepdims=True)
    i1 = jnp.min(jnp.where(logits == m1, lane, none), axis=1, keepdims=True)
    rest = jnp.where(lane == i1, -jnp.inf, logits)
    m2 = jnp.max(rest, axis=1, keepdims=True)
    i2 = jnp.min(jnp.where(rest == m2, lane, none), axis=1, keepdims=True)
    e = jnp.exp(m2 - m1)
    den = 1.0 + e
    gate_ref[...] = jnp.where(lane == i1, 1.0 / den, 0.0) + jnp.where(lane == i2, e / den, 0.0)
    first = lax.broadcasted_iota(jnp.int32, top_i_ref.shape, 1) == 0
    top_i_ref[...] = jnp.where(first, i1, i2).astype(jnp.int32)
    top_g_ref[...] = jnp.where(first, 1.0 / den, e / den)


def _router(x, wr, tm):
    T, D = x.shape
    top = pl.BlockSpec((tm, 2), lambda i: (i, 0))
    return pl.pallas_call(
        _router_kernel,
        grid=(T // tm,),
        in_specs=[pl.BlockSpec((tm, D), lambda i: (i, 0)), _full(wr.shape)],
        out_specs=(pl.BlockSpec((tm, N_EXPERTS), lambda i: (i, 0)), top, top),
        out_shape=(jax.ShapeDtypeStruct((T, N_EXPERTS), F32), jax.ShapeDtypeStruct((T, 2), jnp.int32),
                   jax.ShapeDtypeStruct((T, 2), F32)),
        compiler_params=_cparams("parallel"),
        name="router",
    )(x, wr)


def _moe_plan(top_i, tm_e, n_tiles):
    T = top_i.shape[0]
    chosen = (top_i[:, :, None] == jnp.arange(N_EXPERTS, dtype=jnp.int32)[None, None, :]).any(axis=1)
    chosen = chosen.astype(jnp.int32)
    count = chosen.sum(axis=0)
    rank = jnp.cumsum(chosen, axis=0) - chosen
    tiles = (count + tm_e - 1) // tm_e
    tile_end = jnp.cumsum(tiles)
    row_start = (tile_end - tiles) * tm_e
    dest = jnp.take_along_axis(row_start[None, :] + rank, top_i, axis=1).astype(jnp.int32)
    token = jnp.broadcast_to(jnp.arange(T, dtype=jnp.int32)[:, None], (T, 2))
    src = jnp.zeros((n_tiles * tm_e,), jnp.int32).at[dest.reshape(-1)].set(token.reshape(-1))
    n_used = tile_end[-1]
    tile = jnp.arange(n_tiles, dtype=jnp.int32)
    tile_expert = (tile_end[None, :] <= jnp.minimum(tile, n_used - 1)[:, None]).sum(axis=1).astype(jnp.int32)
    return src, dest, tile_expert, n_used.reshape(1).astype(jnp.int32)


def _row_copy(src_hbm, src_row, dst_ref, dst_row, sem):
    return pltpu.make_async_copy(src_hbm.at[pl.ds(src_row, 1), :], dst_ref.at[pl.ds(dst_row, 1), :], sem)


def _moe_ffn_kernel(te_ref, nv_ref, src_ref, x_hbm, wg_ref, wu_ref, wd_ref, y_ref, xrows_ref, xb_ref, sem):
    tm_e = xrows_ref.shape[1]
    i = pl.program_id(0)
    f = pl.program_id(1)
    n_used = nv_ref[0]
    slot = lax.rem(i, 2)

    def start_gather(tile, slot):
        def body(r8, carry):
            for u in range(ROW_COPY_UNROLL):
                r = r8 * ROW_COPY_UNROLL + u
                _row_copy(x_hbm, src_ref[tile * tm_e + r], xrows_ref.at[slot], r, sem.at[slot]).start()
            return carry
        lax.fori_loop(0, tm_e // ROW_COPY_UNROLL, body, 0)

    @pl.when(f == 0)
    def _():
        @pl.when(i == 0)
        def _():
            start_gather(0, 0)

        @pl.when(i < n_used)
        def _():
            pltpu.make_async_copy(x_hbm.at[pl.ds(0, tm_e), :], xrows_ref.at[slot], sem.at[slot]).wait()
            xb_ref[...] = xrows_ref[slot].astype(BF16)

        @pl.when(i + 1 < n_used)
        def _():
            start_gather(i + 1, 1 - slot)

    @pl.when(i < n_used)
    def _():
        xb = xb_ref[...]
        gate = jnp.dot(xb, wg_ref[...], preferred_element_type=F32)
        up = jnp.dot(xb, wu_ref[...], preferred_element_type=F32)
        part = jnp.dot((_silu(gate) * up).astype(BF16), wd_ref[...], preferred_element_type=F32)

        @pl.when(f == 0)
        def _():
            y_ref[...] = part

        @pl.when(f > 0)
        def _():
            y_ref[...] += part

    @pl.when(jnp.logical_and(i >= n_used, f == 0))
    def _():
        y_ref[...] = jnp.zeros(y_ref.shape, F32)


def _moe_ffn(x, src, tile_expert, n_used, w_gu, w_down, tm_e, tf):
    T, D = x.shape
    E, F, _ = w_down.shape
    nf = F // tf
    n_tiles = tile_expert.shape[0]
    grid_spec = pltpu.PrefetchScalarGridSpec(
        num_scalar_prefetch=3,
        grid=(n_tiles, nf),
        in_specs=[pl.BlockSpec(memory_space=pl.ANY),
                  pl.BlockSpec((None, D, tf), lambda i, f, te, nv, src: (te[i], 0, f)),
                  pl.BlockSpec((None, D, tf), lambda i, f, te, nv, src: (te[i], 0, nf + f)),
                  pl.BlockSpec((None, tf, D), lambda i, f, te, nv, src: (te[i], f, 0))],
        out_specs=pl.BlockSpec((tm_e, D), lambda i, f, te, nv, src: (i, 0)),
        scratch_shapes=[pltpu.VMEM((2, tm_e, D), F32), pltpu.VMEM((tm_e, D), BF16), pltpu.SemaphoreType.DMA((2,))],
    )
    return pl.pallas_call(
        _moe_ffn_kernel,
        grid_spec=grid_spec,
        out_shape=jax.ShapeDtypeStruct((n_tiles * tm_e, D), F32),
        compiler_params=_cparams("arbitrary", "arbitrary"),
        name="moe_ffn",
    )(tile_expert, n_used, src, x, w_gu, w_gu, w_down)


def _moe_combine_kernel(d1_ref, d2_ref, x_ref, gate_ref, g_ref, b_ref, y_hbm, o_ref, yrows_ref, sem, *, alpha):
    tm = yrows_ref.shape[2]
    i = pl.program_id(0)
    slot = lax.rem(i, 2)

    def start_gather(tile, slot):
        def body(r8, carry):
            for u in range(ROW_COPY_UNROLL):
                r = r8 * ROW_COPY_UNROLL + u
                _row_copy(y_hbm, d1_ref[tile * tm + r], yrows_ref.at[slot, 0], r, sem.at[slot]).start()
                _row_copy(y_hbm, d2_ref[tile * tm + r], yrows_ref.at[slot, 1], r, sem.at[slot]).start()
            return carry
        lax.fori_loop(0, tm // ROW_COPY_UNROLL, body, 0)

    @pl.when(i == 0)
    def _():
        start_gather(0, 0)

    for k in range(2):
        pltpu.make_async_copy(y_hbm.at[pl.ds(0, tm), :], yrows_ref.at[slot, k], sem.at[slot]).wait()

    @pl.when(i + 1 < pl.num_programs(0))
    def _():
        start_gather(i + 1, 1 - slot)

    y = yrows_ref[slot, 0] * gate_ref[:, 0:1] + yrows_ref[slot, 1] * gate_ref[:, 1:2]
    o_ref[...] = _layer_norm(alpha * x_ref[...] + y, g_ref[...], b_ref[...])


def _moe_combine(x, y_rows, dest, top_g, g, b, alpha, tm):
    T, D = x.shape
    row = pl.BlockSpec((tm, D), lambda i, d1, d2: (i, 0))
    grid_spec = pltpu.PrefetchScalarGridSpec(
        num_scalar_prefetch=2,
        grid=(T // tm,),
        in_specs=[row, pl.BlockSpec((tm, 2), lambda i, d1, d2: (i, 0)),
                  pl.BlockSpec(g.shape, lambda i, d1, d2: (0, 0)), pl.BlockSpec(b.shape, lambda i, d1, d2: (0, 0)),
                  pl.BlockSpec(memory_space=pl.ANY)],
        out_specs=row,
        scratch_shapes=[pltpu.VMEM((2, 2, tm, D), F32), pltpu.SemaphoreType.DMA((2,))],
    )
    return pl.pallas_call(
        functools.partial(_moe_combine_kernel, alpha=alpha),
        grid_spec=grid_spec,
        out_shape=jax.ShapeDtypeStruct((T, D), F32),
        compiler_params=_cparams("arbitrary"),
        name="moe_combine",
    )(dest[:, 0], dest[:, 1], x, top_g, g, b, y_rows)


def _moe_kernel(x_ref, gate_ref, wg_ref, wu_ref, wd_ref, g_ref, b_ref, o_ref, acc_ref, *, alpha):
    e = pl.program_id(1)
    f = pl.program_id(2)
    x = x_ref[...]
    xb = x.astype(BF16)

    @pl.when(jnp.logical_and(e == 0, f == 0))
    def _():
        acc_ref[...] = alpha * x

    gates = gate_ref[...]
    lane = lax.broadcasted_iota(jnp.int32, gates.shape, 1)
    ge = jnp.sum(jnp.where(lane == e, gates, 0.0), axis=1, keepdims=True)
    gate = jnp.dot(xb, wg_ref[0], preferred_element_type=F32)
    up = jnp.dot(xb, wu_ref[0], preferred_element_type=F32)
    acc_ref[...] += jnp.dot((_silu(gate) * up).astype(BF16), wd_ref[0], preferred_element_type=F32) * ge

    @pl.when(jnp.logical_and(e == pl.num_programs(1) - 1, f == pl.num_programs(2) - 1))
    def _():
        o_ref[...] = _layer_norm(acc_ref[...], g_ref[...], b_ref[...])


def _moe(x, gates, w_gu, w_down, g, b, alpha, tm, tf):
    T, D = x.shape
    E, F, _ = w_down.shape
    nf = F // tf
    row = pl.BlockSpec((tm, D), lambda i, e, f: (i, 0))
    return pl.pallas_call(
        functools.partial(_moe_kernel, alpha=alpha),
        grid=(T // tm, E, nf),
        in_specs=[row, pl.BlockSpec((tm, E), lambda i, e, f: (i, 0)),
                  pl.BlockSpec((1, D, tf), lambda i, e, f: (e, 0, f)),
                  pl.BlockSpec((1, D, tf), lambda i, e, f: (e, 0, nf + f)),
                  pl.BlockSpec((1, tf, D), lambda i, e, f: (e, f, 0)), _full(g.shape), _full(b.shape)],
        out_specs=row,
        out_shape=jax.ShapeDtypeStruct((T, D), F32),
        scratch_shapes=[pltpu.VMEM((tm, D), F32)],
        compiler_params=_cparams("parallel", "arbitrary", "arbitrary"),
        name="moe",
    )(x, gates, w_gu, w_gu, w_down, g, b)


def _rope_tables(pos):
    half = ATT_QK_DIM // 2
    inv = ROPE_THETA ** (-jnp.arange(0, ATT_QK_DIM, 2, dtype=F32) / ATT_QK_DIM)
    ang = pos.astype(F32)[:, None] * inv[None, :]
    cos, sin = jnp.cos(ang), jnp.sin(ang)
    return jnp.concatenate([cos] * 4, axis=1), jnp.concatenate([-sin, sin, -sin, sin], axis=1)


def _row_tile(n, target):
    t = min(n, target)
    while n % t:
        t //= 2
    return t


def kernel(x_prompt, x_sample, cache_attn_k, cache_attn_v, cache_mem_k, cache_mem_v, state_conv, state_ssm,
           page_table, mem_prompt, ln_g, ln_b, w_in, conv_w, conv_b, dt_bias, a_log, d_skip, ssm_norm_w,
           lambda_params, attn_norm_w, w_out, w_cq, w_ckv, w_co, w_ffn_gu, w_ffn_down, w_router, w_exp_gu,
           w_exp_down):
    B, S, D = x_prompt.shape
    Bs = x_sample.shape[0]
    depth = w_in.shape[0]
    n_phys, page = cache_attn_k.shape[1], cache_attn_k.shape[2]
    n_pages = page_table.shape[1]
    past_len = n_pages * page
    n_mem = mem_prompt.shape[1]
    alpha = (2 * depth) ** 0.25
    assert S % SSD_CHUNK == 0 and x_sample.shape[1] == 1

    tm_p = _row_tile(S, 512)
    tm_s = _row_tile(Bs, 128)
    cos_p, sin_p = _rope_tables(jnp.arange(S, dtype=jnp.int32))
    cos_s, sin_s = _rope_tables(jnp.full((tm_s,), past_len, jnp.int32))

    hp = x_prompt.reshape(B * S, D)
    hs = x_sample.reshape(Bs, D)
    mem = mem_prompt.reshape(B * n_mem, D)
    cache_k = cache_attn_k.transpose(0, 1, 3, 4, 2)
    cache_v = cache_attn_v.reshape(depth, n_phys, page * ATT_HEADS, ATT_V_DIM)
    dh = D // MEM_HEADS
    mem_rows = lambda t: (t.reshape(depth, Bs, n_mem, MEM_HEADS, dh // LANES, LANES).swapaxes(3, 4)
                          .reshape(depth, Bs, n_mem * dh // LANES * MEM_HEADS, LANES))
    cmem_k, cmem_v = mem_rows(cache_mem_k), mem_rows(cache_mem_v)
    cstate = state_conv.reshape(depth, Bs, (CONV_WIDTH - 1) * CONV_DIM)
    pad_heads = lambda t: jnp.pad(t, [(0, 0)] * (t.ndim - 2) + [(0, 2 * ATT_HEADS - t.shape[-2]), (0, 0)])

    outs = {n: [] for n in ("cp", "sp", "ks", "vs", "cs")}
    kv_prompt = mem_kv_prompt = ssm_sample = None
    for l in range(depth):
        lam_init = 0.8 - 0.6 * math.exp(-0.3 * l)
        wl = w_in[l].astype(BF16)
        wdt = wl[:, 3 * ATT_COLS + SSM_INNER + CONV_DIM:]
        w_proj = (wl, wdt, wdt.T)
        sw = (conv_w[l], conv_b[l][None, :], dt_bias[l][None, :], dt_bias[l][:, None], a_log[l][None, :],
              a_log[l][:, None], d_skip[l][None, :], ssm_norm_w[l][None, :])
        lp = lambda_params[l]
        nw = attn_norm_w[l][None, :]
        wo = w_out[l].astype(BF16)
        wo_att, wo_ssm = wo[:ATT_COLS], wo[ATT_COLS:]
        wcq = w_cq[l].astype(BF16)
        wckv = w_ckv[l].astype(BF16)
        wco = w_co[l].astype(BF16)
        g0, g1, g2 = (ln_g[l, i][None, :] for i in range(3))
        b0, b1, b2 = (ln_b[l, i][None, :] for i in range(3))

        q, k_all, kb, v_all, vb, z, xbc, dt, dtT = _in_proj(hp, cos_p, sin_p, w_proj, tm_p, S // tm_p,
                                                            cache=(kv_prompt,))
        kv_prompt = (k_all, v_all)
        att = _attention(q, kb, vb, lp, nw, lam_init, B, S)
        y, conv_new, h_last = _ssd(xbc, z, dt, dtT, sw, B, S)
        hp = _proj_ln((att, y), (wo_att, wo_ssm), hp, g0, b0, alpha, _row_tile(S, 1024))
        outs["cp"].append(conv_new)
        outs["sp"].append(h_last.reshape(B, SSM_HEADS, SSM_HEAD_DIM, SSM_STATE))
        mk_all, mv_all, mk_b, mv_b = _mem_kv(mem, wckv, mem_kv_prompt, _row_tile(B * n_mem, 256))
        mem_kv_prompt = (mk_all, mv_all)
        hp = _cross_attention(hp, wcq, mk_b, mv_b, wco, g1, b1, alpha, B, S, n_mem, _row_tile(S, 1024))

        q, kf, _, vf, _, z, xbc, dt, _ = _in_proj(hs, cos_s, sin_s, w_proj, tm_s, 1)
        att = _decode_attention(page_table, l, q.astype(F32).reshape(Bs, 1, ATT_COLS), kf.reshape(Bs, 1, ATT_COLS),
                                pad_heads(vf.reshape(Bs, ATT_HEADS, ATT_V_DIM)), cache_k, cache_v, lp, nw, lam_init)
        att = att[:, :ATT_HEADS].reshape(Bs, ATT_COLS)
        y, conv_new, ssm_sample = _ssd_step(xbc, z, dt, cstate, state_ssm, l, sw, ssm_sample)
        hs = _proj_ln((att, y), (wo_att, wo_ssm), hs, g0, b0, alpha, tm_s)
        outs["ks"].append(kf.reshape(Bs, 1, 2 * ATT_HEADS, ATT_QK_DIM))
        outs["vs"].append(vf.reshape(Bs, 1, ATT_HEADS, ATT_V_DIM))
        outs["cs"].append(conv_new.reshape(Bs, CONV_WIDTH - 1, CONV_DIM))
        (qc,) = _matmul(hs, wcq, (F32,), tm_s, D)
        q_rows = pad_heads(qc.reshape(Bs, MEM_HEADS, dh // LANES, LANES).swapaxes(1, 2))
        oc = _cross_decode(q_rows.reshape(Bs, -1, LANES), cmem_k, cmem_v, l)
        oc = oc.reshape(Bs, dh // LANES, 2 * ATT_HEADS, LANES)[:, :, :MEM_HEADS].swapaxes(1, 2).reshape(Bs, D)
        hs = _proj_ln((oc,), (wco,), hs, g1, b1, alpha, tm_s)

        if l % 2 == 0:
            wgu = w_ffn_gu[l // 2].astype(BF16)
            wd = w_ffn_down[l // 2].astype(BF16)
            tf = wd.shape[0]
            hp = _ffn(hp, wgu, wd, g2, b2, alpha, tm_p, tf)
            hs = _ffn(hs, wgu, wd, g2, b2, alpha, tm_s, tf)
        else:
            wr = w_router[l // 2].astype(BF16)
            wgu = w_exp_gu[l // 2].astype(BF16)
            wd = w_exp_down[l // 2].astype(BF16)
            tf = wd.shape[1] // MOE_F_STEPS
            _, top_i, top_g = _router(hp, wr, tm_p)
            tm_e = _row_tile(2 * B * S, MOE_ROW_TILE)
            src, dest, tile_expert, n_used = _moe_plan(top_i, tm_e, 2 * B * S // tm_e + N_EXPERTS)
            y_rows = _moe_ffn(hp, src, tile_expert, n_used, wgu, wd, tm_e, tf)
            hp = _moe_combine(hp, y_rows, dest, top_g, g2, b2, alpha, tm_p)
            hs = _moe(hs, _router(hs, wr, tm_s)[0], wgu, wd, g2, b2, alpha, tm_s, tf)

    st = {n: jnp.stack(v) for n, v in outs.items()}
    k_prompt = kv_prompt[0].reshape(depth, B, 2 * ATT_HEADS, ATT_QK_DIM, S).transpose(0, 1, 4, 2, 3)
    v_prompt = kv_prompt[1].reshape(depth, B, S, ATT_HEADS, ATT_V_DIM)
    mem_shape = (depth, B, n_mem, dh // LANES, MEM_HEADS, LANES)
    mem_k_prompt, mem_v_prompt = (t.reshape(mem_shape).swapaxes(3, 4).reshape(depth, B, n_mem, MEM_HEADS, dh)
                                  for t in mem_kv_prompt)
    return (hp.reshape(B, S, D), hs.reshape(Bs, 1, D), k_prompt, v_prompt, mem_k_prompt, mem_v_prompt, st["cp"],
            st["sp"], st["ks"], st["vs"], st["cs"], ssm_sample)
```
